```python
import jax, jax.numpy as jnp
from jax import lax
import numpy as np

D_MODEL = 1024
BATCH = 8
SEQ = 16384
DEPTH = 2

N_A_LAYERS = DEPTH // 2
N_B_LAYERS = DEPTH - N_A_LAYERS

CONV_WIDTH = 31
CONV_CH = D_MODEL

HEAD_DIM = 64
HEADS_PER_GROUP = D_MODEL // HEAD_DIM
ATTN_WIDTH = HEADS_PER_GROUP * HEAD_DIM
DILATED_GROUPS = ((128, 1), (512, 4), (2048, 16))
N_GROUPS = len(DILATED_GROUPS)
Q_WIDTH = N_GROUPS * ATTN_WIDTH
BLOCK = 128
ALIBI_MAX_EXP = 8.0

ALPHA = (2.0 * DEPTH) ** 0.25
BETA = (8.0 * DEPTH) ** -0.25
LN_EPS = 1e-5

kernel_name = "yoco_conformer_conv_dilated_attn_deepnorm"


def layer_norm(x, g, b):
    xf = x.astype(jnp.float32)
    mu = jnp.mean(xf, axis=-1, keepdims=True)
    xc = xf - mu
    var = jnp.mean(xc * xc, axis=-1, keepdims=True)
    y = xc * lax.rsqrt(var + LN_EPS) * g.astype(jnp.float32) + b.astype(jnp.float32)
    return y.astype(x.dtype)


def alibi_slopes(n_heads):
    h = jnp.arange(1, n_heads + 1, dtype=jnp.float32)
    return jnp.exp2(-ALIBI_MAX_EXP * h / n_heads)


def conformer_conv_branch(x, w_in, b_in, w_dw, b_dw, ln_g, ln_b, w_out, b_out):
    h = x @ w_in + b_in
    a, a_gate, z = jnp.split(h, 3, axis=-1)
    u = a * jax.nn.sigmoid(a_gate)
    u = lax.conv_general_dilated(
        u, w_dw[:, None, :].astype(u.dtype), window_strides=(1,),
        padding=((CONV_WIDTH - 1, 0),),
        dimension_numbers=('NWC', 'WIO', 'NWC'),
        feature_group_count=CONV_CH) + b_dw
    u = jax.nn.silu(layer_norm(u, ln_g, ln_b))
    return (u * jax.nn.silu(z)) @ w_out + b_out


def dilated_window_attention(q, k, v, slopes, window, dilation):
    B, S, H, hd = q.shape
    n_back = window // dilation
    span = dilation * BLOCK
    s_pad = -(-S // span) * span
    L = s_pad // dilation
    nb = L // BLOCK
    pad = ((0, 0), (0, s_pad - S), (0, 0), (0, 0))

    def to_blocks(t):
        t = jnp.pad(t.astype(jnp.float32), pad).reshape(B, L, dilation, H, hd)
        return t.transpose(0, 2, 1, 3, 4).reshape(B, dilation, nb, BLOCK, H, hd)

    def with_prev(t):
        prev = jnp.pad(t[:, :, :-1], ((0, 0), (0, 0), (1, 0), (0, 0), (0, 0), (0, 0)))
        return jnp.concatenate([prev, t], axis=3)

    qb = to_blocks(q)
    kk = with_prev(to_blocks(k))
    vv = with_prev(to_blocks(v))

    scores = jnp.einsum('brnqhd,brnkhd->brnhqk', qb, kk) * (hd ** -0.5)
    qi = jnp.arange(BLOCK)[:, None]
    kj = jnp.arange(2 * BLOCK)[None, :]
    dist = qi + BLOCK - kj
    band = (dist >= 0) & (dist <= n_back)
    first = (jnp.arange(nb) == 0)[:, None, None]
    valid = band[None] & ~(first & (kj < BLOCK)[None])
    bias = -slopes[:, None, None] * (dilation * dist).astype(jnp.float32)[None]
    scores = jnp.where(valid[None, None, :, None], scores + bias, -jnp.inf)

    m = jnp.max(scores, axis=-1, keepdims=True)
    p = jnp.exp(scores - m)
    denom = jnp.sum(p, axis=-1, keepdims=True)
    o = jnp.einsum('brnhqk,brnkhd->brnqhd', p / denom, vv)
    lse = (m + jnp.log(denom))[..., 0]

    o = o.reshape(B, dilation, L, H, hd).transpose(0, 2, 1, 3, 4).reshape(B, s_pad, H, hd)[:, :S]
    lse = lse.transpose(0, 1, 2, 4, 3).reshape(B, dilation, L, H).transpose(0, 2, 1, 3)
    lse = lse.reshape(B, s_pad, H)[:, :S]
    return o, lse


def dilated_attention_branch(x, w_in, w_out, b_out, k_shared, v_shared):
    B, S, _ = x.shape
    h = x @ w_in
    q = h[..., :Q_WIDTH].reshape(B, S, N_GROUPS, HEADS_PER_GROUP, HEAD_DIM)
    z = h[..., Q_WIDTH:]
    slopes = alibi_slopes(HEADS_PER_GROUP)
    outs, lses = [], []
    for g, (window, dilation) in enumerate(DILATED_GROUPS):
        o, l = dilated_window_attention(q[:, :, g], k_shared[:, :, g], v_shared[:, :, g],
                                        slopes, window, dilation)
        outs.append(o)
        lses.append(l)
    wts = jax.nn.softmax(jnp.stack(lses, axis=0), axis=0)
    o = jnp.sum(wts[..., None] * jnp.stack(outs, axis=0), axis=0)
    o = o.reshape(B, S, ATTN_WIDTH).astype(x.dtype)
    return (o * jax.nn.silu(z)) @ w_out + b_out


def _fwd_setup_inputs(seed: int = 0) -> dict:
    key = jax.random.key(seed)
    ks = jax.random.split(key, 16)
    f32 = jnp.float32
    nrm = lambda k, shape: jax.random.normal(k, shape, dtype=f32)
    C, D = CONV_CH, D_MODEL
    return {
        "x": nrm(ks[0], (BATCH, SEQ, D)),
        "a_w_in": nrm(ks[1], (N_A_LAYERS, D, 3 * C)) * D ** -0.5,
        "a_b_in": 0.02 * nrm(ks[2], (N_A_LAYERS, 3 * C)),
        "a_w_dw": nrm(ks[3], (N_A_LAYERS, CONV_WIDTH, C)) * CONV_WIDTH ** -0.5,
        "a_b_dw": 0.02 * nrm(ks[4], (N_A_LAYERS, C)),
        "a_ln_g": 1.0 + 0.02 * nrm(ks[5], (N_A_LAYERS, C)),
        "a_ln_b": 0.02 * nrm(ks[6], (N_A_LAYERS, C)),
        "a_w_out": nrm(ks[7], (N_A_LAYERS, C, D)) * (C ** -0.5 * BETA),
        "a_b_out": 0.02 * nrm(ks[8], (N_A_LAYERS, D)),
        "kv_w": nrm(ks[9], (D, 2 * Q_WIDTH)) * D ** -0.5,
        "b_w_in": nrm(ks[10], (N_B_LAYERS, D, Q_WIDTH + ATTN_WIDTH)) * D ** -0.5,
        "b_w_out": nrm(ks[11], (N_B_LAYERS, ATTN_WIDTH, D)) * (ATTN_WIDTH ** -0.5 * BETA),
        "b_b_out": 0.02 * nrm(ks[12], (N_B_LAYERS, D)),
        "post_ln_g": 1.0 + 0.02 * nrm(ks[13], (DEPTH, D)),
        "post_ln_b": 0.02 * nrm(ks[14], (DEPTH, D)),
    }


def _fwd_reference(x, a_w_in, a_b_in, a_w_dw, a_b_dw, a_ln_g, a_ln_b, a_w_out, a_b_out,
              kv_w, b_w_in, b_w_out, b_b_out, post_ln_g, post_ln_b):
    B, S, _ = x.shape
    k_shared = v_shared = None
    for layer in range(DEPTH):
        if layer < N_A_LAYERS:
            i = layer
            y = conformer_conv_branch(x, a_w_in[i], a_b_in[i], a_w_dw[i], a_b_dw[i],
                                      a_ln_g[i], a_ln_b[i], a_w_out[i], a_b_out[i])
        else:
            if layer == N_A_LAYERS:
                kv = (x @ kv_w).reshape(B, S, 2, N_GROUPS, HEADS_PER_GROUP, HEAD_DIM)
                k_shared, v_shared = kv[:, :, 0], kv[:, :, 1]
            i = layer - N_A_LAYERS
            y = dilated_attention_branch(x, b_w_in[i], b_w_out[i], b_b_out[i], k_shared, v_shared)
        x = layer_norm(ALPHA * x + y, post_ln_g[layer], post_ln_b[layer])
    return x


import jax as _jax
import jax.numpy as _jnp

TWIN_FORMAT = 'train_step'
FWD_PARAMS = ['x', 'a_w_in', 'a_b_in', 'a_w_dw', 'a_b_dw', 'a_ln_g', 'a_ln_b', 'a_w_out', 'a_b_out', 'kv_w', 'b_w_in', 'b_w_out', 'b_b_out', 'post_ln_g', 'post_ln_b']
TWIN_WEIGHTS = ['a_w_in', 'a_b_in', 'a_w_dw', 'a_b_dw', 'a_ln_g', 'a_ln_b', 'a_w_out', 'a_b_out', 'kv_w', 'b_w_in', 'b_w_out', 'b_b_out', 'post_ln_g', 'post_ln_b']
TWIN_DIFF_INPUT = 'x'
TWIN_INPUTS = ['x', 'a_w_in', 'a_b_in', 'a_w_dw', 'a_b_dw', 'a_ln_g', 'a_ln_b', 'a_w_out', 'a_b_out', 'kv_w', 'b_w_in', 'b_w_out', 'b_b_out', 'post_ln_g', 'post_ln_b', 'loss_target', 'm_a_w_in', 'm_a_b_in', 'm_a_w_dw', 'm_a_b_dw', 'm_a_ln_g', 'm_a_ln_b', 'm_a_w_out', 'm_a_b_out', 'm_kv_w', 'm_b_w_in', 'm_b_w_out', 'm_b_b_out', 'm_post_ln_g', 'm_post_ln_b', 'v_a_w_in', 'v_a_b_in', 'v_a_w_dw', 'v_a_b_dw', 'v_a_ln_g', 'v_a_ln_b', 'v_a_w_out', 'v_a_b_out', 'v_kv_w', 'v_b_w_in', 'v_b_w_out', 'v_b_b_out', 'v_post_ln_g', 'v_post_ln_b']
TWIN_OUTPUTS = ['loss', 'grad_x', 'grad_a_w_in', 'grad_a_b_in', 'grad_a_w_dw', 'grad_a_b_dw', 'grad_a_ln_g', 'grad_a_ln_b', 'grad_a_w_out', 'grad_a_b_out', 'grad_kv_w', 'grad_b_w_in', 'grad_b_w_out', 'grad_b_b_out', 'grad_post_ln_g', 'grad_post_ln_b', 'delta_a_w_in', 'delta_a_b_in', 'delta_a_w_dw', 'delta_a_b_dw', 'delta_a_ln_g', 'delta_a_ln_b', 'delta_a_w_out', 'delta_a_b_out', 'delta_kv_w', 'delta_b_w_in', 'delta_b_w_out', 'delta_b_b_out', 'delta_post_ln_g', 'delta_post_ln_b', 'new_m_a_w_in', 'new_m_a_b_in', 'new_m_a_w_dw', 'new_m_a_b_dw', 'new_m_a_ln_g', 'new_m_a_ln_b', 'new_m_a_w_out', 'new_m_a_b_out', 'new_m_kv_w', 'new_m_b_w_in', 'new_m_b_w_out', 'new_m_b_b_out', 'new_m_post_ln_g', 'new_m_post_ln_b', 'new_v_a_w_in', 'new_v_a_b_in', 'new_v_a_w_dw', 'new_v_a_b_dw', 'new_v_a_ln_g', 'new_v_a_ln_b', 'new_v_a_w_out', 'new_v_a_b_out', 'new_v_kv_w', 'new_v_b_w_in', 'new_v_b_w_out', 'new_v_b_b_out', 'new_v_post_ln_g', 'new_v_post_ln_b']
TWIN_LEAF_KINDS = {'loss': 'loss', 'grad_x': 'grad_x', 'grad_a_w_in': 'grad_w', 'grad_a_b_in': 'grad_w', 'grad_a_w_dw': 'grad_w', 'grad_a_b_dw': 'grad_w', 'grad_a_ln_g': 'grad_w', 'grad_a_ln_b': 'grad_w', 'grad_a_w_out': 'grad_w', 'grad_a_b_out': 'grad_w', 'grad_kv_w': 'grad_w', 'grad_b_w_in': 'grad_w', 'grad_b_w_out': 'grad_w', 'grad_b_b_out': 'grad_w', 'grad_post_ln_g': 'grad_w', 'grad_post_ln_b': 'grad_w', 'delta_a_w_in': 'delta_w', 'delta_a_b_in': 'delta_w', 'delta_a_w_dw': 'delta_w', 'delta_a_b_dw': 'delta_w', 'delta_a_ln_g': 'delta_w', 'delta_a_ln_b': 'delta_w', 'delta_a_w_out': 'delta_w', 'delta_a_b_out': 'delta_w', 'delta_kv_w': 'delta_w', 'delta_b_w_in': 'delta_w', 'delta_b_w_out': 'delta_w', 'delta_b_b_out': 'delta_w', 'delta_post_ln_g': 'delta_w', 'delta_post_ln_b': 'delta_w', 'new_m_a_w_in': 'new_m', 'new_m_a_b_in': 'new_m', 'new_m_a_w_dw': 'new_m', 'new_m_a_b_dw': 'new_m', 'new_m_a_ln_g': 'new_m', 'new_m_a_ln_b': 'new_m', 'new_m_a_w_out': 'new_m', 'new_m_a_b_out': 'new_m', 'new_m_kv_w': 'new_m', 'new_m_b_w_in': 'new_m', 'new_m_b_w_out': 'new_m', 'new_m_b_b_out': 'new_m', 'new_m_post_ln_g': 'new_m', 'new_m_post_ln_b': 'new_m', 'new_v_a_w_in': 'new_v', 'new_v_a_b_in': 'new_v', 'new_v_a_w_dw': 'new_v', 'new_v_a_b_dw': 'new_v', 'new_v_a_ln_g': 'new_v', 'new_v_a_ln_b': 'new_v', 'new_v_a_w_out': 'new_v', 'new_v_a_b_out': 'new_v', 'new_v_kv_w': 'new_v', 'new_v_b_w_in': 'new_v', 'new_v_b_w_out': 'new_v', 'new_v_b_b_out': 'new_v', 'new_v_post_ln_g': 'new_v', 'new_v_post_ln_b': 'new_v'}


def _forward(args):
    return _fwd_reference(*[args[k] for k in FWD_PARAMS])


def _output_shape():
    def fwd():
        inp = _fwd_setup_inputs(0)
        return _fwd_reference(*[inp[k] for k in FWD_PARAMS])
    out = _jax.eval_shape(fwd)
    return out.shape, out.dtype

N_MICROBATCH = 1
ADAM_LR = 0.001
ADAM_B1 = 0.9
ADAM_B2 = 0.999
ADAM_EPS = 1e-08
ADAM_WD = 0.01
ADAM_STEP = 10
PER_EXAMPLE_BATCH_AXIS = {'x': 0, 'loss_target': 0}
SHARED_INPUTS = []
_WEIGHT_DTYPES = {'a_w_in': _jnp.float32, 'a_b_in': _jnp.float32, 'a_w_dw': _jnp.float32, 'a_b_dw': _jnp.float32, 'a_ln_g': _jnp.float32, 'a_ln_b': _jnp.float32, 'a_w_out': _jnp.float32, 'a_b_out': _jnp.float32, 'kv_w': _jnp.float32, 'b_w_in': _jnp.float32, 'b_w_out': _jnp.float32, 'b_b_out': _jnp.float32, 'post_ln_g': _jnp.float32, 'post_ln_b': _jnp.float32}
MOMENT_SCALE = {'a_w_in': 3.930981e-02, 'a_b_in': 5.377307e-02, 'a_w_dw': 4.664449e-02, 'a_b_dw': 1.041786e-01, 'a_ln_g': 6.443934e-02, 'a_ln_b': 6.495159e-02, 'a_w_out': 9.537368e-02, 'a_b_out': 1.347711e+00, 'kv_w': 1.258362e-02, 'b_w_in': 1.570162e-02, 'b_w_out': 4.870335e-02, 'b_b_out': 1.357124e+00, 'post_ln_g': 9.065120e+01, 'post_ln_b': 2.952527e+00}


def _to_microbatches(a, axis):
    t = _jnp.moveaxis(a, axis, 0)
    t = t.reshape((N_MICROBATCH, t.shape[0] // N_MICROBATCH) + t.shape[1:])
    return _jnp.moveaxis(t, 1, axis + 1)


def setup_inputs(seed: int = 0) -> dict:
    inp = _fwd_setup_inputs(seed)
    key = _jax.random.fold_in(_jax.random.key(seed), 7919)
    shape, _ = _output_shape()
    out = dict(inp)
    out["loss_target"] = _jax.random.normal(_jax.random.fold_in(key, 0), shape, _jnp.float32)
    for i, name in enumerate(TWIN_WEIGHTS):
        w = inp[name].astype(_jnp.float32)
        if MOMENT_SCALE is None:
            s = _jnp.sqrt(_jnp.mean(_jnp.square(w)) + 1e-30)
        else:
            s = MOMENT_SCALE[name]
        km, kv = _jax.random.split(_jax.random.fold_in(key, i + 1))
        out[name] = w
        out["m_" + name] = s * _jax.random.normal(km, w.shape, _jnp.float32)
        out["v_" + name] = (s * s) * _jax.random.uniform(kv, w.shape, _jnp.float32, 0.5, 1.5)
    if N_MICROBATCH > 1:
        for name, axis in PER_EXAMPLE_BATCH_AXIS.items():
            out[name] = _to_microbatches(out[name], axis)
    return {'x': out['x'], 'a_w_in': out['a_w_in'], 'a_b_in': out['a_b_in'], 'a_w_dw': out['a_w_dw'], 'a_b_dw': out['a_b_dw'], 'a_ln_g': out['a_ln_g'], 'a_ln_b': out['a_ln_b'], 'a_w_out': out['a_w_out'], 'a_b_out': out['a_b_out'], 'kv_w': out['kv_w'], 'b_w_in': out['b_w_in'], 'b_w_out': out['b_w_out'], 'b_b_out': out['b_b_out'], 'post_ln_g': out['post_ln_g'], 'post_ln_b': out['post_ln_b'], 'loss_target': out['loss_target'], 'm_a_w_in': out['m_a_w_in'], 'm_a_b_in': out['m_a_b_in'], 'm_a_w_dw': out['m_a_w_dw'], 'm_a_b_dw': out['m_a_b_dw'], 'm_a_ln_g': out['m_a_ln_g'], 'm_a_ln_b': out['m_a_ln_b'], 'm_a_w_out': out['m_a_w_out'], 'm_a_b_out': out['m_a_b_out'], 'm_kv_w': out['m_kv_w'], 'm_b_w_in': out['m_b_w_in'], 'm_b_w_out': out['m_b_w_out'], 'm_b_b_out': out['m_b_b_out'], 'm_post_ln_g': out['m_post_ln_g'], 'm_post_ln_b': out['m_post_ln_b'], 'v_a_w_in': out['v_a_w_in'], 'v_a_b_in': out['v_a_b_in'], 'v_a_w_dw': out['v_a_w_dw'], 'v_a_b_dw': out['v_a_b_dw'], 'v_a_ln_g': out['v_a_ln_g'], 'v_a_ln_b': out['v_a_ln_b'], 'v_a_w_out': out['v_a_w_out'], 'v_a_b_out': out['v_a_b_out'], 'v_kv_w': out['v_kv_w'], 'v_b_w_in': out['v_b_w_in'], 'v_b_w_out': out['v_b_w_out'], 'v_b_b_out': out['v_b_b_out'], 'v_post_ln_g': out['v_post_ln_g'], 'v_post_ln_b': out['v_post_ln_b']}


def _loss(weights, diff, rest, loss_target):
    with _jax.named_scope("forward"):
        args = {**rest, TWIN_DIFF_INPUT: diff, **{k: w.astype(_WEIGHT_DTYPES[k]) for k, w in weights.items()}}
        y = _forward(args)
    with _jax.named_scope("loss_head"):
        err = _jnp.square(y.astype(_jnp.float32) - loss_target)
        return 0.5 * _jnp.sum(_jnp.mean(err, axis=-1)) if err.ndim else 0.5 * err


def _adamw(w, g, m, v):
    m = ADAM_B1 * m + (1.0 - ADAM_B1) * g
    v = ADAM_B2 * v + (1.0 - ADAM_B2) * _jnp.square(g)
    m_hat = m / (1.0 - ADAM_B1 ** ADAM_STEP)
    v_hat = v / (1.0 - ADAM_B2 ** ADAM_STEP)
    delta = -ADAM_LR * (m_hat / (_jnp.sqrt(v_hat) + ADAM_EPS) + ADAM_WD * w)
    return delta, m, v


def reference(x, a_w_in, a_b_in, a_w_dw, a_b_dw, a_ln_g, a_ln_b, a_w_out, a_b_out, kv_w, b_w_in, b_w_out, b_b_out, post_ln_g, post_ln_b, loss_target, m_a_w_in, m_a_b_in, m_a_w_dw, m_a_b_dw, m_a_ln_g, m_a_ln_b, m_a_w_out, m_a_b_out, m_kv_w, m_b_w_in, m_b_w_out, m_b_b_out, m_post_ln_g, m_post_ln_b, v_a_w_in, v_a_b_in, v_a_w_dw, v_a_b_dw, v_a_ln_g, v_a_ln_b, v_a_w_out, v_a_b_out, v_kv_w, v_b_w_in, v_b_w_out, v_b_b_out, v_post_ln_g, v_post_ln_b):
    given = dict(x=x, a_w_in=a_w_in, a_b_in=a_b_in, a_w_dw=a_w_dw, a_b_dw=a_b_dw, a_ln_g=a_ln_g, a_ln_b=a_ln_b, a_w_out=a_w_out, a_b_out=a_b_out, kv_w=kv_w, b_w_in=b_w_in, b_w_out=b_w_out, b_b_out=b_b_out, post_ln_g=post_ln_g, post_ln_b=post_ln_b, loss_target=loss_target, m_a_w_in=m_a_w_in, m_a_b_in=m_a_b_in, m_a_w_dw=m_a_w_dw, m_a_b_dw=m_a_b_dw, m_a_ln_g=m_a_ln_g, m_a_ln_b=m_a_ln_b, m_a_w_out=m_a_w_out, m_a_b_out=m_a_b_out, m_kv_w=m_kv_w, m_b_w_in=m_b_w_in, m_b_w_out=m_b_w_out, m_b_b_out=m_b_b_out, m_post_ln_g=m_post_ln_g, m_post_ln_b=m_post_ln_b, v_a_w_in=v_a_w_in, v_a_b_in=v_a_b_in, v_a_w_dw=v_a_w_dw, v_a_b_dw=v_a_b_dw, v_a_ln_g=v_a_ln_g, v_a_ln_b=v_a_ln_b, v_a_w_out=v_a_w_out, v_a_b_out=v_a_b_out, v_kv_w=v_kv_w, v_b_w_in=v_b_w_in, v_b_w_out=v_b_w_out, v_b_b_out=v_b_b_out, v_post_ln_g=v_post_ln_g, v_post_ln_b=v_post_ln_b)
    weights = {n: given[n] for n in TWIN_WEIGHTS}
    shared = {n: given[n] for n in SHARED_INPUTS}
    per_example = {n: given[n] for n in ['x']}
    grad_fn = _jax.value_and_grad(_loss, argnums=(0, 1))

    def one_microbatch(ex, loss_target):
        ex = dict(ex)
        diff = ex.pop(TWIN_DIFF_INPUT)
        return grad_fn(weights, diff, {**shared, **ex}, loss_target)

    if N_MICROBATCH == 1:
        loss, (grad_w, grad_x) = one_microbatch(per_example, given["loss_target"])
    else:
        def body(carry, xs):
            loss_sum, grad_sum = carry
            l_k, (gw_k, gx_k) = one_microbatch(xs[0], xs[1])
            with _jax.named_scope("update"):
                return (loss_sum + l_k, _jax.tree.map(_jnp.add, grad_sum, gw_k)), gx_k

        init = (_jnp.zeros((), _jnp.float32), _jax.tree.map(_jnp.zeros_like, weights))
        (loss, grad_w), grad_x = _jax.lax.scan(body, init, (per_example, given["loss_target"]))
    with _jax.named_scope("update"):
        delta_w, new_m, new_v = {}, {}, {}
        for n in TWIN_WEIGHTS:
            delta_w[n], new_m[n], new_v[n] = _adamw(weights[n], grad_w[n], given["m_" + n], given["v_" + n])
    return (loss, grad_x, *[grad_w[n] for n in TWIN_WEIGHTS], *[delta_w[n] for n in TWIN_WEIGHTS],
            *[new_m[n] for n in TWIN_WEIGHTS], *[new_v[n] for n in TWIN_WEIGHTS])
```

```python
import functools

import numpy as np
import jax
import jax.numpy as jnp
from jax import lax
from jax.experimental import pallas as pl
from jax.experimental.pallas import tpu as pltpu

F32 = jnp.float32
BF16 = jnp.bfloat16
MESH = pl.DeviceIdType.MESH
SDS = jax.ShapeDtypeStruct

HEAD_DIM = 64
BLOCK = 128
DILATIONS = (1, 4, 16)
ALIBI_MAX_EXP = 8.0
CONV_WIDTH = 31
HALO = 32
CONV_ROWS = 32
LSE_LANES = 128
DEPTH = 2
ALPHA = (2.0 * DEPTH) ** 0.25
LN_EPS = 1e-5
ADAM_LR = 0.001
ADAM_B1 = 0.9
ADAM_B2 = 0.999
ADAM_EPS = 1e-08
ADAM_WD = 0.01
ADAM_STEP = 10
N_CHIPS = 4
N_DEV = 8
VMEM_LIMIT = 56 * 2 ** 20
SMALL_ROWS = 48


def _params(sem=None):
    return pltpu.CompilerParams(dimension_semantics=sem, vmem_limit_bytes=VMEM_LIMIT)


def _sigmoid(x):
    return 1.0 / (1.0 + jnp.exp(-x))


def _silu_grad(x, s):
    return s * (1.0 + x * (1.0 - s))


def _ln_fwd(x):
    mu = jnp.mean(x, axis=-1, keepdims=True)
    xc = x - mu
    var = jnp.mean(xc * xc, axis=-1, keepdims=True)
    rstd = lax.rsqrt(var + LN_EPS)
    return xc * rstd, rstd


def _ln_bwd(dxhat, xhat, rstd):
    m1 = jnp.mean(dxhat, axis=-1, keepdims=True)
    m2 = jnp.mean(dxhat * xhat, axis=-1, keepdims=True)
    return rstd * (dxhat - m1 - xhat * m2)


def _dot(a, b):
    return jnp.dot(a, b, preferred_element_type=F32)


def _dot_t(a, b):
    return lax.dot_general(a, b, (((1,), (1,)), ((), ())), preferred_element_type=F32)


def _tdot(a, b):
    return lax.dot_general(a, b, (((0,), (0,)), ((), ())), preferred_element_type=F32)


def _colsum(x):
    return jnp.sum(x, axis=0, keepdims=True)


def _slopes(n_heads):
    return [float(np.float32(2.0 ** (-ALIBI_MAX_EXP * (h + 1) / n_heads))) for h in range(n_heads)]


def _to_chunks(chunks_ref, x):
    for cc in range(chunks_ref.shape[0]):
        chunks_ref[cc] = x[:, cc * 128:(cc + 1) * 128]


def _from_chunks(chunks_ref):
    return jnp.concatenate([chunks_ref[cc] for cc in range(chunks_ref.shape[0])], axis=1)


def _deinterleave(chunks_ref, out_ref, d, dtype):
    rows = chunks_ref.shape[1] // d
    for r in range(d):
        for cc in range(chunks_ref.shape[0]):
            out_ref[r, :, cc * 128:(cc + 1) * 128] = chunks_ref[cc, pl.ds(r, rows, stride=d), :].astype(dtype)


def _interleave(in_ref, chunks_ref, d):
    rows = chunks_ref.shape[1] // d
    for r in range(d):
        for cc in range(chunks_ref.shape[0]):
            chunks_ref[cc, pl.ds(r, rows, stride=d), :] = in_ref[r, :, cc * 128:(cc + 1) * 128]


def _hbm_specs(n):
    return [pl.BlockSpec(memory_space=pl.ANY)] * n


def _position():
    x, y, c = lax.axis_index("x"), lax.axis_index("y"), lax.axis_index("c")
    return x, y, c


def _all_gather_chips(shards):
    n = len(shards)

    def body(*refs):
        ins, outs = refs[:n], refs[n:2 * n]
        send_sems, recv_sems, local_sems = refs[2 * n:]
        x, y, c = _position()
        j = 2 * x + y
        me, sibling = (x, y, c), (x, y, 1 - c)
        chips = [(1 - x, y), (x, 1 - y), (1 - x, 1 - y)]

        def copy(i, k, src, dst, to):
            return pltpu.make_async_remote_copy(
                src_ref=src, dst_ref=dst, send_sem=send_sems.at[i, k], recv_sem=recv_sems.at[i, k],
                device_id=to, device_id_type=MESH)

        local = [pltpu.make_async_copy(ins[i], outs[i].at[j], local_sems.at[i]) for i in range(n)]
        for cp in local:
            cp.start()
        first = []
        for i in range(n):
            for k, chip in enumerate(chips):
                cp = copy(i, k, ins[i].at[c], outs[i].at[j, c], (*chip, c))
                cp.start()
                first.append(cp)
        passed = []
        for i in range(n):
            for k, chip in enumerate(chips):
                pj = 2 * chip[0] + chip[1]
                copy(i, k, ins[i].at[c], outs[i].at[pj, c], me).wait_recv()
                cp = copy(i, 3 + k, outs[i].at[pj, c], outs[i].at[pj, c], sibling)
                cp.start()
                passed.append(cp)
        for i in range(n):
            for k, chip in enumerate(chips):
                pj = 2 * chip[0] + chip[1]
                copy(i, 3 + k, ins[i].at[c], outs[i].at[pj, 1 - c], me).wait_recv()
        for cp in first + passed:
            cp.wait_send()
        for cp in local:
            cp.wait()

    return pl.pallas_call(
        body, name="all_gather_chips",
        out_shape=[SDS((N_CHIPS,) + s.shape, s.dtype) for s in shards],
        in_specs=_hbm_specs(n), out_specs=_hbm_specs(n),
        scratch_shapes=[pltpu.SemaphoreType.DMA((n, 6)), pltpu.SemaphoreType.DMA((n, 6)),
                        pltpu.SemaphoreType.DMA((n,))],
    )(*shards)


def _pair_exchange(grads):
    n = len(grads)

    def body(*refs):
        ins, outs = refs[:n], refs[n:2 * n]
        send_sems, recv_sems, local_sems = refs[2 * n:]
        x, y, c = _position()
        sibling = (x, y, 1 - c)
        local, remote = [], []
        for i in range(n):
            for j in range(N_CHIPS):
                local.append(pltpu.make_async_copy(ins[i].at[j, c], outs[i].at[0, j], local_sems.at[i, j]))
                remote.append(pltpu.make_async_remote_copy(
                    src_ref=ins[i].at[j, 1 - c], dst_ref=outs[i].at[1, j],
                    send_sem=send_sems.at[i, j], recv_sem=recv_sems.at[i, j],
                    device_id=sibling, device_id_type=MESH))
        for cp in local + remote:
            cp.start()
        for cp in remote:
            cp.wait_recv()
        for cp in remote:
            cp.wait_send()
        for cp in local:
            cp.wait()

    return pl.pallas_call(
        body, name="grad_pair_exchange",
        out_shape=[SDS((2, N_CHIPS) + g.shape[2:], g.dtype) for g in grads],
        in_specs=_hbm_specs(n), out_specs=_hbm_specs(n),
        scratch_shapes=[pltpu.SemaphoreType.DMA((n, N_CHIPS)), pltpu.SemaphoreType.DMA((n, N_CHIPS)),
                        pltpu.SemaphoreType.DMA((n, N_CHIPS))],
    )(*grads)


def _chip_scatter(parts):
    n = len(parts)

    def body(*refs):
        ins, outs = refs[:n], refs[n:2 * n]
        send_sems, recv_sems, local_sems = refs[2 * n:]
        x, y, c = _position()
        j = 2 * x + y
        chips = [(1 - x, y), (x, 1 - y), (1 - x, 1 - y)]
        local, remote, landing = [], [], []
        for i in range(n):
            local.append(pltpu.make_async_copy(ins[i].at[j], outs[i].at[j], local_sems.at[i]))
            for k, chip in enumerate(chips):
                pj = 2 * chip[0] + chip[1]
                remote.append(pltpu.make_async_remote_copy(
                    src_ref=ins[i].at[pj], dst_ref=outs[i].at[j],
                    send_sem=send_sems.at[i, k], recv_sem=recv_sems.at[i, k],
                    device_id=(*chip, c), device_id_type=MESH))
                landing.append(pltpu.make_async_remote_copy(
                    src_ref=ins[i].at[pj], dst_ref=outs[i].at[pj],
                    send_sem=send_sems.at[i, k], recv_sem=recv_sems.at[i, k],
                    device_id=(*chip, c), device_id_type=MESH))
        for cp in local + remote:
            cp.start()
        for cp in landing:
            cp.wait_recv()
        for cp in remote:
            cp.wait_send()
        for cp in local:
            cp.wait()

    return pl.pallas_call(
        body, name="grad_chip_scatter",
        out_shape=[SDS(p.shape, p.dtype) for p in parts],
        in_specs=_hbm_specs(n), out_specs=_hbm_specs(n),
        scratch_shapes=[pltpu.SemaphoreType.DMA((n, 3)), pltpu.SemaphoreType.DMA((n, 3)),
                        pltpu.SemaphoreType.DMA((n,))],
    )(*parts)


def _pair_share(halves):
    n = len(halves)

    def body(*refs):
        ins, outs = refs[:n], refs[n:2 * n]
        send_sems, recv_sems, local_sems = refs[2 * n:]
        x, y, c = _position()
        sibling = (x, y, 1 - c)
        local, remote, landing = [], [], []
        for i in range(n):
            local.append(pltpu.make_async_copy(ins[i], outs[i].at[c], local_sems.at[i]))
            remote.append(pltpu.make_async_remote_copy(
                src_ref=ins[i], dst_ref=outs[i].at[c], send_sem=send_sems.at[i], recv_sem=recv_sems.at[i],
                device_id=sibling, device_id_type=MESH))
            landing.append(pltpu.make_async_remote_copy(
                src_ref=ins[i], dst_ref=outs[i].at[1 - c], send_sem=send_sems.at[i], recv_sem=recv_sems.at[i],
                device_id=sibling, device_id_type=MESH))
        for cp in local + remote:
            cp.start()
        for cp in landing:
            cp.wait_recv()
        for cp in remote:
            cp.wait_send()
        for cp in local:
            cp.wait()

    return pl.pallas_call(
        body, name="grad_pair_share",
        out_shape=[SDS((2,) + h.shape, h.dtype) for h in halves],
        in_specs=_hbm_specs(n), out_specs=_hbm_specs(n),
        scratch_shapes=[pltpu.SemaphoreType.DMA((n,)), pltpu.SemaphoreType.DMA((n,)),
                        pltpu.SemaphoreType.DMA((n,))],
    )(*halves)


def _gather_all_devices(slab):
    def body(in_ref, out_ref, send_sems, recv_sems, local_sem):
        x, y, c = _position()
        me = 4 * x + 2 * y + c
        local = pltpu.make_async_copy(in_ref, out_ref.at[me], local_sem)
        local.start()
        remote, landing = [], []
        for mask in range(1, N_DEV):
            px, py, pc = x ^ (mask >> 2), y ^ ((mask >> 1) & 1), c ^ (mask & 1)
            peer = 4 * px + 2 * py + pc
            remote.append(pltpu.make_async_remote_copy(
                src_ref=in_ref, dst_ref=out_ref.at[me], send_sem=send_sems.at[mask - 1],
                recv_sem=recv_sems.at[mask - 1], device_id=(px, py, pc), device_id_type=MESH))
            landing.append(pltpu.make_async_remote_copy(
                src_ref=in_ref, dst_ref=out_ref.at[peer], send_sem=send_sems.at[mask - 1],
                recv_sem=recv_sems.at[mask - 1], device_id=(px, py, pc), device_id_type=MESH))
        for cp in remote:
            cp.start()
        for cp in landing:
            cp.wait_recv()
        for cp in remote:
            cp.wait_send()
        local.wait()

    return pl.pallas_call(
        body, name="small_grad_gather",
        out_shape=SDS((N_DEV,) + slab.shape, slab.dtype),
        in_specs=_hbm_specs(1), out_specs=pl.BlockSpec(memory_space=pl.ANY),
        scratch_shapes=[pltpu.SemaphoreType.DMA((N_DEV - 1,)), pltpu.SemaphoreType.DMA((N_DEV - 1,)),
                        pltpu.SemaphoreType.DMA],
    )(slab)


def _sum_slots(arrays, splits):
    n = len(arrays)

    def body(*refs):
        for i in range(n):
            acc = refs[i][0]
            for k in range(1, arrays[i].shape[0]):
                acc = acc + refs[i][k]
            refs[n + i][...] = acc

    return pl.pallas_call(
        body, name="sum_slots_%d" % arrays[0].shape[0], grid=(splits,),
        in_specs=[pl.BlockSpec((a.shape[0], a.shape[1] // splits, a.shape[2]), lambda s: (0, s, 0)) for a in arrays],
        out_specs=[pl.BlockSpec((a.shape[1] // splits, a.shape[2]), lambda s: (s, 0)) for a in arrays],
        out_shape=[SDS(a.shape[1:], a.dtype) for a in arrays],
        compiler_params=_params(("parallel",)),
    )(*arrays)


def _adamw_math(w, g, m, v):
    m = ADAM_B1 * m + (1.0 - ADAM_B1) * g
    v = ADAM_B2 * v + (1.0 - ADAM_B2) * (g * g)
    m_hat = m / (1.0 - ADAM_B1 ** ADAM_STEP)
    v_hat = v / (1.0 - ADAM_B2 ** ADAM_STEP)
    delta = -ADAM_LR * (m_hat / (jnp.sqrt(v_hat) + ADAM_EPS) + ADAM_WD * w)
    return delta, m, v


def _adamw(name, ws, gs, ms, vs, splits):
    n = len(ws)

    def body(*refs):
        for i in range(n):
            w, g, m, v = (refs[q * n + i][...] for q in range(4))
            delta, m, v = _adamw_math(w, g, m, v)
            refs[4 * n + i][...] = delta
            refs[5 * n + i][...] = m
            refs[6 * n + i][...] = v

    def spec(a):
        if splits == 1:
            return pl.BlockSpec(a.shape, lambda s: (0, 0))
        return pl.BlockSpec((a.shape[0] // splits, a.shape[1]), lambda s: (s, 0))

    specs = [spec(a) for a in ws]
    outs = pl.pallas_call(
        body, name=name, grid=(splits,),
        in_specs=specs * 4, out_specs=specs * 3,
        out_shape=[SDS(a.shape, F32) for a in ws] * 3,
        compiler_params=_params(("parallel",)),
    )(*ws, *gs, *ms, *vs)
    return outs[:n], outs[n:2 * n], outs[2 * n:]


def _sum_devices(slabs):
    def body(in_ref, out_ref):
        acc = in_ref[0]
        for k in range(1, N_DEV):
            acc = acc + in_ref[k]
        out_ref[...] = acc

    return pl.pallas_call(
        body, name="small_grad_sum", out_shape=SDS(slabs.shape[1:], F32),
    )(slabs)


def _mm_nn(name, a, w, out_dtype, tm, scale_first_tile=None):
    S, K = a.shape
    N = w.shape[1]
    tn = K

    def body(a_ref, w_ref, o_ref):
        acc = _dot(a_ref[...], w_ref[...])
        if scale_first_tile is not None:
            acc = acc * jnp.where(pl.program_id(1) == 0, scale_first_tile, 1.0)
        o_ref[...] = acc.astype(out_dtype)

    return pl.pallas_call(
        body, name=name, grid=(S // tm, N // tn),
        in_specs=[pl.BlockSpec((tm, K), lambda i, t: (i, 0)), pl.BlockSpec((K, tn), lambda i, t: (0, t))],
        out_specs=pl.BlockSpec((tm, tn), lambda i, t: (i, t)),
        out_shape=SDS((S, N), out_dtype),
        compiler_params=_params(("parallel", "arbitrary")),
    )(a, w)


def _mm_tn(name, a, b, tk):
    S, M = a.shape
    N = b.shape[1]
    tn = M

    def body(a_ref, b_ref, o_ref):
        @pl.when(pl.program_id(1) == 0)
        def _():
            o_ref[...] = jnp.zeros_like(o_ref)
        o_ref[...] += _tdot(a_ref[...], b_ref[...])

    return pl.pallas_call(
        body, name=name, grid=(N // tn, S // tk),
        in_specs=[pl.BlockSpec((tk, M), lambda t, k: (k, 0)), pl.BlockSpec((tk, tn), lambda t, k: (k, t))],
        out_specs=pl.BlockSpec((M, tn), lambda t, k: (0, t)),
        out_shape=SDS((M, N), F32),
        compiler_params=_params(("parallel", "arbitrary")),
    )(a, b)


def _a_in_proj(x, w4, b_in, tm):
    S, D = x.shape
    nj = w4.shape[2]

    def body(x_ref, w_ref, b_ref, h_ref, xb_ref):
        xb = x_ref[...].astype(BF16)

        @pl.when(pl.program_id(1) == 0)
        def _():
            xb_ref[...] = xb
        h_ref[...] = _dot(xb, w_ref[...]) + b_ref[...]

    return pl.pallas_call(
        body, name="a_in_proj", grid=(S // tm, N_CHIPS),
        in_specs=[pl.BlockSpec((tm, D), lambda i, t: (i, 0)),
                  pl.BlockSpec((None, D, nj), lambda i, t: (t, 0, 0)),
                  pl.BlockSpec((1, nj), lambda i, t: (0, t))],
        out_specs=[pl.BlockSpec((tm, nj), lambda i, t: (i, t)), pl.BlockSpec((tm, D), lambda i, t: (i, 0))],
        out_shape=[SDS((S, N_CHIPS * nj), F32), SDS((S, D), BF16)],
        compiler_params=_params(("parallel", "arbitrary")),
    )(x, w4, b_in)


def _fill_glu_ext(ext_ref, a_ref, g_ref, ah_ref, gh_ref, has_prev):
    u0h = ah_ref[...] * _sigmoid(gh_ref[...])
    ext_ref[0:HALO, :] = jnp.where(has_prev, u0h, 0.0)
    ext_ref[HALO:, :] = a_ref[...] * _sigmoid(g_ref[...])


def _a_conv_out(h, x, wdw, bdw, lng, lnb, wout, bout, pg, pb, tm):
    S, D = x.shape
    hb = tm // HALO
    d1, d2 = DILATIONS[1], DILATIONS[2]

    def body(a_ref, g_ref, z_ref, ah_ref, gh_ref, x_ref, wdw_ref, bdw_ref, lng_ref, lnb_ref, wout_ref,
             bout_ref, pg_ref, pb_ref, xhu_ref, rsu_ref, vb_ref, xh1_ref, rs1_ref, x1b_ref, x1p1_ref,
             x1p2_ref, ext_ref, u1_ref, x1_ref):
        i = pl.program_id(0)
        _fill_glu_ext(ext_ref, a_ref, g_ref, ah_ref, gh_ref, i > 0)
        off = HALO - (CONV_WIDTH - 1)
        for r0 in range(0, tm, CONV_ROWS):
            acc = jnp.broadcast_to(bdw_ref[...], (CONV_ROWS, D))
            for k in range(CONV_WIDTH):
                acc = acc + wdw_ref[k:k + 1, :] * ext_ref[r0 + off + k:r0 + off + k + CONV_ROWS, :]
            u1_ref[r0:r0 + CONV_ROWS, :] = acc
        xhu, rsu = _ln_fwd(u1_ref[...])
        xhu_ref[...] = xhu
        rsu_ref[...] = rsu
        u2 = xhu * lng_ref[...] + lnb_ref[...]
        z = z_ref[...]
        v = (u2 * _sigmoid(u2)) * (z * _sigmoid(z))
        vb = v.astype(BF16)
        vb_ref[...] = vb
        s1 = ALPHA * x_ref[...] + _dot(vb, wout_ref[...]) + bout_ref[...]
        xh1, rs1 = _ln_fwd(s1)
        xh1_ref[...] = xh1
        rs1_ref[...] = rs1
        x1 = xh1 * pg_ref[...] + pb_ref[...]
        x1b_ref[...] = x1.astype(BF16)
        _to_chunks(x1_ref, x1)
        _deinterleave(x1_ref, x1p1_ref, d1, BF16)
        _deinterleave(x1_ref, x1p2_ref, d2, BF16)

    tile = lambda c: pl.BlockSpec((tm, D), lambda i, c=c: (i, c))
    halo = lambda c: pl.BlockSpec((HALO, D), lambda i, c=c: (jnp.maximum(i * hb - 1, 0), c))
    row = pl.BlockSpec((1, D), lambda i: (0, 0))
    stat = pl.BlockSpec((tm, 1), lambda i: (i, 0))
    return pl.pallas_call(
        body, name="a_conv_out", grid=(S // tm,),
        in_specs=[tile(0), tile(1), tile(2), halo(0), halo(1), tile(0),
                  pl.BlockSpec((HALO, D), lambda i: (0, 0)), row, row, row,
                  pl.BlockSpec((D, D), lambda i: (0, 0)), row, row, row],
        out_specs=[tile(0), stat, tile(0), tile(0), stat, tile(0),
                   pl.BlockSpec((d1, tm // d1, D), lambda i: (0, i, 0)),
                   pl.BlockSpec((d2, tm // d2, D), lambda i: (0, i, 0))],
        out_shape=[SDS((S, D), F32), SDS((S, 1), F32), SDS((S, D), BF16), SDS((S, D), F32), SDS((S, 1), F32),
                   SDS((S, D), BF16), SDS((d1, S // d1, D), BF16), SDS((d2, S // d2, D), BF16)],
        scratch_shapes=[pltpu.VMEM((HALO + tm, D), F32), pltpu.VMEM((tm, D), F32),
                        pltpu.VMEM((D // 128, tm, 128), F32)],
        compiler_params=_params(("parallel",)),
    )(h, h, h, h, h, x, wdw, bdw, lng, lnb, wout, bout, pg, pb)


def _band(n, dilation):
    qi = lax.broadcasted_iota(jnp.int32, (BLOCK, 2 * BLOCK), 0)
    kj = lax.broadcasted_iota(jnp.int32, (BLOCK, 2 * BLOCK), 1)
    dist = qi + BLOCK - kj
    valid = (dist >= 0) & (dist <= BLOCK) & ((n > 0) | (kj >= BLOCK))
    return valid, dist.astype(F32) * float(dilation)


def _attn_fwd(g, qkv, D):
    S = qkv.shape[0]
    d = DILATIONS[g]
    nb = S // (d * BLOCK)
    H = D // HEAD_DIM
    slopes = _slopes(H)

    def body(q_ref, kp_ref, kc_ref, vp_ref, vc_ref, o_ref, lse_ref):
        valid, distf = _band(pl.program_id(1), d)
        lane = lax.broadcasted_iota(jnp.int32, (BLOCK, LSE_LANES), 1)
        lse = jnp.zeros((BLOCK, LSE_LANES), F32)
        for h in range(H):
            sl = slice(h * HEAD_DIM, (h + 1) * HEAD_DIM)
            k = jnp.concatenate([kp_ref[:, sl], kc_ref[:, sl]], axis=0)
            v = jnp.concatenate([vp_ref[:, sl], vc_ref[:, sl]], axis=0)
            s = _dot_t(q_ref[:, sl], k)
            s = jnp.where(valid, s - slopes[h] * distf, -jnp.inf)
            m = jnp.max(s, axis=1, keepdims=True)
            p = jnp.exp(s - m)
            l = jnp.sum(p, axis=1, keepdims=True)
            o_ref[:, sl] = _dot(p.astype(BF16), v) / l
            lse = jnp.where(lane == h, m + jnp.log(l), lse)
        lse_ref[...] = lse

    cur = lambda c: pl.BlockSpec((BLOCK, D), lambda r, n, c=c: (r * nb + n, c))
    prev = lambda c: pl.BlockSpec((BLOCK, D), lambda r, n, c=c: (r * nb + jnp.maximum(n - 1, 0), c))
    return pl.pallas_call(
        body, name="attn_fwd_g%d" % g, grid=(d, nb),
        in_specs=[cur(0), prev(1), cur(1), prev(2), cur(2)],
        out_specs=[cur(0), pl.BlockSpec((BLOCK, LSE_LANES), lambda r, n: (r * nb + n, 0))],
        out_shape=[SDS((S, D), F32), SDS((S, LSE_LANES), F32)],
        compiler_params=_params(("parallel", "parallel")),
    )(qkv, qkv, qkv, qkv, qkv)


def _attn_bwd(g, qkv, do, lse, delta, D):
    S = qkv.shape[0]
    d = DILATIONS[g]
    nb = S // (d * BLOCK)
    H = D // HEAD_DIM
    slopes = _slopes(H)

    def body(q_ref, kp_ref, kc_ref, vp_ref, vc_ref, do_ref, lse_ref, dl_ref, dq_ref, dkv_ref, ck_ref, cv_ref):
        n = pl.program_id(1)

        @pl.when(n < nb)
        def _():
            valid, distf = _band(n, d)
            for h in range(H):
                sl = slice(h * HEAD_DIM, (h + 1) * HEAD_DIM)
                q = q_ref[:, sl]
                do_h = do_ref[:, sl]
                k = jnp.concatenate([kp_ref[:, sl], kc_ref[:, sl]], axis=0)
                v = jnp.concatenate([vp_ref[:, sl], vc_ref[:, sl]], axis=0)
                s = _dot_t(q, k)
                s = jnp.where(valid, s - slopes[h] * distf, -jnp.inf)
                p = jnp.exp(s - lse_ref[:, h:h + 1])
                ds = p * (_dot_t(do_h, v) - dl_ref[:, h:h + 1])
                dsb = ds.astype(BF16)
                dq_ref[:, sl] = (_dot(dsb, k) * (HEAD_DIM ** -0.5)).astype(BF16)
                dk = _tdot(dsb, q)
                dv = _tdot(p.astype(BF16), do_h)

                @pl.when(n > 0)
                def _():
                    dkv_ref[:, sl] = (ck_ref[:, sl] + dk[:BLOCK]).astype(BF16)
                    dkv_ref[:, D + h * HEAD_DIM:D + (h + 1) * HEAD_DIM] = (cv_ref[:, sl] + dv[:BLOCK]).astype(BF16)
                ck_ref[:, sl] = dk[BLOCK:]
                cv_ref[:, sl] = dv[BLOCK:]

        @pl.when(n == nb)
        def _():
            dkv_ref[:, :D] = ck_ref[...].astype(BF16)
            dkv_ref[:, D:] = cv_ref[...].astype(BF16)

    nq = lambda n: jnp.minimum(n, nb - 1)
    cur = lambda c: pl.BlockSpec((BLOCK, D), lambda r, n, c=c: (r * nb + nq(n), c))
    prev = lambda c: pl.BlockSpec((BLOCK, D), lambda r, n, c=c: (r * nb + jnp.maximum(nq(n) - 1, 0), c))
    stat = pl.BlockSpec((BLOCK, LSE_LANES), lambda r, n: (r * nb + nq(n), 0))
    return pl.pallas_call(
        body, name="attn_bwd_g%d" % g, grid=(d, nb + 1),
        in_specs=[cur(0), prev(1), cur(1), prev(2), cur(2), cur(0), stat, stat],
        out_specs=[cur(0), pl.BlockSpec((BLOCK, 2 * D), lambda r, n: (r * nb + jnp.maximum(n - 1, 0), 0))],
        out_shape=[SDS((S, D), BF16), SDS((S, 2 * D), BF16)],
        scratch_shapes=[pltpu.VMEM((BLOCK, D), F32), pltpu.VMEM((BLOCK, D), F32)],
        compiler_params=_params(("parallel", "arbitrary")),
    )(qkv, qkv, qkv, qkv, qkv, do, lse, delta)


def _b_merge_out_loss(o0, o1, o2, l0, l1, l2, z2, xh1, target, wbo, bbo, pg0, pb0, pg1, pb1, tm):
    S, D = o0.shape
    H = D // HEAD_DIM
    d1, d2 = DILATIONS[1], DILATIONS[2]
    inv_d = 1.0 / D

    def body(o0_ref, o1_ref, o2_ref, l0_ref, l1_ref, l2_ref, z_ref, xh1_ref, t_ref, wbo_ref, bbo_ref,
             pg0_ref, pb0_ref, pg1_ref, pb1_ref,
             v2b_ref, ds2_ref, ds2b_ref, dz2b_ref, da0_ref, da1_ref, da2_ref, ls0_ref, ls1_ref, ls2_ref,
             dl0_ref, dl1_ref, dl2_ref, loss_ref, sums_ref,
             o1n_ref, o2n_ref, l1n_ref, l2n_ref, att_ref, st_ref):
        i = pl.program_id(0)
        _interleave(o1_ref, o1n_ref, d1)
        _interleave(o2_ref, o2n_ref, d2)
        for r in range(d1):
            l1n_ref[pl.ds(r, tm // d1, stride=d1), :] = l1_ref[r]
        for r in range(d2):
            l2n_ref[pl.ds(r, tm // d2, stride=d2), :] = l2_ref[r]
        la, lb, lc = l0_ref[...], l1n_ref[...], l2n_ref[...]
        m = jnp.maximum(jnp.maximum(la, lb), lc)
        ea, eb, ec = jnp.exp(la - m), jnp.exp(lb - m), jnp.exp(lc - m)
        den = ea + eb + ec
        wa, wb, wc = ea / den, eb / den, ec / den
        lse = m + jnp.log(den)
        for h in range(H):
            sl = slice(h * HEAD_DIM, (h + 1) * HEAD_DIM)
            cc, hl = divmod(h * HEAD_DIM, 128)
            att_ref[:, sl] = (wa[:, h:h + 1] * o0_ref[:, sl] + wb[:, h:h + 1] * o1n_ref[cc, :, hl:hl + HEAD_DIM]
                              + wc[:, h:h + 1] * o2n_ref[cc, :, hl:hl + HEAD_DIM])
        att = att_ref[...]
        z = z_ref[...]
        sz = _sigmoid(z)
        gate = z * sz
        v2b = (att * gate).astype(BF16)
        v2b_ref[...] = v2b
        x1 = xh1_ref[...] * pg0_ref[...] + pb0_ref[...]
        s2 = ALPHA * x1 + _dot(v2b, wbo_ref[...]) + bbo_ref[...]
        xh2, rs2 = _ln_fwd(s2)
        err = xh2 * pg1_ref[...] + pb1_ref[...] - t_ref[...]
        dy = err * inv_d
        ds2 = _ln_bwd(dy * pg1_ref[...], xh2, rs2)
        ds2b = ds2.astype(BF16)
        ds2_ref[...] = ds2
        ds2b_ref[...] = ds2b

        @pl.when(i == 0)
        def _():
            loss_ref[...] = jnp.zeros_like(loss_ref)
            sums_ref[...] = jnp.zeros_like(sums_ref)
        loss_ref[...] += 0.5 * inv_d * jnp.sum(err * err)
        sums_ref[0:1, :] += _colsum(dy * xh2)
        sums_ref[1:2, :] += _colsum(dy)
        sums_ref[2:3, :] += _colsum(ds2)

        dv2 = _dot_t(ds2b, wbo_ref[...])
        datt = dv2 * gate
        dz2b_ref[...] = (dv2 * att * _silu_grad(z, sz)).astype(BF16)
        prod = datt * att
        lane = lax.broadcasted_iota(jnp.int32, (tm, LSE_LANES), 1)
        dl = jnp.zeros((tm, LSE_LANES), F32)
        for h in range(H):
            sl = slice(h * HEAD_DIM, (h + 1) * HEAD_DIM)
            dl = jnp.where(lane == h, jnp.sum(prod[:, sl], axis=1, keepdims=True), dl)
        da0_ref[...] = datt.astype(BF16)
        ls0_ref[...] = lse
        dl0_ref[...] = dl
        _to_chunks(o1n_ref, datt)
        _deinterleave(o1n_ref, da1_ref, d1, BF16)
        _deinterleave(o1n_ref, da2_ref, d2, BF16)
        st_ref[...] = lse
        for r in range(d1):
            ls1_ref[r] = st_ref[pl.ds(r, tm // d1, stride=d1), :]
        for r in range(d2):
            ls2_ref[r] = st_ref[pl.ds(r, tm // d2, stride=d2), :]
        st_ref[...] = dl
        for r in range(d1):
            dl1_ref[r] = st_ref[pl.ds(r, tm // d1, stride=d1), :]
        for r in range(d2):
            dl2_ref[r] = st_ref[pl.ds(r, tm // d2, stride=d2), :]

    tile = pl.BlockSpec((tm, D), lambda i: (i, 0))
    stat = pl.BlockSpec((tm, LSE_LANES), lambda i: (i, 0))
    perm = lambda d, w: pl.BlockSpec((d, tm // d, w), lambda i: (0, i, 0))
    row = pl.BlockSpec((1, D), lambda i: (0, 0))
    acc = lambda w: pl.BlockSpec((8, w), lambda i: (0, 0))
    pshape = lambda d, w, dt: SDS((d, S // d, w), dt)
    return pl.pallas_call(
        body, name="b_merge_out_loss", grid=(S // tm,),
        in_specs=[tile, perm(d1, D), perm(d2, D), stat, perm(d1, LSE_LANES), perm(d2, LSE_LANES),
                  tile, tile, tile, pl.BlockSpec((D, D), lambda i: (0, 0)), row, row, row, row, row],
        out_specs=[tile, tile, tile, tile, tile, perm(d1, D), perm(d2, D),
                   stat, perm(d1, LSE_LANES), perm(d2, LSE_LANES),
                   stat, perm(d1, LSE_LANES), perm(d2, LSE_LANES), acc(LSE_LANES), acc(D)],
        out_shape=[SDS((S, D), BF16), SDS((S, D), F32), SDS((S, D), BF16), SDS((S, D), BF16),
                   SDS((S, D), BF16), pshape(d1, D, BF16), pshape(d2, D, BF16),
                   SDS((S, LSE_LANES), F32), pshape(d1, LSE_LANES, F32), pshape(d2, LSE_LANES, F32),
                   SDS((S, LSE_LANES), F32), pshape(d1, LSE_LANES, F32), pshape(d2, LSE_LANES, F32),
                   SDS((8, LSE_LANES), F32), SDS((8, D), F32)],
        scratch_shapes=[pltpu.VMEM((D // 128, tm, 128), F32), pltpu.VMEM((D // 128, tm, 128), F32),
                        pltpu.VMEM((tm, LSE_LANES), F32), pltpu.VMEM((tm, LSE_LANES), F32),
                        pltpu.VMEM((tm, D), F32), pltpu.VMEM((tm, LSE_LANES), F32)],
        compiler_params=_params(("arbitrary",)),
    )(o0, o1, o2, l0, l1, l2, z2, xh1, target, wbo, bbo, pg0, pb0, pg1, pb1)


def _b_dx1_ln1_bwd(ds2, dz2b, dq, dkv, xh1, rs1, wz, wg, pg0, tm):
    S, D = ds2.shape
    d1, d2 = DILATIONS[1], DILATIONS[2]

    def group_part(dq_blk, dkv_blk, w_ref):
        return (_dot_t(dq_blk, w_ref[:, 0:D]) + _dot_t(dkv_blk[:, 0:D], w_ref[:, D:2 * D])
                + _dot_t(dkv_blk[:, D:2 * D], w_ref[:, 2 * D:3 * D]))

    def body(ds2_ref, dz_ref, dq0_ref, dkv0_ref, dq1_ref, dkv1_ref, dq2_ref, dkv2_ref, xh1_ref, rs1_ref,
             wz_ref, w0_ref, w1_ref, w2_ref, pg0_ref, ds1_ref, ds1b_ref, sums_ref, acc_ref):
        i = pl.program_id(0)
        _to_chunks(acc_ref, ALPHA * ds2_ref[...] + _dot_t(dz_ref[...], wz_ref[...])
                   + group_part(dq0_ref[...], dkv0_ref[...], w0_ref))
        for d, dq_ref, dkv_ref, w_ref in ((d1, dq1_ref, dkv1_ref, w1_ref), (d2, dq2_ref, dkv2_ref, w2_ref)):
            rows = tm // d
            part = group_part(dq_ref[...].reshape(tm, D), dkv_ref[...].reshape(tm, 2 * D), w_ref)
            for r in range(d):
                idx = pl.ds(r, rows, stride=d)
                for cc in range(D // 128):
                    acc_ref[cc, idx, :] = acc_ref[cc, idx, :] + part[r * rows:(r + 1) * rows, cc * 128:(cc + 1) * 128]
        dx1 = _from_chunks(acc_ref)
        xh1 = xh1_ref[...]
        ds1 = _ln_bwd(dx1 * pg0_ref[...], xh1, rs1_ref[...])
        ds1_ref[...] = ds1
        ds1b_ref[...] = ds1.astype(BF16)

        @pl.when(i == 0)
        def _():
            sums_ref[...] = jnp.zeros_like(sums_ref)
        sums_ref[0:1, :] += _colsum(dx1 * xh1)
        sums_ref[1:2, :] += _colsum(dx1)
        sums_ref[2:3, :] += _colsum(ds1)

    tile = lambda w: pl.BlockSpec((tm, w), lambda i: (i, 0))
    perm = lambda d, w: pl.BlockSpec((d, tm // d, w), lambda i: (0, i, 0))
    whole = pl.BlockSpec(memory_space=pltpu.VMEM)
    return pl.pallas_call(
        body, name="b_dx1_ln1_bwd", grid=(S // tm,),
        in_specs=[tile(D), tile(D), tile(D), tile(2 * D), perm(d1, D), perm(d1, 2 * D), perm(d2, D),
                  perm(d2, 2 * D), tile(D), tile(1), whole, whole, whole, whole,
                  pl.BlockSpec((1, D), lambda i: (0, 0))],
        out_specs=[tile(D), tile(D), pl.BlockSpec((8, D), lambda i: (0, 0))],
        out_shape=[SDS((S, D), F32), SDS((S, D), BF16), SDS((8, D), F32)],
        scratch_shapes=[pltpu.VMEM((D // 128, tm, 128), F32)],
        compiler_params=_params(("arbitrary",)),
    )(ds2, dz2b, dq[0], dkv[0], dq[1].reshape(d1, S // d1, D), dkv[1].reshape(d1, S // d1, 2 * D),
      dq[2].reshape(d2, S // d2, D), dkv[2].reshape(d2, S // d2, 2 * D), xh1, rs1, wz, wg[0], wg[1], wg[2], pg0)


def _a_gate_bwd(ds1b, h, xhu, rsu, wout, lng, lnb, tm):
    S, D = xhu.shape

    def body(ds_ref, z_ref, xhu_ref, rsu_ref, w_ref, lng_ref, lnb_ref, du1_ref, dzb_ref, sums_ref):
        i = pl.program_id(0)
        dv = _dot_t(ds_ref[...], w_ref[...])
        xhu = xhu_ref[...]
        u2 = xhu * lng_ref[...] + lnb_ref[...]
        su = _sigmoid(u2)
        z = z_ref[...]
        sz = _sigmoid(z)
        dz = dv * (u2 * su) * _silu_grad(z, sz)
        du2 = dv * (z * sz) * _silu_grad(u2, su)
        du1 = _ln_bwd(du2 * lng_ref[...], xhu, rsu_ref[...])
        du1_ref[...] = du1
        dzb_ref[...] = dz.astype(BF16)

        @pl.when(i == 0)
        def _():
            sums_ref[...] = jnp.zeros_like(sums_ref)
        sums_ref[0:1, :] += _colsum(du2 * xhu)
        sums_ref[1:2, :] += _colsum(du2)
        sums_ref[2:3, :] += _colsum(du1)
        sums_ref[3:4, :] += _colsum(dz)

    tile = pl.BlockSpec((tm, D), lambda i: (i, 0))
    row = pl.BlockSpec((1, D), lambda i: (0, 0))
    return pl.pallas_call(
        body, name="a_gate_bwd", grid=(S // tm,),
        in_specs=[tile, pl.BlockSpec((tm, D), lambda i: (i, 2)), tile, pl.BlockSpec((tm, 1), lambda i: (i, 0)),
                  pl.BlockSpec((D, D), lambda i: (0, 0)), row, row],
        out_specs=[tile, tile, pl.BlockSpec((8, D), lambda i: (0, 0))],
        out_shape=[SDS((S, D), F32), SDS((S, D), BF16), SDS((8, D), F32)],
        compiler_params=_params(("arbitrary",)),
    )(ds1b, h, xhu, rsu, wout, lng, lnb)


def _a_conv_bwd(du1, h, wdw, tm):
    S, D = du1.shape
    hb = tm // HALO
    last_halo = S // HALO - 1
    n_tiles = S // tm

    def body(du_ref, dun_ref, a_ref, g_ref, ah_ref, gh_ref, wdw_ref, dag_ref, sums_ref, wsum_ref,
             dext_ref, ext_ref):
        i = pl.program_id(0)

        @pl.when(i == 0)
        def _():
            sums_ref[...] = jnp.zeros_like(sums_ref)
            wsum_ref[...] = jnp.zeros_like(wsum_ref)
        dext_ref[0:tm, :] = du_ref[...]
        dext_ref[tm:, :] = jnp.where(i < n_tiles - 1, dun_ref[...], 0.0)
        _fill_glu_ext(ext_ref, a_ref, g_ref, ah_ref, gh_ref, i > 0)
        sa = jnp.zeros((1, D), F32)
        sg = jnp.zeros((1, D), F32)
        for r0 in range(0, tm, CONV_ROWS):
            acc = jnp.zeros((CONV_ROWS, D), F32)
            for k in range(CONV_WIDTH):
                o = r0 + CONV_WIDTH - 1 - k
                acc = acc + wdw_ref[k:k + 1, :] * dext_ref[o:o + CONV_ROWS, :]
            a = a_ref[r0:r0 + CONV_ROWS, :]
            s = _sigmoid(g_ref[r0:r0 + CONV_ROWS, :])
            da = acc * s
            dg = acc * a * s * (1.0 - s)
            dag_ref[r0:r0 + CONV_ROWS, 0:D] = da.astype(BF16)
            dag_ref[r0:r0 + CONV_ROWS, D:2 * D] = dg.astype(BF16)
            sa = sa + _colsum(da)
            sg = sg + _colsum(dg)
        sums_ref[0:1, :] += sa
        sums_ref[1:2, :] += sg
        off = HALO - (CONV_WIDTH - 1)
        for k in range(CONV_WIDTH):
            acc = jnp.zeros((8, D), F32)
            for r0 in range(0, tm, CONV_ROWS):
                p = du_ref[r0:r0 + CONV_ROWS, :] * ext_ref[r0 + off + k:r0 + off + k + CONV_ROWS, :]
                for q in range(0, CONV_ROWS, 8):
                    acc = acc + p[q:q + 8]
            wsum_ref[k:k + 1, :] += _colsum(acc)

    tile = lambda c: pl.BlockSpec((tm, D), lambda i, c=c: (i, c))
    halo = lambda c: pl.BlockSpec((HALO, D), lambda i, c=c: (jnp.maximum(i * hb - 1, 0), c))
    return pl.pallas_call(
        body, name="a_conv_bwd", grid=(n_tiles,),
        in_specs=[tile(0), pl.BlockSpec((HALO, D), lambda i: (jnp.minimum((i + 1) * hb, last_halo), 0)),
                  tile(0), tile(1), halo(0), halo(1), pl.BlockSpec((HALO, D), lambda i: (0, 0))],
        out_specs=[pl.BlockSpec((tm, 2 * D), lambda i: (i, 0)), pl.BlockSpec((8, D), lambda i: (0, 0)),
                   pl.BlockSpec((HALO, D), lambda i: (0, 0))],
        out_shape=[SDS((S, 2 * D), BF16), SDS((8, D), F32), SDS((HALO, D), F32)],
        scratch_shapes=[pltpu.VMEM((tm + HALO, D), F32), pltpu.VMEM((HALO + tm, D), F32)],
        compiler_params=_params(("arbitrary",)),
    )(du1, du1, h, h, h, h, wdw)


def _a_dx(ds1, dag, dzb, w_in, tm):
    S, D = ds1.shape

    def body(ds_ref, dag_ref, dz_ref, w_ref, o_ref):
        o_ref[...] = (ALPHA * ds_ref[...] + _dot_t(dag_ref[...], w_ref[:, 0:2 * D])
                      + _dot_t(dz_ref[...], w_ref[:, 2 * D:3 * D]))

    tile = lambda w: pl.BlockSpec((tm, w), lambda i: (i, 0))
    return pl.pallas_call(
        body, name="a_dx", grid=(S // tm,),
        in_specs=[tile(D), tile(2 * D), tile(D), pl.BlockSpec(memory_space=pltpu.VMEM)],
        out_specs=tile(D), out_shape=SDS((S, D), F32),
        compiler_params=_params(("parallel",)),
    )(ds1, dag, dzb, w_in)


def _halves(w):
    return w.reshape(2, w.shape[0] // 2, w.shape[1])


def _unstack_cols(w4):
    return jnp.transpose(w4, (1, 0, 2)).reshape(w4.shape[1], N_CHIPS * w4.shape[2])


def _stack_cols(w):
    D, n = w.shape
    return jnp.transpose(w.reshape(D, N_CHIPS, n // N_CHIPS), (1, 0, 2))


def _pack_rows(rows, width):
    slab = jnp.concatenate([r.reshape(-1, width) for r in rows], axis=0)
    return jnp.pad(slab, ((0, SMALL_ROWS - slab.shape[0]), (0, 0)))


def kernel(x, a_w_in, a_b_in, a_w_dw, a_b_dw, a_ln_g, a_ln_b, a_w_out, a_b_out, kv_w, b_w_in, b_w_out, b_b_out, post_ln_g, post_ln_b, loss_target, m_a_w_in, m_a_b_in, m_a_w_dw, m_a_b_dw, m_a_ln_g, m_a_ln_b, m_a_w_out, m_a_b_out, m_kv_w, m_b_w_in, m_b_w_out, m_b_b_out, m_post_ln_g, m_post_ln_b, v_a_w_in, v_a_b_in, v_a_w_dw, v_a_b_dw, v_a_ln_g, v_a_ln_b, v_a_w_out, v_a_b_out, v_kv_w, v_b_w_in, v_b_w_out, v_b_b_out, v_post_ln_g, v_post_ln_b):
    S, D = x.shape[1], x.shape[2]
    dq4 = D // N_CHIPS
    tm = 256
    tm_mm = 512
    x2 = x.reshape(S, D)
    target = loss_target.reshape(S, D)
    jchip = 2 * lax.axis_index("x") + lax.axis_index("y")

    big_local = [a_w_in[0], kv_w, b_w_in[0], a_w_out[0], b_w_out[0]]
    small_local = _pack_rows([a_b_in.reshape(3, dq4), jnp.pad(a_w_dw[0], ((0, 1), (0, 0))), a_b_dw, a_ln_g,
                              a_ln_b, a_b_out], dq4)
    gathered = _all_gather_chips([_halves(w.astype(BF16)) for w in big_local] + [_halves(small_local)])
    w_in4, kv4, b_in4, w_out_a, w_out_b = [g.reshape((N_CHIPS, 2 * g.shape[2], g.shape[3])) for g in gathered[:5]]
    w_in_a = _unstack_cols(w_in4)
    kv_full = _unstack_cols(kv4)
    w_out_a = w_out_a.reshape(D, D)
    w_out_b = w_out_b.reshape(D, D)
    w_z = b_in4[3]
    w_g = [jnp.concatenate([b_in4[g], kv_full[:, g * D:(g + 1) * D], kv_full[:, (3 + g) * D:(4 + g) * D]], axis=1)
           for g in range(3)]
    small = jnp.transpose(gathered[5].reshape(N_CHIPS, SMALL_ROWS, dq4), (1, 0, 2))
    b_in_full = jnp.transpose(small[0:3], (1, 0, 2)).reshape(1, 3 * D)
    wdw_full = small[3:3 + HALO].reshape(HALO, D)
    bdw_full, lng_full, lnb_full, bout_a_full = [small[35 + q].reshape(1, D) for q in range(4)]
    pg0, pg1 = post_ln_g[0:1], post_ln_g[1:2]
    pb0, pb1 = post_ln_b[0:1], post_ln_b[1:2]

    h, xb = _a_in_proj(x2, w_in4, b_in_full, tm_mm)
    xhu, rsu, vb, xh1, rs1, x1b, x1p1, x1p2 = _a_conv_out(
        h, x2, wdw_full, bdw_full, lng_full, lnb_full, w_out_a, bout_a_full, pg0, pb0, tm)
    x1g = [x1b, x1p1.reshape(S, D), x1p2.reshape(S, D)]
    qkv = [_mm_nn("b_qkv_g%d" % g, x1g[g], w_g[g], BF16, tm_mm, scale_first_tile=HEAD_DIM ** -0.5)
           for g in range(3)]
    z2 = _mm_nn("b_gate_proj", x1b, w_z, F32, tm_mm)
    og, lg = zip(*[_attn_fwd(g, qkv[g], D) for g in range(3)])
    d1, d2 = DILATIONS[1], DILATIONS[2]
    (v2b, ds2, ds2b, dz2b, da0, da1, da2, ls0, ls1, ls2, dl0, dl1, dl2, loss_acc, sums_b) = _b_merge_out_loss(
        og[0], og[1].reshape(d1, S // d1, D), og[2].reshape(d2, S // d2, D),
        lg[0], lg[1].reshape(d1, S // d1, LSE_LANES), lg[2].reshape(d2, S // d2, LSE_LANES),
        z2, xh1, target, w_out_b, b_b_out, pg0, pb0, pg1, pb1, tm)

    das = [da0, da1.reshape(S, D), da2.reshape(S, D)]
    lss = [ls0, ls1.reshape(S, LSE_LANES), ls2.reshape(S, LSE_LANES)]
    dls = [dl0, dl1.reshape(S, LSE_LANES), dl2.reshape(S, LSE_LANES)]
    dq, dkv = zip(*[_attn_bwd(g, qkv[g], das[g], lss[g], dls[g], D) for g in range(3)])
    ds1, ds1b, sums_1 = _b_dx1_ln1_bwd(ds2, dz2b, dq, dkv, xh1, rs1, w_z, w_g, pg0, tm)
    du1, dzab, sums_a = _a_gate_bwd(ds1b, h, xhu, rsu, w_out_a, lng_full, lnb_full, tm)
    dag, sums_c, wsum = _a_conv_bwd(du1, h, wdw_full, tm)
    grad_x = _a_dx(ds1, dag, dzab, w_in_a, tm)

    g_w_in = jnp.concatenate([_mm_tn("dw_a_in_ag", xb, dag, tm_mm), _mm_tn("dw_a_in_z", xb, dzab, tm_mm)], axis=1)
    g_w_out_a = _mm_tn("dw_a_out", vb, ds1b, tm_mm)
    g_w_out_b = _mm_tn("dw_b_out", v2b, ds2b, tm_mm)
    g_q = [_mm_tn("dw_b_q_g%d" % g, x1g[g], dq[g], tm_mm) for g in range(3)]
    g_z = _mm_tn("dw_b_z", x1b, dz2b, tm_mm)
    g_kvg = [_mm_tn("dw_kv_g%d" % g, x1g[g], dkv[g], tm_mm) for g in range(3)]
    g_kv = jnp.concatenate([t[:, :D] for t in g_kvg] + [t[:, D:] for t in g_kvg], axis=1)

    def by_chip_cols(gw):
        s4 = _stack_cols(gw)
        return s4.reshape(N_CHIPS, 2, D // 2, s4.shape[2])

    def by_chip_rows(gw):
        return gw.reshape(N_CHIPS, 2, D // 8, D)

    big_grads = [by_chip_cols(g_w_in), by_chip_cols(g_kv), jnp.stack(g_q + [g_z]).reshape(N_CHIPS, 2, D // 2, D),
                 by_chip_rows(g_w_out_a), by_chip_rows(g_w_out_b)]
    pairs = _pair_exchange(big_grads)
    parts = _sum_slots([p.reshape(2, N_CHIPS * p.shape[2], p.shape[3]) for p in pairs], 16)
    landed = _chip_scatter([p.reshape(N_CHIPS, p.shape[0] // N_CHIPS, p.shape[1]) for p in parts])
    halves = _sum_slots(landed, 8)
    shards = [s.reshape(2 * s.shape[1], s.shape[2]) for s in _pair_share(halves)]

    small_grads = _pack_rows([sums_c[0:1], sums_c[1:2], sums_a[3:4], wsum, sums_a[2:3], sums_a[0:1], sums_a[1:2],
                              sums_1[2:3], sums_b[2:3], sums_1[0:1], sums_b[0:1], sums_1[1:2], sums_b[1:2]], D)
    small_sum = _sum_devices(_gather_all_devices(small_grads))
    loss = lax.psum(loss_acc[0, 0], ("x", "y", "c"))

    big_m = [m_a_w_in[0], m_kv_w, m_b_w_in[0], m_a_w_out[0], m_b_w_out[0]]
    big_v = [v_a_w_in[0], v_kv_w, v_b_w_in[0], v_a_w_out[0], v_b_w_out[0]]
    big_delta, big_new_m, big_new_v = _adamw("adamw_big", big_local, shards, big_m, big_v, 8)

    def chip_cols(rows):
        return lax.dynamic_slice_in_dim(rows, jchip * dq4, dq4, axis=1)

    g_b_in = lax.dynamic_slice_in_dim(small_sum[0:3].reshape(1, 3 * D), jchip * 3 * dq4, 3 * dq4, axis=1)
    small_g = [g_b_in, chip_cols(small_sum[3:3 + CONV_WIDTH]), chip_cols(small_sum[35:36]), chip_cols(small_sum[36:37]),
               chip_cols(small_sum[37:38]), chip_cols(small_sum[38:39]), small_sum[39:40], small_sum[40:42],
               small_sum[42:44]]
    small_w = [a_b_in, a_w_dw[0], a_b_dw, a_ln_g, a_ln_b, a_b_out, b_b_out, post_ln_g, post_ln_b]
    small_m = [m_a_b_in, m_a_w_dw[0], m_a_b_dw, m_a_ln_g, m_a_ln_b, m_a_b_out, m_b_b_out, m_post_ln_g, m_post_ln_b]
    small_v = [v_a_b_in, v_a_w_dw[0], v_a_b_dw, v_a_ln_g, v_a_ln_b, v_a_b_out, v_b_b_out, v_post_ln_g, v_post_ln_b]
    small_delta, small_new_m, small_new_v = _adamw("adamw_small", small_w, small_g, small_m, small_v, 1)

    def ordered(big, sm):
        return (big[0][None], sm[0], sm[1][None], sm[2], sm[3], sm[4], big[3][None], sm[5], big[1], big[2][None],
                big[4][None], sm[6], sm[7], sm[8])

    return (loss, grad_x.reshape(1, S, D), *ordered(shards, small_g), *ordered(big_delta, small_delta),
            *ordered(big_new_m, small_new_m), *ordered(big_new_v, small_new_v))
```

```python
import functools

import numpy as np
import jax
import jax.numpy as jnp
from jax import lax
from jax.experimental import pallas as pl
from jax.experimental.pallas import tpu as pltpu

F32 = jnp.float32
BF16 = jnp.bfloat16
MESH = pl.DeviceIdType.MESH
SDS = jax.ShapeDtypeStruct

HEAD_DIM = 64
BLOCK = 128
DILATIONS = (1, 4, 16)
ALIBI_MAX_EXP = 8.0
CONV_WIDTH = 31
HALO = 32
CONV_ROWS = 32
LSE_LANES = 128
DEPTH = 2
ALPHA = (2.0 * DEPTH) ** 0.25
LN_EPS = 1e-5
ADAM_LR = 0.001
ADAM_B1 = 0.9
ADAM_B2 = 0.999
ADAM_EPS = 1e-08
ADAM_WD = 0.01
ADAM_STEP = 10
N_CHIPS = 4
N_DEV = 8
VMEM_LIMIT = 56 * 2 ** 20
SMALL_ROWS = 48


def _params(sem=None):
    return pltpu.CompilerParams(dimension_semantics=sem, vmem_limit_bytes=VMEM_LIMIT)


def _sigmoid(x):
    return 1.0 / (1.0 + jnp.exp(-x))


def _silu_grad(x, s):
    return s * (1.0 + x * (1.0 - s))


def _ln_fwd(x):
    mu = jnp.mean(x, axis=-1, keepdims=True)
    xc = x - mu
    var = jnp.mean(xc * xc, axis=-1, keepdims=True)
    rstd = lax.rsqrt(var + LN_EPS)
    return xc * rstd, rstd


def _ln_bwd(dxhat, xhat, rstd):
    m1 = jnp.mean(dxhat, axis=-1, keepdims=True)
    m2 = jnp.mean(dxhat * xhat, axis=-1, keepdims=True)
    return rstd * (dxhat - m1 - xhat * m2)


def _dot(a, b):
    return jnp.dot(a, b, preferred_element_type=F32)


def _dot_t(a, b):
    return lax.dot_general(a, b, (((1,), (1,)), ((), ())), preferred_element_type=F32)


def _tdot(a, b):
    return lax.dot_general(a, b, (((0,), (0,)), ((), ())), preferred_element_type=F32)


def _colsum(x):
    return jnp.sum(x, axis=0, keepdims=True)


def _slopes(n_heads):
    return [float(np.float32(2.0 ** (-ALIBI_MAX_EXP * (h + 1) / n_heads))) for h in range(n_heads)]


def _to_chunks(chunks_ref, x):
    for cc in range(chunks_ref.shape[0]):
        chunks_ref[cc] = x[:, cc * 128:(cc + 1) * 128]


def _from_chunks(chunks_ref):
    return jnp.concatenate([chunks_ref[cc] for cc in range(chunks_ref.shape[0])], axis=1)


def _deinterleave(chunks_ref, out_ref, d, dtype):
    rows = chunks_ref.shape[1] // d
    for r in range(d):
        for cc in range(chunks_ref.shape[0]):
            out_ref[r, :, cc * 128:(cc + 1) * 128] = chunks_ref[cc, pl.ds(r, rows, stride=d), :].astype(dtype)


def _interleave(in_ref, chunks_ref, d):
    rows = chunks_ref.shape[1] // d
    for r in range(d):
        for cc in range(chunks_ref.shape[0]):
            chunks_ref[cc, pl.ds(r, rows, stride=d), :] = in_ref[r, :, cc * 128:(cc + 1) * 128]


def _hbm_specs(n):
    return [pl.BlockSpec(memory_space=pl.ANY)] * n


def _position():
    x, y, c = lax.axis_index("x"), lax.axis_index("y"), lax.axis_index("c")
    return x, y, c


def _all_gather_chips(shards):
    n = len(shards)

    def body(*refs):
        ins, outs = refs[:n], refs[n:2 * n]
        send_sems, recv_sems, local_sems = refs[2 * n:]
        x, y, c = _position()
        j = 2 * x + y
        me, sibling = (x, y, c), (x, y, 1 - c)
        chips = [(1 - x, y), (x, 1 - y), (1 - x, 1 - y)]

        def copy(i, k, src, dst, to):
            return pltpu.make_async_remote_copy(
                src_ref=src, dst_ref=dst, send_sem=send_sems.at[i, k], recv_sem=recv_sems.at[i, k],
                device_id=to, device_id_type=MESH)

        local = [pltpu.make_async_copy(ins[i], outs[i].at[j], local_sems.at[i]) for i in range(n)]
        for cp in local:
            cp.start()
        first = []
        for i in range(n):
            for k, chip in enumerate(chips):
                cp = copy(i, k, ins[i].at[c], outs[i].at[j, c], (*chip, c))
                cp.start()
                first.append(cp)
        passed = []
        for i in range(n):
            for k, chip in enumerate(chips):
                pj = 2 * chip[0] + chip[1]
                copy(i, k, ins[i].at[c], outs[i].at[pj, c], me).wait_recv()
                cp = copy(i, 3 + k, outs[i].at[pj, c], outs[i].at[pj, c], sibling)
                cp.start()
                passed.append(cp)
        for i in range(n):
            for k, chip in enumerate(chips):
                pj = 2 * chip[0] + chip[1]
                copy(i, 3 + k, ins[i].at[c], outs[i].at[pj, 1 - c], me).wait_recv()
        for cp in first + passed:
            cp.wait_send()
        for cp in local:
            cp.wait()

    return pl.pallas_call(
        body, name="all_gather_chips",
        out_shape=[SDS((N_CHIPS,) + s.shape, s.dtype) for s in shards],
        in_specs=_hbm_specs(n), out_specs=_hbm_specs(n),
        scratch_shapes=[pltpu.SemaphoreType.DMA((n, 6)), pltpu.SemaphoreType.DMA((n, 6)),
                        pltpu.SemaphoreType.DMA((n,))],
    )(*shards)


def _pair_exchange(grads):
    n = len(grads)

    def body(*refs):
        ins, outs = refs[:n], refs[n:2 * n]
        send_sems, recv_sems = refs[2 * n:]
        x, y, c = _position()
        sibling = (x, y, 1 - c)
        remote = []
        for i in range(n):
            for j in range(N_CHIPS):
                remote.append(pltpu.make_async_remote_copy(
                    src_ref=ins[i].at[j, 1 - c], dst_ref=outs[i].at[j],
                    send_sem=send_sems.at[i, j], recv_sem=recv_sems.at[i, j],
                    device_id=sibling, device_id_type=MESH))
        for cp in remote:
            cp.start()
        for cp in remote:
            cp.wait_recv()
        for cp in remote:
            cp.wait_send()

    return pl.pallas_call(
        body, name="grad_pair_exchange",
        out_shape=[SDS((N_CHIPS,) + g.shape[2:], g.dtype) for g in grads],
        in_specs=_hbm_specs(n), out_specs=_hbm_specs(n),
        scratch_shapes=[pltpu.SemaphoreType.DMA((n, N_CHIPS)), pltpu.SemaphoreType.DMA((n, N_CHIPS))],
    )(*grads)


def _chip_scatter(parts):
    n = len(parts)

    def body(*refs):
        ins, outs = refs[:n], refs[n:2 * n]
        send_sems, recv_sems = refs[2 * n:]
        x, y, c = _position()
        chips = [(1 - x, y), (x, 1 - y), (1 - x, 1 - y)]
        remote = []
        for i in range(n):
            for k, chip in enumerate(chips):
                remote.append(pltpu.make_async_remote_copy(
                    src_ref=ins[i].at[2 * chip[0] + chip[1]], dst_ref=outs[i].at[k],
                    send_sem=send_sems.at[i, k], recv_sem=recv_sems.at[i, k],
                    device_id=(*chip, c), device_id_type=MESH))
        for cp in remote:
            cp.start()
        for cp in remote:
            cp.wait_recv()
        for cp in remote:
            cp.wait_send()

    return pl.pallas_call(
        body, name="grad_chip_scatter",
        out_shape=[SDS((3,) + p.shape[1:], p.dtype) for p in parts],
        in_specs=_hbm_specs(n), out_specs=_hbm_specs(n),
        scratch_shapes=[pltpu.SemaphoreType.DMA((n, 3)), pltpu.SemaphoreType.DMA((n, 3))],
    )(*parts)


def _pair_share(halves):
    n = len(halves)

    def body(*refs):
        ins, outs = refs[:n], refs[n:2 * n]
        send_sems, recv_sems = refs[2 * n:]
        x, y, c = _position()
        remote = [pltpu.make_async_remote_copy(
            src_ref=ins[i], dst_ref=outs[i], send_sem=send_sems.at[i], recv_sem=recv_sems.at[i],
            device_id=(x, y, 1 - c), device_id_type=MESH) for i in range(n)]
        for cp in remote:
            cp.start()
        for cp in remote:
            cp.wait_recv()
        for cp in remote:
            cp.wait_send()

    return pl.pallas_call(
        body, name="grad_pair_share",
        out_shape=[SDS(h.shape, h.dtype) for h in halves],
        in_specs=_hbm_specs(n), out_specs=_hbm_specs(n),
        scratch_shapes=[pltpu.SemaphoreType.DMA((n,)), pltpu.SemaphoreType.DMA((n,))],
    )(*halves)


def _gather_all_devices(slab):
    def body(in_ref, out_ref, send_sems, recv_sems, local_sem):
        x, y, c = _position()
        me = 4 * x + 2 * y + c
        local = pltpu.make_async_copy(in_ref, out_ref.at[me], local_sem)
        local.start()
        remote, landing = [], []
        for mask in range(1, N_DEV):
            px, py, pc = x ^ (mask >> 2), y ^ ((mask >> 1) & 1), c ^ (mask & 1)
            peer = 4 * px + 2 * py + pc
            remote.append(pltpu.make_async_remote_copy(
                src_ref=in_ref, dst_ref=out_ref.at[me], send_sem=send_sems.at[mask - 1],
                recv_sem=recv_sems.at[mask - 1], device_id=(px, py, pc), device_id_type=MESH))
            landing.append(pltpu.make_async_remote_copy(
                src_ref=in_ref, dst_ref=out_ref.at[peer], send_sem=send_sems.at[mask - 1],
                recv_sem=recv_sems.at[mask - 1], device_id=(px, py, pc), device_id_type=MESH))
        for cp in remote:
            cp.start()
        for cp in landing:
            cp.wait_recv()
        for cp in remote:
            cp.wait_send()
        local.wait()

    return pl.pallas_call(
        body, name="small_grad_gather",
        out_shape=SDS((N_DEV,) + slab.shape, slab.dtype),
        in_specs=_hbm_specs(1), out_specs=pl.BlockSpec(memory_space=pl.ANY),
        scratch_shapes=[pltpu.SemaphoreType.DMA((N_DEV - 1,)), pltpu.SemaphoreType.DMA((N_DEV - 1,)),
                        pltpu.SemaphoreType.DMA],
    )(slab)


def _row_splits(arrays):
    return min(a.shape[-2] for a in arrays) // 16


def _pair_sum(grads, recvd, core):
    n = len(grads)
    splits = _row_splits(recvd)

    def body(core_ref, *refs):
        for i in range(n):
            s = refs[i][...] + refs[n + i][...]
            refs[2 * n + i][...] = s
            refs[3 * n + i][...] = s.astype(BF16)

    mine = [pl.BlockSpec((N_CHIPS, None, r.shape[1] // splits, r.shape[2]), lambda s, core: (0, core[0], s, 0))
            for r in recvd]
    block = [pl.BlockSpec((N_CHIPS, r.shape[1] // splits, r.shape[2]), lambda s, core: (0, s, 0)) for r in recvd]
    outs = pl.pallas_call(
        body, name="grad_pair_sum",
        grid_spec=pltpu.PrefetchScalarGridSpec(
            num_scalar_prefetch=1, grid=(splits,), in_specs=mine + block, out_specs=block + block),
        out_shape=[SDS(r.shape, F32) for r in recvd] + [SDS(r.shape, BF16) for r in recvd],
        compiler_params=_params(("parallel",)),
    )(core, *grads, *recvd)
    return outs[:n], outs[n:]


def _chip_sum(parts, landed, chip):
    n = len(parts)
    splits = _row_splits(landed)

    def body(chip_ref, *refs):
        for i in range(n):
            acc = refs[i][...]
            for k in range(3):
                acc = acc + refs[n + i][k].astype(F32)
            refs[2 * n + i][...] = acc

    rows = lambda p: p.shape[1] // splits
    return pl.pallas_call(
        body, name="grad_chip_sum",
        grid_spec=pltpu.PrefetchScalarGridSpec(
            num_scalar_prefetch=1, grid=(splits,),
            in_specs=[pl.BlockSpec((None, rows(p), p.shape[2]), lambda s, chip: (chip[0], s, 0)) for p in parts]
            + [pl.BlockSpec((3, rows(p), p.shape[2]), lambda s, chip: (0, s, 0)) for p in parts],
            out_specs=[pl.BlockSpec((rows(p), p.shape[2]), lambda s, chip: (s, 0)) for p in parts]),
        out_shape=[SDS(p.shape[1:], F32) for p in parts],
        compiler_params=_params(("parallel",)),
    )(chip, *parts, *landed)


def _adamw_math(w, g, m, v):
    m = ADAM_B1 * m + (1.0 - ADAM_B1) * g
    v = ADAM_B2 * v + (1.0 - ADAM_B2) * (g * g)
    m_hat = m / (1.0 - ADAM_B1 ** ADAM_STEP)
    v_hat = v / (1.0 - ADAM_B2 ** ADAM_STEP)
    delta = -ADAM_LR * (m_hat / (jnp.sqrt(v_hat) + ADAM_EPS) + ADAM_WD * w)
    return delta, m, v


def _adamw(name, ws, gs, ms, vs, splits):
    n = len(ws)

    def body(*refs):
        for i in range(n):
            w, g, m, v = (refs[q * n + i][...] for q in range(4))
            delta, m, v = _adamw_math(w, g, m, v)
            refs[4 * n + i][...] = delta
            refs[5 * n + i][...] = m
            refs[6 * n + i][...] = v

    def spec(a):
        if splits == 1:
            return pl.BlockSpec(a.shape, lambda s: (0, 0))
        return pl.BlockSpec((a.shape[0] // splits, a.shape[1]), lambda s: (s, 0))

    specs = [spec(a) for a in ws]
    outs = pl.pallas_call(
        body, name=name, grid=(splits,),
        in_specs=specs * 4, out_specs=specs * 3,
        out_shape=[SDS(a.shape, F32) for a in ws] * 3,
        compiler_params=_params(("parallel",)),
    )(*ws, *gs, *ms, *vs)
    return outs[:n], outs[n:2 * n], outs[2 * n:]


def _adamw_halves(ws, own, other, ms, vs, core):
    n = len(ws)
    splits = _row_splits(own)

    def body(core_ref, *refs):
        mine = pl.program_id(0) == core_ref[0]
        for i in range(n):
            g = jnp.where(mine, refs[n + i][...], refs[2 * n + i][...])
            delta, m, v = _adamw_math(refs[i][...], g, refs[3 * n + i][...], refs[4 * n + i][...])
            refs[5 * n + i][...] = g
            refs[6 * n + i][...] = delta
            refs[7 * n + i][...] = m
            refs[8 * n + i][...] = v

    rows = lambda a: a.shape[0] // splits
    half = [pl.BlockSpec((None, rows(a), a.shape[1]), lambda hh, s, core: (hh, s, 0)) for a in own]
    flat = [pl.BlockSpec((rows(a), a.shape[1]), lambda hh, s, core: (s, 0)) for a in own]
    outs = pl.pallas_call(
        body, name="adamw_big",
        grid_spec=pltpu.PrefetchScalarGridSpec(
            num_scalar_prefetch=1, grid=(2, splits), in_specs=half + flat + flat + half + half, out_specs=half * 4),
        out_shape=[SDS(w.shape, F32) for w in ws] * 4,
        compiler_params=_params(("parallel", "parallel")),
    )(core, *ws, *own, *other, *ms, *vs)
    return outs[:n], outs[n:2 * n], outs[2 * n:3 * n], outs[3 * n:]


def _sum_devices(slabs):
    def body(in_ref, out_ref):
        acc = in_ref[0]
        for k in range(1, N_DEV):
            acc = acc + in_ref[k]
        out_ref[...] = acc

    return pl.pallas_call(
        body, name="small_grad_sum", out_shape=SDS(slabs.shape[1:], F32),
    )(slabs)


def _mm_nn(name, a, w, out_dtype, tm, scale_first_tile=None):
    S, K = a.shape
    N = w.shape[1]
    tn = K

    def body(a_ref, w_ref, o_ref):
        acc = _dot(a_ref[...], w_ref[...])
        if scale_first_tile is not None:
            acc = acc * jnp.where(pl.program_id(1) == 0, scale_first_tile, 1.0)
        o_ref[...] = acc.astype(out_dtype)

    return pl.pallas_call(
        body, name=name, grid=(S // tm, N // tn),
        in_specs=[pl.BlockSpec((tm, K), lambda i, t: (i, 0)), pl.BlockSpec((K, tn), lambda i, t: (0, t))],
        out_specs=pl.BlockSpec((tm, tn), lambda i, t: (i, t)),
        out_shape=SDS((S, N), out_dtype),
        compiler_params=_params(("parallel", "arbitrary")),
    )(a, w)


def _mm_tn(name, a, b, tk):
    S, M = a.shape
    N = b.shape[1]
    tn = M

    def body(a_ref, b_ref, o_ref):
        @pl.when(pl.program_id(1) == 0)
        def _():
            o_ref[...] = jnp.zeros_like(o_ref)
        o_ref[...] += _tdot(a_ref[...], b_ref[...])

    return pl.pallas_call(
        body, name=name, grid=(N // tn, S // tk),
        in_specs=[pl.BlockSpec((tk, M), lambda t, k: (k, 0)), pl.BlockSpec((tk, tn), lambda t, k: (k, t))],
        out_specs=pl.BlockSpec((M, tn), lambda t, k: (0, t)),
        out_shape=SDS((M, N), F32),
        compiler_params=_params(("parallel", "arbitrary")),
    )(a, b)


def _a_in_proj(x, w4, b_in, tm):
    S, D = x.shape
    nj = w4.shape[2]

    def body(x_ref, w_ref, b_ref, h_ref, xb_ref):
        xb = x_ref[...].astype(BF16)

        @pl.when(pl.program_id(1) == 0)
        def _():
            xb_ref[...] = xb
        h_ref[...] = _dot(xb, w_ref[...]) + b_ref[...]

    return pl.pallas_call(
        body, name="a_in_proj", grid=(S // tm, N_CHIPS),
        in_specs=[pl.BlockSpec((tm, D), lambda i, t: (i, 0)),
                  pl.BlockSpec((None, D, nj), lambda i, t: (t, 0, 0)),
                  pl.BlockSpec((1, nj), lambda i, t: (0, t))],
        out_specs=[pl.BlockSpec((tm, nj), lambda i, t: (i, t)), pl.BlockSpec((tm, D), lambda i, t: (i, 0))],
        out_shape=[SDS((S, N_CHIPS * nj), F32), SDS((S, D), BF16)],
        compiler_params=_params(("parallel", "arbitrary")),
    )(x, w4, b_in)


def _fill_glu_ext(ext_ref, a_ref, g_ref, ah_ref, gh_ref, has_prev):
    u0h = ah_ref[...] * _sigmoid(gh_ref[...])
    ext_ref[0:HALO, :] = jnp.where(has_prev, u0h, 0.0)
    ext_ref[HALO:, :] = a_ref[...] * _sigmoid(g_ref[...])


def _fill_shifts(shift_ref, ext_ref):
    for s in range(1, 8):
        shift_ref[s - 1] = ext_ref[s:s + shift_ref.shape[1], :]


def _tap(ext_ref, shift_ref, start, n):
    s = start % 8
    if s == 0:
        return ext_ref[start:start + n, :]
    return shift_ref[s - 1, start - s:start - s + n, :]


def _a_conv_out(h, x, wdw, bdw, lng, lnb, wout, bout, pg, pb, tm):
    S, D = x.shape
    hb = tm // HALO
    d1, d2 = DILATIONS[1], DILATIONS[2]

    def body(a_ref, g_ref, z_ref, ah_ref, gh_ref, x_ref, wdw_ref, bdw_ref, lng_ref, lnb_ref, wout_ref,
             bout_ref, pg_ref, pb_ref, xhu_ref, rsu_ref, vb_ref, xh1_ref, rs1_ref, x1b_ref, x1p1_ref,
             x1p2_ref, ext_ref, u1_ref, x1_ref, sh_ref):
        i = pl.program_id(0)
        _fill_glu_ext(ext_ref, a_ref, g_ref, ah_ref, gh_ref, i > 0)
        _fill_shifts(sh_ref, ext_ref)
        off = HALO - (CONV_WIDTH - 1)
        for r0 in range(0, tm, CONV_ROWS):
            acc = jnp.broadcast_to(bdw_ref[...], (CONV_ROWS, D))
            for k in range(CONV_WIDTH):
                acc = acc + wdw_ref[k:k + 1, :] * _tap(ext_ref, sh_ref, r0 + off + k, CONV_ROWS)
            u1_ref[r0:r0 + CONV_ROWS, :] = acc
        xhu, rsu = _ln_fwd(u1_ref[...])
        xhu_ref[...] = xhu
        rsu_ref[...] = rsu
        u2 = xhu * lng_ref[...] + lnb_ref[...]
        z = z_ref[...]
        v = (u2 * _sigmoid(u2)) * (z * _sigmoid(z))
        vb = v.astype(BF16)
        vb_ref[...] = vb
        s1 = ALPHA * x_ref[...] + _dot(vb, wout_ref[...]) + bout_ref[...]
        xh1, rs1 = _ln_fwd(s1)
        xh1_ref[...] = xh1
        rs1_ref[...] = rs1
        x1 = xh1 * pg_ref[...] + pb_ref[...]
        x1b_ref[...] = x1.astype(BF16)
        _to_chunks(x1_ref, x1)
        _deinterleave(x1_ref, x1p1_ref, d1, BF16)
        _deinterleave(x1_ref, x1p2_ref, d2, BF16)

    tile = lambda c: pl.BlockSpec((tm, D), lambda i, c=c: (i, c))
    halo = lambda c: pl.BlockSpec((HALO, D), lambda i, c=c: (jnp.maximum(i * hb - 1, 0), c))
    row = pl.BlockSpec((1, D), lambda i: (0, 0))
    stat = pl.BlockSpec((tm, 1), lambda i: (i, 0))
    return pl.pallas_call(
        body, name="a_conv_out", grid=(S // tm,),
        in_specs=[tile(0), tile(1), tile(2), halo(0), halo(1), tile(0),
                  pl.BlockSpec((HALO, D), lambda i: (0, 0)), row, row, row,
                  pl.BlockSpec((D, D), lambda i: (0, 0)), row, row, row],
        out_specs=[tile(0), stat, tile(0), tile(0), stat, tile(0),
                   pl.BlockSpec((d1, tm // d1, D), lambda i: (0, i, 0)),
                   pl.BlockSpec((d2, tm // d2, D), lambda i: (0, i, 0))],
        out_shape=[SDS((S, D), F32), SDS((S, 1), F32), SDS((S, D), BF16), SDS((S, D), F32), SDS((S, 1), F32),
                   SDS((S, D), BF16), SDS((d1, S // d1, D), BF16), SDS((d2, S // d2, D), BF16)],
        scratch_shapes=[pltpu.VMEM((HALO + tm, D), F32), pltpu.VMEM((tm, D), F32),
                        pltpu.VMEM((D // 128, tm, 128), F32), pltpu.VMEM((7, HALO + tm - 8, D), F32)],
        compiler_params=_params(("parallel",)),
    )(h, h, h, h, h, x, wdw, bdw, lng, lnb, wout, bout, pg, pb)


def _band(n, dilation):
    qi = lax.broadcasted_iota(jnp.int32, (BLOCK, 2 * BLOCK), 0)
    kj = lax.broadcasted_iota(jnp.int32, (BLOCK, 2 * BLOCK), 1)
    dist = qi + BLOCK - kj
    valid = (dist >= 0) & (dist <= BLOCK) & ((n > 0) | (kj >= BLOCK))
    return valid, dist.astype(F32) * float(dilation)


def _attn_fwd(g, qkv, D):
    S = qkv.shape[0]
    d = DILATIONS[g]
    nb = S // (d * BLOCK)
    H = D // HEAD_DIM
    slopes = _slopes(H)

    def body(q_ref, kp_ref, kc_ref, vp_ref, vc_ref, o_ref, lse_ref):
        valid, distf = _band(pl.program_id(1), d)
        lane = lax.broadcasted_iota(jnp.int32, (BLOCK, LSE_LANES), 1)
        low = lane < HEAD_DIM
        lse = jnp.zeros((BLOCK, LSE_LANES), F32)
        for hp in range(H // 2):
            sl = slice(hp * 128, (hp + 1) * 128)
            q = q_ref[:, sl]
            k = jnp.concatenate([kp_ref[:, sl], kc_ref[:, sl]], axis=0)
            v = jnp.concatenate([vp_ref[:, sl], vc_ref[:, sl]], axis=0)
            o = []
            for a in range(2):
                h = 2 * hp + a
                s = _dot_t(jnp.where(low if a == 0 else ~low, q, jnp.zeros_like(q)), k)
                s = jnp.where(valid, s - slopes[h] * distf, -jnp.inf)
                m = jnp.max(s, axis=1, keepdims=True)
                p = jnp.exp(s - m)
                l = jnp.sum(p, axis=1, keepdims=True)
                o.append(_dot(p.astype(BF16), v) / l)
                lse = jnp.where(lane == h, m + jnp.log(l), lse)
            o_ref[:, sl] = jnp.where(low, o[0], o[1])
        lse_ref[...] = lse

    cur = lambda c: pl.BlockSpec((BLOCK, D), lambda r, n, c=c: (r * nb + n, c))
    prev = lambda c: pl.BlockSpec((BLOCK, D), lambda r, n, c=c: (r * nb + jnp.maximum(n - 1, 0), c))
    return pl.pallas_call(
        body, name="attn_fwd_g%d" % g, grid=(d, nb),
        in_specs=[cur(0), prev(1), cur(1), prev(2), cur(2)],
        out_specs=[cur(0), pl.BlockSpec((BLOCK, LSE_LANES), lambda r, n: (r * nb + n, 0))],
        out_shape=[SDS((S, D), F32), SDS((S, LSE_LANES), F32)],
        compiler_params=_params(("parallel", "parallel")),
    )(qkv, qkv, qkv, qkv, qkv)


def _attn_bwd(g, qkv, do, lse, delta, D):
    S = qkv.shape[0]
    d = DILATIONS[g]
    nb = S // (d * BLOCK)
    H = D // HEAD_DIM
    slopes = _slopes(H)

    def body(q_ref, kp_ref, kc_ref, vp_ref, vc_ref, do_ref, lse_ref, dl_ref, dq_ref, dkv_ref, ck_ref, cv_ref):
        n = pl.program_id(1)

        @pl.when(n == 0)
        def _():
            ck_ref[...] = jnp.zeros_like(ck_ref)
            cv_ref[...] = jnp.zeros_like(cv_ref)

        @pl.when(n < nb)
        def _():
            valid, distf = _band(n, d)
            low = lax.broadcasted_iota(jnp.int32, (BLOCK, 128), 1) < HEAD_DIM
            low2 = lax.broadcasted_iota(jnp.int32, (2 * BLOCK, 128), 1) < HEAD_DIM
            for hp in range(H // 2):
                sl = slice(hp * 128, (hp + 1) * 128)
                q = q_ref[:, sl]
                do2 = do_ref[:, sl]
                k = jnp.concatenate([kp_ref[:, sl], kc_ref[:, sl]], axis=0)
                v = jnp.concatenate([vp_ref[:, sl], vc_ref[:, sl]], axis=0)
                dq, dk, dv = [], [], []
                for a in range(2):
                    h = 2 * hp + a
                    keep = low if a == 0 else ~low
                    s = _dot_t(jnp.where(keep, q, jnp.zeros_like(q)), k)
                    s = jnp.where(valid, s - slopes[h] * distf, -jnp.inf)
                    p = jnp.exp(s - lse_ref[:, h:h + 1])
                    dp = _dot_t(jnp.where(keep, do2, jnp.zeros_like(do2)), v)
                    dsb = (p * (dp - dl_ref[:, h:h + 1])).astype(BF16)
                    dq.append(_dot(dsb, k))
                    dk.append(_tdot(dsb, q))
                    dv.append(_tdot(p.astype(BF16), do2))
                dq_ref[:, sl] = (jnp.where(low, dq[0], dq[1]) * (HEAD_DIM ** -0.5)).astype(BF16)
                dk2 = jnp.where(low2, dk[0], dk[1])
                dv2 = jnp.where(low2, dv[0], dv[1])
                dkv_ref[:, sl] = (ck_ref[:, sl] + dk2[:BLOCK]).astype(BF16)
                dkv_ref[:, D + hp * 128:D + (hp + 1) * 128] = (cv_ref[:, sl] + dv2[:BLOCK]).astype(BF16)
                ck_ref[:, sl] = dk2[BLOCK:]
                cv_ref[:, sl] = dv2[BLOCK:]

        @pl.when(n == nb)
        def _():
            dkv_ref[:, :D] = ck_ref[...].astype(BF16)
            dkv_ref[:, D:] = cv_ref[...].astype(BF16)

    nq = lambda n: jnp.minimum(n, nb - 1)
    cur = lambda c: pl.BlockSpec((BLOCK, D), lambda r, n, c=c: (r * nb + nq(n), c))
    prev = lambda c: pl.BlockSpec((BLOCK, D), lambda r, n, c=c: (r * nb + jnp.maximum(nq(n) - 1, 0), c))
    stat = pl.BlockSpec((BLOCK, LSE_LANES), lambda r, n: (r * nb + nq(n), 0))
    return pl.pallas_call(
        body, name="attn_bwd_g%d" % g, grid=(d, nb + 1),
        in_specs=[cur(0), prev(1), cur(1), prev(2), cur(2), cur(0), stat, stat],
        out_specs=[cur(0), pl.BlockSpec((BLOCK, 2 * D), lambda r, n: (r * nb + jnp.maximum(n - 1, 0), 0))],
        out_shape=[SDS((S, D), BF16), SDS((S, 2 * D), BF16)],
        scratch_shapes=[pltpu.VMEM((BLOCK, D), F32), pltpu.VMEM((BLOCK, D), F32)],
        compiler_params=_params(("parallel", "arbitrary")),
    )(qkv, qkv, qkv, qkv, qkv, do, lse, delta)


def _b_merge_out_loss(o0, o1, o2, l0, l1, l2, z2, xh1, target, wbo, bbo, pg0, pb0, pg1, pb1, tm):
    S, D = o0.shape
    H = D // HEAD_DIM
    d1, d2 = DILATIONS[1], DILATIONS[2]
    inv_d = 1.0 / D

    def body(o0_ref, o1_ref, o2_ref, l0_ref, l1_ref, l2_ref, z_ref, xh1_ref, t_ref, wbo_ref, bbo_ref,
             pg0_ref, pb0_ref, pg1_ref, pb1_ref,
             v2b_ref, ds2_ref, ds2b_ref, dz2b_ref, da0_ref, da1_ref, da2_ref, ls0_ref, ls1_ref, ls2_ref,
             dl0_ref, dl1_ref, dl2_ref, loss_ref, sums_ref,
             o1n_ref, o2n_ref, l1n_ref, l2n_ref, att_ref, st_ref):
        i = pl.program_id(0)
        _interleave(o1_ref, o1n_ref, d1)
        _interleave(o2_ref, o2n_ref, d2)
        for r in range(d1):
            l1n_ref[pl.ds(r, tm // d1, stride=d1), :] = l1_ref[r]
        for r in range(d2):
            l2n_ref[pl.ds(r, tm // d2, stride=d2), :] = l2_ref[r]
        la, lb, lc = l0_ref[...], l1n_ref[...], l2n_ref[...]
        m = jnp.maximum(jnp.maximum(la, lb), lc)
        ea, eb, ec = jnp.exp(la - m), jnp.exp(lb - m), jnp.exp(lc - m)
        den = ea + eb + ec
        wa, wb, wc = ea / den, eb / den, ec / den
        lse = m + jnp.log(den)
        for h in range(H):
            sl = slice(h * HEAD_DIM, (h + 1) * HEAD_DIM)
            cc, hl = divmod(h * HEAD_DIM, 128)
            att_ref[:, sl] = (wa[:, h:h + 1] * o0_ref[:, sl] + wb[:, h:h + 1] * o1n_ref[cc, :, hl:hl + HEAD_DIM]
                              + wc[:, h:h + 1] * o2n_ref[cc, :, hl:hl + HEAD_DIM])
        att = att_ref[...]
        z = z_ref[...]
        sz = _sigmoid(z)
        gate = z * sz
        v2b = (att * gate).astype(BF16)
        v2b_ref[...] = v2b
        x1 = xh1_ref[...] * pg0_ref[...] + pb0_ref[...]
        s2 = ALPHA * x1 + _dot(v2b, wbo_ref[...]) + bbo_ref[...]
        xh2, rs2 = _ln_fwd(s2)
        err = xh2 * pg1_ref[...] + pb1_ref[...] - t_ref[...]
        dy = err * inv_d
        ds2 = _ln_bwd(dy * pg1_ref[...], xh2, rs2)
        ds2b = ds2.astype(BF16)
        ds2_ref[...] = ds2
        ds2b_ref[...] = ds2b

        @pl.when(i == 0)
        def _():
            loss_ref[...] = jnp.zeros_like(loss_ref)
            sums_ref[...] = jnp.zeros_like(sums_ref)
        loss_ref[...] += 0.5 * inv_d * jnp.sum(err * err)
        sums_ref[0:1, :] += _colsum(dy * xh2)
        sums_ref[1:2, :] += _colsum(dy)
        sums_ref[2:3, :] += _colsum(ds2)

        dv2 = _dot_t(ds2b, wbo_ref[...])
        datt = dv2 * gate
        dz2b_ref[...] = (dv2 * att * _silu_grad(z, sz)).astype(BF16)
        prod = datt * att
        lane = lax.broadcasted_iota(jnp.int32, (tm, LSE_LANES), 1)
        dl = jnp.zeros((tm, LSE_LANES), F32)
        for h in range(H):
            sl = slice(h * HEAD_DIM, (h + 1) * HEAD_DIM)
            dl = jnp.where(lane == h, jnp.sum(prod[:, sl], axis=1, keepdims=True), dl)
        da0_ref[...] = datt.astype(BF16)
        ls0_ref[...] = lse
        dl0_ref[...] = dl
        _to_chunks(o1n_ref, datt)
        _deinterleave(o1n_ref, da1_ref, d1, BF16)
        _deinterleave(o1n_ref, da2_ref, d2, BF16)
        st_ref[...] = lse
        for r in range(d1):
            ls1_ref[r] = st_ref[pl.ds(r, tm // d1, stride=d1), :]
        for r in range(d2):
            ls2_ref[r] = st_ref[pl.ds(r, tm // d2, stride=d2), :]
        st_ref[...] = dl
        for r in range(d1):
            dl1_ref[r] = st_ref[pl.ds(r, tm // d1, stride=d1), :]
        for r in range(d2):
            dl2_ref[r] = st_ref[pl.ds(r, tm // d2, stride=d2), :]

    tile = pl.BlockSpec((tm, D), lambda i: (i, 0))
    stat = pl.BlockSpec((tm, LSE_LANES), lambda i: (i, 0))
    perm = lambda d, w: pl.BlockSpec((d, tm // d, w), lambda i: (0, i, 0))
    row = pl.BlockSpec((1, D), lambda i: (0, 0))
    acc = lambda w: pl.BlockSpec((8, w), lambda i: (0, 0))
    pshape = lambda d, w, dt: SDS((d, S // d, w), dt)
    return pl.pallas_call(
        body, name="b_merge_out_loss", grid=(S // tm,),
        in_specs=[tile, perm(d1, D), perm(d2, D), stat, perm(d1, LSE_LANES), perm(d2, LSE_LANES),
                  tile, tile, tile, pl.BlockSpec((D, D), lambda i: (0, 0)), row, row, row, row, row],
        out_specs=[tile, tile, tile, tile, tile, perm(d1, D), perm(d2, D),
                   stat, perm(d1, LSE_LANES), perm(d2, LSE_LANES),
                   stat, perm(d1, LSE_LANES), perm(d2, LSE_LANES), acc(LSE_LANES), acc(D)],
        out_shape=[SDS((S, D), BF16), SDS((S, D), F32), SDS((S, D), BF16), SDS((S, D), BF16),
                   SDS((S, D), BF16), pshape(d1, D, BF16), pshape(d2, D, BF16),
                   SDS((S, LSE_LANES), F32), pshape(d1, LSE_LANES, F32), pshape(d2, LSE_LANES, F32),
                   SDS((S, LSE_LANES), F32), pshape(d1, LSE_LANES, F32), pshape(d2, LSE_LANES, F32),
                   SDS((8, LSE_LANES), F32), SDS((8, D), F32)],
        scratch_shapes=[pltpu.VMEM((D // 128, tm, 128), F32), pltpu.VMEM((D // 128, tm, 128), F32),
                        pltpu.VMEM((tm, LSE_LANES), F32), pltpu.VMEM((tm, LSE_LANES), F32),
                        pltpu.VMEM((tm, D), F32), pltpu.VMEM((tm, LSE_LANES), F32)],
        compiler_params=_params(("arbitrary",)),
    )(o0, o1, o2, l0, l1, l2, z2, xh1, target, wbo, bbo, pg0, pb0, pg1, pb1)


def _b_dx1_ln1_bwd(ds2, dz2b, dq, dkv, xh1, rs1, wz, wg, pg0, tm):
    S, D = ds2.shape
    d1, d2 = DILATIONS[1], DILATIONS[2]

    def group_part(dq_blk, dkv_blk, w_ref):
        return (_dot_t(dq_blk, w_ref[:, 0:D]) + _dot_t(dkv_blk[:, 0:D], w_ref[:, D:2 * D])
                + _dot_t(dkv_blk[:, D:2 * D], w_ref[:, 2 * D:3 * D]))

    def body(ds2_ref, dz_ref, dq0_ref, dkv0_ref, dq1_ref, dkv1_ref, dq2_ref, dkv2_ref, xh1_ref, rs1_ref,
             wz_ref, w0_ref, w1_ref, w2_ref, pg0_ref, ds1_ref, ds1b_ref, sums_ref, acc_ref):
        i = pl.program_id(0)
        _to_chunks(acc_ref, ALPHA * ds2_ref[...] + _dot_t(dz_ref[...], wz_ref[...])
                   + group_part(dq0_ref[...], dkv0_ref[...], w0_ref))
        for d, dq_ref, dkv_ref, w_ref in ((d1, dq1_ref, dkv1_ref, w1_ref), (d2, dq2_ref, dkv2_ref, w2_ref)):
            rows = tm // d
            part = group_part(dq_ref[...].reshape(tm, D), dkv_ref[...].reshape(tm, 2 * D), w_ref)
            for r in range(d):
                idx = pl.ds(r, rows, stride=d)
                for cc in range(D // 128):
                    acc_ref[cc, idx, :] = acc_ref[cc, idx, :] + part[r * rows:(r + 1) * rows, cc * 128:(cc + 1) * 128]
        dx1 = _from_chunks(acc_ref)
        xh1 = xh1_ref[...]
        ds1 = _ln_bwd(dx1 * pg0_ref[...], xh1, rs1_ref[...])
        ds1_ref[...] = ds1
        ds1b_ref[...] = ds1.astype(BF16)

        @pl.when(i == 0)
        def _():
            sums_ref[...] = jnp.zeros_like(sums_ref)
        sums_ref[0:1, :] += _colsum(dx1 * xh1)
        sums_ref[1:2, :] += _colsum(dx1)
        sums_ref[2:3, :] += _colsum(ds1)

    tile = lambda w: pl.BlockSpec((tm, w), lambda i: (i, 0))
    perm = lambda d, w: pl.BlockSpec((d, tm // d, w), lambda i: (0, i, 0))
    whole = pl.BlockSpec(memory_space=pltpu.VMEM)
    return pl.pallas_call(
        body, name="b_dx1_ln1_bwd", grid=(S // tm,),
        in_specs=[tile(D), tile(D), tile(D), tile(2 * D), perm(d1, D), perm(d1, 2 * D), perm(d2, D),
                  perm(d2, 2 * D), tile(D), tile(1), whole, whole, whole, whole,
                  pl.BlockSpec((1, D), lambda i: (0, 0))],
        out_specs=[tile(D), tile(D), pl.BlockSpec((8, D), lambda i: (0, 0))],
        out_shape=[SDS((S, D), F32), SDS((S, D), BF16), SDS((8, D), F32)],
        scratch_shapes=[pltpu.VMEM((D // 128, tm, 128), F32)],
        compiler_params=_params(("arbitrary",)),
    )(ds2, dz2b, dq[0], dkv[0], dq[1].reshape(d1, S // d1, D), dkv[1].reshape(d1, S // d1, 2 * D),
      dq[2].reshape(d2, S // d2, D), dkv[2].reshape(d2, S // d2, 2 * D), xh1, rs1, wz, wg[0], wg[1], wg[2], pg0)


def _a_gate_bwd(ds1b, h, xhu, rsu, wout, lng, lnb, tm):
    S, D = xhu.shape

    def body(ds_ref, z_ref, xhu_ref, rsu_ref, w_ref, lng_ref, lnb_ref, du1_ref, dzb_ref, sums_ref):
        i = pl.program_id(0)
        dv = _dot_t(ds_ref[...], w_ref[...])
        xhu = xhu_ref[...]
        u2 = xhu * lng_ref[...] + lnb_ref[...]
        su = _sigmoid(u2)
        z = z_ref[...]
        sz = _sigmoid(z)
        dz = dv * (u2 * su) * _silu_grad(z, sz)
        du2 = dv * (z * sz) * _silu_grad(u2, su)
        du1 = _ln_bwd(du2 * lng_ref[...], xhu, rsu_ref[...])
        du1_ref[...] = du1
        dzb_ref[...] = dz.astype(BF16)

        @pl.when(i == 0)
        def _():
            sums_ref[...] = jnp.zeros_like(sums_ref)
        sums_ref[0:1, :] += _colsum(du2 * xhu)
        sums_ref[1:2, :] += _colsum(du2)
        sums_ref[2:3, :] += _colsum(du1)
        sums_ref[3:4, :] += _colsum(dz)

    tile = pl.BlockSpec((tm, D), lambda i: (i, 0))
    row = pl.BlockSpec((1, D), lambda i: (0, 0))
    return pl.pallas_call(
        body, name="a_gate_bwd", grid=(S // tm,),
        in_specs=[tile, pl.BlockSpec((tm, D), lambda i: (i, 2)), tile, pl.BlockSpec((tm, 1), lambda i: (i, 0)),
                  pl.BlockSpec((D, D), lambda i: (0, 0)), row, row],
        out_specs=[tile, tile, pl.BlockSpec((8, D), lambda i: (0, 0))],
        out_shape=[SDS((S, D), F32), SDS((S, D), BF16), SDS((8, D), F32)],
        compiler_params=_params(("arbitrary",)),
    )(ds1b, h, xhu, rsu, wout, lng, lnb)


def _a_conv_bwd(du1, h, wdw, tm):
    S, D = du1.shape
    hb = tm // HALO
    last_halo = S // HALO - 1
    n_tiles = S // tm

    def body(du_ref, dun_ref, a_ref, g_ref, ah_ref, gh_ref, wdw_ref, dag_ref, sums_ref, wsum_ref,
             dext_ref, ext_ref, dsh_ref, sh_ref):
        i = pl.program_id(0)

        @pl.when(i == 0)
        def _():
            sums_ref[...] = jnp.zeros_like(sums_ref)
            wsum_ref[...] = jnp.zeros_like(wsum_ref)
        dext_ref[0:tm, :] = du_ref[...]
        dext_ref[tm:, :] = jnp.where(i < n_tiles - 1, dun_ref[...], 0.0)
        _fill_shifts(dsh_ref, dext_ref)
        _fill_glu_ext(ext_ref, a_ref, g_ref, ah_ref, gh_ref, i > 0)
        _fill_shifts(sh_ref, ext_ref)
        sa = jnp.zeros((1, D), F32)
        sg = jnp.zeros((1, D), F32)
        for r0 in range(0, tm, CONV_ROWS):
            acc = jnp.zeros((CONV_ROWS, D), F32)
            for k in range(CONV_WIDTH):
                acc = acc + wdw_ref[k:k + 1, :] * _tap(dext_ref, dsh_ref, r0 + CONV_WIDTH - 1 - k, CONV_ROWS)
            a = a_ref[r0:r0 + CONV_ROWS, :]
            s = _sigmoid(g_ref[r0:r0 + CONV_ROWS, :])
            da = acc * s
            dg = acc * a * s * (1.0 - s)
            dag_ref[r0:r0 + CONV_ROWS, 0:D] = da.astype(BF16)
            dag_ref[r0:r0 + CONV_ROWS, D:2 * D] = dg.astype(BF16)
            sa = sa + _colsum(da)
            sg = sg + _colsum(dg)
        sums_ref[0:1, :] += sa
        sums_ref[1:2, :] += sg
        off = HALO - (CONV_WIDTH - 1)
        for k in range(CONV_WIDTH):
            acc = jnp.zeros((8, D), F32)
            for r0 in range(0, tm, CONV_ROWS):
                p = du_ref[r0:r0 + CONV_ROWS, :] * _tap(ext_ref, sh_ref, r0 + off + k, CONV_ROWS)
                for q in range(0, CONV_ROWS, 8):
                    acc = acc + p[q:q + 8]
            wsum_ref[k:k + 1, :] += _colsum(acc)

    tile = lambda c: pl.BlockSpec((tm, D), lambda i, c=c: (i, c))
    halo = lambda c: pl.BlockSpec((HALO, D), lambda i, c=c: (jnp.maximum(i * hb - 1, 0), c))
    return pl.pallas_call(
        body, name="a_conv_bwd", grid=(n_tiles,),
        in_specs=[tile(0), pl.BlockSpec((HALO, D), lambda i: (jnp.minimum((i + 1) * hb, last_halo), 0)),
                  tile(0), tile(1), halo(0), halo(1), pl.BlockSpec((HALO, D), lambda i: (0, 0))],
        out_specs=[pl.BlockSpec((tm, 2 * D), lambda i: (i, 0)), pl.BlockSpec((8, D), lambda i: (0, 0)),
                   pl.BlockSpec((HALO, D), lambda i: (0, 0))],
        out_shape=[SDS((S, 2 * D), BF16), SDS((8, D), F32), SDS((HALO, D), F32)],
        scratch_shapes=[pltpu.VMEM((tm + HALO, D), F32), pltpu.VMEM((HALO + tm, D), F32),
                        pltpu.VMEM((7, HALO + tm - 8, D), F32), pltpu.VMEM((7, HALO + tm - 8, D), F32)],
        compiler_params=_params(("arbitrary",)),
    )(du1, du1, h, h, h, h, wdw)


def _a_dx(ds1, dag, dzb, w_in, tm):
    S, D = ds1.shape

    def body(ds_ref, dag_ref, dz_ref, w_ref, o_ref):
        o_ref[...] = (ALPHA * ds_ref[...] + _dot_t(dag_ref[...], w_ref[:, 0:2 * D])
                      + _dot_t(dz_ref[...], w_ref[:, 2 * D:3 * D]))

    tile = lambda w: pl.BlockSpec((tm, w), lambda i: (i, 0))
    return pl.pallas_call(
        body, name="a_dx", grid=(S // tm,),
        in_specs=[tile(D), tile(2 * D), tile(D), pl.BlockSpec(memory_space=pltpu.VMEM)],
        out_specs=tile(D), out_shape=SDS((S, D), F32),
        compiler_params=_params(("parallel",)),
    )(ds1, dag, dzb, w_in)


def _halves(w):
    return w.reshape(2, w.shape[0] // 2, w.shape[1])


def _unstack_cols(w4):
    return jnp.transpose(w4, (1, 0, 2)).reshape(w4.shape[1], N_CHIPS * w4.shape[2])


def _stack_cols(w):
    D, n = w.shape
    return jnp.transpose(w.reshape(D, N_CHIPS, n // N_CHIPS), (1, 0, 2))


def _pack_rows(rows, width):
    slab = jnp.concatenate([r.reshape(-1, width) for r in rows], axis=0)
    return jnp.pad(slab, ((0, SMALL_ROWS - slab.shape[0]), (0, 0)))


def kernel(x, a_w_in, a_b_in, a_w_dw, a_b_dw, a_ln_g, a_ln_b, a_w_out, a_b_out, kv_w, b_w_in, b_w_out, b_b_out, post_ln_g, post_ln_b, loss_target, m_a_w_in, m_a_b_in, m_a_w_dw, m_a_b_dw, m_a_ln_g, m_a_ln_b, m_a_w_out, m_a_b_out, m_kv_w, m_b_w_in, m_b_w_out, m_b_b_out, m_post_ln_g, m_post_ln_b, v_a_w_in, v_a_b_in, v_a_w_dw, v_a_b_dw, v_a_ln_g, v_a_ln_b, v_a_w_out, v_a_b_out, v_kv_w, v_b_w_in, v_b_w_out, v_b_b_out, v_post_ln_g, v_post_ln_b):
    S, D = x.shape[1], x.shape[2]
    dq4 = D // N_CHIPS
    tm = 256
    tm_mm = 512
    x2 = x.reshape(S, D)
    target = loss_target.reshape(S, D)
    jchip = 2 * lax.axis_index("x") + lax.axis_index("y")

    big_local = [a_w_in[0], kv_w, b_w_in[0], a_w_out[0], b_w_out[0]]
    small_local = _pack_rows([a_b_in.reshape(3, dq4), jnp.pad(a_w_dw[0], ((0, 1), (0, 0))), a_b_dw, a_ln_g,
                              a_ln_b, a_b_out], dq4)
    gathered = _all_gather_chips([_halves(w.astype(BF16)) for w in big_local] + [_halves(small_local)])
    w_in4, kv4, b_in4, w_out_a, w_out_b = [g.reshape((N_CHIPS, 2 * g.shape[2], g.shape[3])) for g in gathered[:5]]
    w_in_a = _unstack_cols(w_in4)
    kv_full = _unstack_cols(kv4)
    w_out_a = w_out_a.reshape(D, D)
    w_out_b = w_out_b.reshape(D, D)
    w_z = b_in4[3]
    w_g = [jnp.concatenate([b_in4[g], kv_full[:, g * D:(g + 1) * D], kv_full[:, (3 + g) * D:(4 + g) * D]], axis=1)
           for g in range(3)]
    small = jnp.transpose(gathered[5].reshape(N_CHIPS, SMALL_ROWS, dq4), (1, 0, 2))
    b_in_full = jnp.transpose(small[0:3], (1, 0, 2)).reshape(1, 3 * D)
    wdw_full = small[3:3 + HALO].reshape(HALO, D)
    bdw_full, lng_full, lnb_full, bout_a_full = [small[35 + q].reshape(1, D) for q in range(4)]
    pg0, pg1 = post_ln_g[0:1], post_ln_g[1:2]
    pb0, pb1 = post_ln_b[0:1], post_ln_b[1:2]

    h, xb = _a_in_proj(x2, w_in4, b_in_full, tm_mm)
    xhu, rsu, vb, xh1, rs1, x1b, x1p1, x1p2 = _a_conv_out(
        h, x2, wdw_full, bdw_full, lng_full, lnb_full, w_out_a, bout_a_full, pg0, pb0, tm)
    x1g = [x1b, x1p1.reshape(S, D), x1p2.reshape(S, D)]
    qkv = [_mm_nn("b_qkv_g%d" % g, x1g[g], w_g[g], BF16, tm_mm, scale_first_tile=HEAD_DIM ** -0.5)
           for g in range(3)]
    z2 = _mm_nn("b_gate_proj", x1b, w_z, F32, tm_mm)
    og, lg = zip(*[_attn_fwd(g, qkv[g], D) for g in range(3)])
    d1, d2 = DILATIONS[1], DILATIONS[2]
    (v2b, ds2, ds2b, dz2b, da0, da1, da2, ls0, ls1, ls2, dl0, dl1, dl2, loss_acc, sums_b) = _b_merge_out_loss(
        og[0], og[1].reshape(d1, S // d1, D), og[2].reshape(d2, S // d2, D),
        lg[0], lg[1].reshape(d1, S // d1, LSE_LANES), lg[2].reshape(d2, S // d2, LSE_LANES),
        z2, xh1, target, w_out_b, b_b_out, pg0, pb0, pg1, pb1, tm)

    das = [da0, da1.reshape(S, D), da2.reshape(S, D)]
    lss = [ls0, ls1.reshape(S, LSE_LANES), ls2.reshape(S, LSE_LANES)]
    dls = [dl0, dl1.reshape(S, LSE_LANES), dl2.reshape(S, LSE_LANES)]
    dq, dkv = zip(*[_attn_bwd(g, qkv[g], das[g], lss[g], dls[g], D) for g in range(3)])
    ds1, ds1b, sums_1 = _b_dx1_ln1_bwd(ds2, dz2b, dq, dkv, xh1, rs1, w_z, w_g, pg0, tm)
    du1, dzab, sums_a = _a_gate_bwd(ds1b, h, xhu, rsu, w_out_a, lng_full, lnb_full, tm)
    dag, sums_c, wsum = _a_conv_bwd(du1, h, wdw_full, tm)
    grad_x = _a_dx(ds1, dag, dzab, w_in_a, tm)

    g_w_in = jnp.concatenate([_mm_tn("dw_a_in_ag", xb, dag, tm_mm), _mm_tn("dw_a_in_z", xb, dzab, tm_mm)], axis=1)
    g_w_out_a = _mm_tn("dw_a_out", vb, ds1b, tm_mm)
    g_w_out_b = _mm_tn("dw_b_out", v2b, ds2b, tm_mm)
    g_q = [_mm_tn("dw_b_q_g%d" % g, x1g[g], dq[g], tm_mm) for g in range(3)]
    g_z = _mm_tn("dw_b_z", x1b, dz2b, tm_mm)
    g_kvg = [_mm_tn("dw_kv_g%d" % g, x1g[g], dkv[g], tm_mm) for g in range(3)]
    g_kv = jnp.concatenate([t[:, :D] for t in g_kvg] + [t[:, D:] for t in g_kvg], axis=1)

    def by_chip_cols(gw):
        s4 = _stack_cols(gw)
        return s4.reshape(N_CHIPS, 2, D // 2, s4.shape[2])

    def by_chip_rows(gw):
        return gw.reshape(N_CHIPS, 2, D // 8, D)

    big_grads = [by_chip_cols(g_w_in), by_chip_cols(g_kv), jnp.stack(g_q + [g_z]).reshape(N_CHIPS, 2, D // 2, D),
                 by_chip_rows(g_w_out_a), by_chip_rows(g_w_out_b)]
    core = lax.axis_index("c").astype(jnp.int32).reshape(1)
    parts, parts_wire = _pair_sum(big_grads, _pair_exchange(big_grads), core)
    own_half = _chip_sum(parts, _chip_scatter(parts_wire), jchip.astype(jnp.int32).reshape(1))
    other_half = _pair_share(own_half)

    small_grads = _pack_rows([sums_c[0:1], sums_c[1:2], sums_a[3:4], wsum, sums_a[2:3], sums_a[0:1], sums_a[1:2],
                              sums_1[2:3], sums_b[2:3], sums_1[0:1], sums_b[0:1], sums_1[1:2], sums_b[1:2]], D)
    small_sum = _sum_devices(_gather_all_devices(small_grads))
    loss = lax.psum(loss_acc[0, 0], ("x", "y", "c"))

    big_m = [m_a_w_in[0], m_kv_w, m_b_w_in[0], m_a_w_out[0], m_b_w_out[0]]
    big_v = [v_a_w_in[0], v_kv_w, v_b_w_in[0], v_a_w_out[0], v_b_w_out[0]]
    shards, big_delta, big_new_m, big_new_v = [
        [a.reshape(2 * a.shape[1], a.shape[2]) for a in group] for group in _adamw_halves(
            [_halves(w) for w in big_local], own_half, other_half, [_halves(m) for m in big_m],
            [_halves(v) for v in big_v], core)]

    def chip_cols(rows):
        return lax.dynamic_slice_in_dim(rows, jchip * dq4, dq4, axis=1)

    g_b_in = lax.dynamic_slice_in_dim(small_sum[0:3].reshape(1, 3 * D), jchip * 3 * dq4, 3 * dq4, axis=1)
    small_g = [g_b_in, chip_cols(small_sum[3:3 + CONV_WIDTH]), chip_cols(small_sum[35:36]), chip_cols(small_sum[36:37]),
               chip_cols(small_sum[37:38]), chip_cols(small_sum[38:39]), small_sum[39:40], small_sum[40:42],
               small_sum[42:44]]
    small_w = [a_b_in, a_w_dw[0], a_b_dw, a_ln_g, a_ln_b, a_b_out, b_b_out, post_ln_g, post_ln_b]
    small_m = [m_a_b_in, m_a_w_dw[0], m_a_b_dw, m_a_ln_g, m_a_ln_b, m_a_b_out, m_b_b_out, m_post_ln_g, m_post_ln_b]
    small_v = [v_a_b_in, v_a_w_dw[0], v_a_b_dw, v_a_ln_g, v_a_ln_b, v_a_b_out, v_b_b_out, v_post_ln_g, v_post_ln_b]
    small_delta, small_new_m, small_new_v = _adamw("adamw_small", small_w, small_g, small_m, small_v, 1)

    def ordered(big, sm):
        return (big[0][None], sm[0], sm[1][None], sm[2], sm[3], sm[4], big[3][None], sm[5], big[1], big[2][None],
                big[4][None], sm[6], sm[7], sm[8])

    return (loss, grad_x.reshape(1, S, D), *ordered(shards, small_g), *ordered(big_delta, small_delta),
            *ordered(big_new_m, small_new_m), *ordered(big_new_v, small_new_v))
```

```python
import functools

import numpy as np
import jax
import jax.numpy as jnp
from jax import lax
from jax.experimental import pallas as pl
from jax.experimental.pallas import tpu as pltpu

F32 = jnp.float32
BF16 = jnp.bfloat16
MESH = pl.DeviceIdType.MESH
SDS = jax.ShapeDtypeStruct

HEAD_DIM = 64
BLOCK = 128
DILATIONS = (1, 4, 16)
ALIBI_MAX_EXP = 8.0
CONV_WIDTH = 31
HALO = 32
CONV_ROWS = 128
LSE_LANES = 128
DEPTH = 2
ALPHA = (2.0 * DEPTH) ** 0.25
LN_EPS = 1e-5
ADAM_LR = 0.001
ADAM_B1 = 0.9
ADAM_B2 = 0.999
ADAM_EPS = 1e-08
ADAM_WD = 0.01
ADAM_STEP = 10
N_CHIPS = 4
N_DEV = 8
VMEM_LIMIT = 56 * 2 ** 20
SMALL_ROWS = 48


def _params(sem=None):
    return pltpu.CompilerParams(dimension_semantics=sem, vmem_limit_bytes=VMEM_LIMIT)


def _sigmoid(x):
    return 1.0 / (1.0 + jnp.exp(-x))


def _silu_grad(x, s):
    return s * (1.0 + x * (1.0 - s))


def _ln_fwd(x):
    mu = jnp.mean(x, axis=-1, keepdims=True)
    xc = x - mu
    var = jnp.mean(xc * xc, axis=-1, keepdims=True)
    rstd = lax.rsqrt(var + LN_EPS)
    return xc * rstd, rstd


def _ln_bwd(dxhat, xhat, rstd):
    m1 = jnp.mean(dxhat, axis=-1, keepdims=True)
    m2 = jnp.mean(dxhat * xhat, axis=-1, keepdims=True)
    return rstd * (dxhat - m1 - xhat * m2)


def _dot(a, b):
    return jnp.dot(a, b, preferred_element_type=F32)


def _dot_t(a, b):
    return lax.dot_general(a, b, (((1,), (1,)), ((), ())), preferred_element_type=F32)


def _tdot(a, b):
    return lax.dot_general(a, b, (((0,), (0,)), ((), ())), preferred_element_type=F32)


def _colsum(x):
    return jnp.sum(x, axis=0, keepdims=True)


def _slopes(n_heads):
    return [float(np.float32(2.0 ** (-ALIBI_MAX_EXP * (h + 1) / n_heads))) for h in range(n_heads)]


def _to_chunks(chunks_ref, x):
    for cc in range(chunks_ref.shape[0]):
        chunks_ref[cc] = x[:, cc * 128:(cc + 1) * 128]


def _from_chunks(chunks_ref):
    return jnp.concatenate([chunks_ref[cc] for cc in range(chunks_ref.shape[0])], axis=1)


def _deinterleave(chunks_ref, out_ref, d, dtype):
    rows = chunks_ref.shape[1] // d
    for r in range(d):
        for cc in range(chunks_ref.shape[0]):
            out_ref[r, :, cc * 128:(cc + 1) * 128] = chunks_ref[cc, pl.ds(r, rows, stride=d), :].astype(dtype)


def _interleave(in_ref, chunks_ref, d):
    rows = chunks_ref.shape[1] // d
    for r in range(d):
        for cc in range(chunks_ref.shape[0]):
            chunks_ref[cc, pl.ds(r, rows, stride=d), :] = in_ref[r, :, cc * 128:(cc + 1) * 128]


def _hbm_specs(n):
    return [pl.BlockSpec(memory_space=pl.ANY)] * n


def _position():
    x, y, c = lax.axis_index("x"), lax.axis_index("y"), lax.axis_index("c")
    return x, y, c


def _gather_stages(ins, outs, send_sems, recv_sems, local_sems):
    n = len(ins)

    def plan():
        x, y, c = _position()
        j = 2 * x + y
        me, sibling = (x, y, c), (x, y, 1 - c)
        chips = [(1 - x, y), (x, 1 - y), (1 - x, 1 - y)]

        def copy(i, k, src, dst, to):
            return pltpu.make_async_remote_copy(
                src_ref=src, dst_ref=dst, send_sem=send_sems.at[i, k], recv_sem=recv_sems.at[i, k],
                device_id=to, device_id_type=MESH)

        local = [pltpu.make_async_copy(ins[i], outs[i].at[j], local_sems.at[i]) for i in range(n)]
        first, landing, passed, passed_landing = [], [], [], []
        for i in range(n):
            for k, chip in enumerate(chips):
                pj = 2 * chip[0] + chip[1]
                first.append(copy(i, k, ins[i].at[c], outs[i].at[j, c], (*chip, c)))
                landing.append(copy(i, k, ins[i].at[c], outs[i].at[pj, c], me))
                passed.append(copy(i, 3 + k, outs[i].at[pj, c], outs[i].at[pj, c], sibling))
                passed_landing.append(copy(i, 3 + k, ins[i].at[c], outs[i].at[pj, 1 - c], me))
        return local, first, landing, passed, passed_landing

    def start():
        local, first, _, _, _ = plan()
        for cp in local + first:
            cp.start()

    def forward():
        _, _, landing, passed, _ = plan()
        for arrived, cp in zip(landing, passed):
            arrived.wait_recv()
            cp.start()

    def finish():
        local, first, _, passed, passed_landing = plan()
        for cp in passed_landing:
            cp.wait_recv()
        for cp in first + passed:
            cp.wait_send()
        for cp in local:
            cp.wait()

    return start, forward, finish


def _gather_scratch(n):
    return [pltpu.SemaphoreType.DMA((n, 6)), pltpu.SemaphoreType.DMA((n, 6)), pltpu.SemaphoreType.DMA((n,))]


def _all_gather_chips(shards):
    n = len(shards)

    def body(*refs):
        for stage in _gather_stages(refs[:n], refs[n:2 * n], *refs[2 * n:]):
            stage()

    return pl.pallas_call(
        body, name="all_gather_chips",
        out_shape=[SDS((N_CHIPS,) + s.shape, s.dtype) for s in shards],
        in_specs=_hbm_specs(n), out_specs=_hbm_specs(n), scratch_shapes=_gather_scratch(n),
    )(*shards)


def _pair_exchange(grads):
    n = len(grads)

    def body(*refs):
        ins, outs = refs[:n], refs[n:2 * n]
        send_sems, recv_sems = refs[2 * n:]
        x, y, c = _position()
        sibling = (x, y, 1 - c)
        remote = []
        for i in range(n):
            for j in range(N_CHIPS):
                remote.append(pltpu.make_async_remote_copy(
                    src_ref=ins[i].at[j, 1 - c], dst_ref=outs[i].at[j],
                    send_sem=send_sems.at[i, j], recv_sem=recv_sems.at[i, j],
                    device_id=sibling, device_id_type=MESH))
        for cp in remote:
            cp.start()
        for cp in remote:
            cp.wait_recv()
        for cp in remote:
            cp.wait_send()

    return pl.pallas_call(
        body, name="grad_pair_exchange",
        out_shape=[SDS((N_CHIPS,) + g.shape[2:], g.dtype) for g in grads],
        in_specs=_hbm_specs(n), out_specs=_hbm_specs(n),
        scratch_shapes=[pltpu.SemaphoreType.DMA((n, N_CHIPS)), pltpu.SemaphoreType.DMA((n, N_CHIPS))],
    )(*grads)


def _chip_scatter(parts):
    n = len(parts)

    def body(*refs):
        ins, outs = refs[:n], refs[n:2 * n]
        send_sems, recv_sems = refs[2 * n:]
        x, y, c = _position()
        chips = [(1 - x, y), (x, 1 - y), (1 - x, 1 - y)]
        remote = []
        for i in range(n):
            for k, chip in enumerate(chips):
                remote.append(pltpu.make_async_remote_copy(
                    src_ref=ins[i].at[2 * chip[0] + chip[1]], dst_ref=outs[i].at[k],
                    send_sem=send_sems.at[i, k], recv_sem=recv_sems.at[i, k],
                    device_id=(*chip, c), device_id_type=MESH))
        for cp in remote:
            cp.start()
        for cp in remote:
            cp.wait_recv()
        for cp in remote:
            cp.wait_send()

    return pl.pallas_call(
        body, name="grad_chip_scatter",
        out_shape=[SDS((3,) + p.shape[1:], p.dtype) for p in parts],
        in_specs=_hbm_specs(n), out_specs=_hbm_specs(n),
        scratch_shapes=[pltpu.SemaphoreType.DMA((n, 3)), pltpu.SemaphoreType.DMA((n, 3))],
    )(*parts)


def _pair_share(halves):
    n = len(halves)

    def body(*refs):
        ins, outs = refs[:n], refs[n:2 * n]
        send_sems, recv_sems = refs[2 * n:]
        x, y, c = _position()
        remote = [pltpu.make_async_remote_copy(
            src_ref=ins[i], dst_ref=outs[i], send_sem=send_sems.at[i], recv_sem=recv_sems.at[i],
            device_id=(x, y, 1 - c), device_id_type=MESH) for i in range(n)]
        for cp in remote:
            cp.start()
        for cp in remote:
            cp.wait_recv()
        for cp in remote:
            cp.wait_send()

    return pl.pallas_call(
        body, name="grad_pair_share",
        out_shape=[SDS(h.shape, h.dtype) for h in halves],
        in_specs=_hbm_specs(n), out_specs=_hbm_specs(n),
        scratch_shapes=[pltpu.SemaphoreType.DMA((n,)), pltpu.SemaphoreType.DMA((n,))],
    )(*halves)


def _gather_all_devices(slab):
    def body(in_ref, out_ref, send_sems, recv_sems, local_sem):
        x, y, c = _position()
        me = 4 * x + 2 * y + c
        local = pltpu.make_async_copy(in_ref, out_ref.at[me], local_sem)
        local.start()
        remote, landing = [], []
        for mask in range(1, N_DEV):
            px, py, pc = x ^ (mask >> 2), y ^ ((mask >> 1) & 1), c ^ (mask & 1)
            peer = 4 * px + 2 * py + pc
            remote.append(pltpu.make_async_remote_copy(
                src_ref=in_ref, dst_ref=out_ref.at[me], send_sem=send_sems.at[mask - 1],
                recv_sem=recv_sems.at[mask - 1], device_id=(px, py, pc), device_id_type=MESH))
            landing.append(pltpu.make_async_remote_copy(
                src_ref=in_ref, dst_ref=out_ref.at[peer], send_sem=send_sems.at[mask - 1],
                recv_sem=recv_sems.at[mask - 1], device_id=(px, py, pc), device_id_type=MESH))
        for cp in remote:
            cp.start()
        for cp in landing:
            cp.wait_recv()
        for cp in remote:
            cp.wait_send()
        local.wait()

    return pl.pallas_call(
        body, name="small_grad_gather",
        out_shape=SDS((N_DEV,) + slab.shape, slab.dtype),
        in_specs=_hbm_specs(1), out_specs=pl.BlockSpec(memory_space=pl.ANY),
        scratch_shapes=[pltpu.SemaphoreType.DMA((N_DEV - 1,)), pltpu.SemaphoreType.DMA((N_DEV - 1,)),
                        pltpu.SemaphoreType.DMA],
    )(slab)


def _row_splits(arrays):
    return min(a.shape[-2] for a in arrays) // 16


def _pair_sum(grads, recvd, core):
    n = len(grads)
    splits = _row_splits(recvd)

    def body(core_ref, *refs):
        for i in range(n):
            s = refs[i][...] + refs[n + i][...]
            refs[2 * n + i][...] = s
            refs[3 * n + i][...] = s.astype(BF16)

    mine = [pl.BlockSpec((N_CHIPS, None, r.shape[1] // splits, r.shape[2]), lambda s, core: (0, core[0], s, 0))
            for r in recvd]
    block = [pl.BlockSpec((N_CHIPS, r.shape[1] // splits, r.shape[2]), lambda s, core: (0, s, 0)) for r in recvd]
    outs = pl.pallas_call(
        body, name="grad_pair_sum",
        grid_spec=pltpu.PrefetchScalarGridSpec(
            num_scalar_prefetch=1, grid=(splits,), in_specs=mine + block, out_specs=block + block),
        out_shape=[SDS(r.shape, F32) for r in recvd] + [SDS(r.shape, BF16) for r in recvd],
        compiler_params=_params(("parallel",)),
    )(core, *grads, *recvd)
    return outs[:n], outs[n:]


def _chip_sum(parts, landed, chip):
    n = len(parts)
    splits = _row_splits(landed)

    def body(chip_ref, *refs):
        for i in range(n):
            acc = refs[i][...]
            for k in range(3):
                acc = acc + refs[n + i][k].astype(F32)
            refs[2 * n + i][...] = acc

    rows = lambda p: p.shape[1] // splits
    return pl.pallas_call(
        body, name="grad_chip_sum",
        grid_spec=pltpu.PrefetchScalarGridSpec(
            num_scalar_prefetch=1, grid=(splits,),
            in_specs=[pl.BlockSpec((None, rows(p), p.shape[2]), lambda s, chip: (chip[0], s, 0)) for p in parts]
            + [pl.BlockSpec((3, rows(p), p.shape[2]), lambda s, chip: (0, s, 0)) for p in parts],
            out_specs=[pl.BlockSpec((rows(p), p.shape[2]), lambda s, chip: (s, 0)) for p in parts]),
        out_shape=[SDS(p.shape[1:], F32) for p in parts],
        compiler_params=_params(("parallel",)),
    )(chip, *parts, *landed)


def _adamw_math(w, g, m, v):
    m = ADAM_B1 * m + (1.0 - ADAM_B1) * g
    v = ADAM_B2 * v + (1.0 - ADAM_B2) * (g * g)
    m_hat = m / (1.0 - ADAM_B1 ** ADAM_STEP)
    v_hat = v / (1.0 - ADAM_B2 ** ADAM_STEP)
    delta = -ADAM_LR * (m_hat / (jnp.sqrt(v_hat) + ADAM_EPS) + ADAM_WD * w)
    return delta, m, v


def _adamw(name, ws, gs, ms, vs, splits):
    n = len(ws)

    def body(*refs):
        for i in range(n):
            w, g, m, v = (refs[q * n + i][...] for q in range(4))
            delta, m, v = _adamw_math(w, g, m, v)
            refs[4 * n + i][...] = delta
            refs[5 * n + i][...] = m
            refs[6 * n + i][...] = v

    def spec(a):
        if splits == 1:
            return pl.BlockSpec(a.shape, lambda s: (0, 0))
        return pl.BlockSpec((a.shape[0] // splits, a.shape[1]), lambda s: (s, 0))

    specs = [spec(a) for a in ws]
    outs = pl.pallas_call(
        body, name=name, grid=(splits,),
        in_specs=specs * 4, out_specs=specs * 3,
        out_shape=[SDS(a.shape, F32) for a in ws] * 3,
        compiler_params=_params(("parallel",)),
    )(*ws, *gs, *ms, *vs)
    return outs[:n], outs[n:2 * n], outs[2 * n:]


def _adamw_halves(ws, own, other, ms, vs, core):
    n = len(ws)
    splits = _row_splits(own)

    def body(core_ref, *refs):
        mine = pl.program_id(0) == core_ref[0]
        for i in range(n):
            g = jnp.where(mine, refs[n + i][...], refs[2 * n + i][...])
            delta, m, v = _adamw_math(refs[i][...], g, refs[3 * n + i][...], refs[4 * n + i][...])
            refs[5 * n + i][...] = g
            refs[6 * n + i][...] = delta
            refs[7 * n + i][...] = m
            refs[8 * n + i][...] = v

    rows = lambda a: a.shape[0] // splits
    half = [pl.BlockSpec((None, rows(a), a.shape[1]), lambda hh, s, core: (hh, s, 0)) for a in own]
    flat = [pl.BlockSpec((rows(a), a.shape[1]), lambda hh, s, core: (s, 0)) for a in own]
    outs = pl.pallas_call(
        body, name="adamw_big",
        grid_spec=pltpu.PrefetchScalarGridSpec(
            num_scalar_prefetch=1, grid=(2, splits), in_specs=half + flat + flat + half + half, out_specs=half * 4),
        out_shape=[SDS(w.shape, F32) for w in ws] * 4,
        compiler_params=_params(("parallel", "parallel")),
    )(core, *ws, *own, *other, *ms, *vs)
    return outs[:n], outs[n:2 * n], outs[2 * n:3 * n], outs[3 * n:]


def _sum_devices(slabs):
    def body(in_ref, out_ref):
        acc = in_ref[0]
        for k in range(1, N_DEV):
            acc = acc + in_ref[k]
        out_ref[...] = acc

    return pl.pallas_call(
        body, name="small_grad_sum", out_shape=SDS(slabs.shape[1:], F32),
    )(slabs)


def _mm_nn(name, a, w, out_dtype, tm, scale_first_tile=None):
    S, K = a.shape
    N = w.shape[1]
    tn = K

    def body(a_ref, w_ref, o_ref):
        acc = _dot(a_ref[...], w_ref[...])
        if scale_first_tile is not None:
            acc = acc * jnp.where(pl.program_id(1) == 0, scale_first_tile, 1.0)
        o_ref[...] = acc.astype(out_dtype)

    return pl.pallas_call(
        body, name=name, grid=(S // tm, N // tn),
        in_specs=[pl.BlockSpec((tm, K), lambda i, t: (i, 0)), pl.BlockSpec((K, tn), lambda i, t: (0, t))],
        out_specs=pl.BlockSpec((tm, tn), lambda i, t: (i, t)),
        out_shape=SDS((S, N), out_dtype),
        compiler_params=_params(("parallel", "arbitrary")),
    )(a, w)


def _mm_tn(name, a, b, tk):
    S, M = a.shape
    N = b.shape[1]
    tn = M

    def body(a_ref, b_ref, o_ref):
        @pl.when(pl.program_id(1) == 0)
        def _():
            o_ref[...] = jnp.zeros_like(o_ref)
        o_ref[...] += _tdot(a_ref[...], b_ref[...])

    return pl.pallas_call(
        body, name=name, grid=(N // tn, S // tk),
        in_specs=[pl.BlockSpec((tk, M), lambda t, k: (k, 0)), pl.BlockSpec((tk, tn), lambda t, k: (k, t))],
        out_specs=pl.BlockSpec((M, tn), lambda t, k: (0, t)),
        out_shape=SDS((M, N), F32),
        compiler_params=_params(("parallel", "arbitrary")),
    )(a, b)


def _a_in_proj(x, w4, b_in, later_shards, tm):
    S, D = x.shape
    nj = w4.shape[2]
    n = len(later_shards)
    steps = S // tm

    def body(x_ref, w_ref, b_ref, *refs):
        shard_refs, (h_ref, xb_ref), gathered_refs = refs[:n], refs[n:n + 2], refs[n + 2:2 * n + 2]
        start, forward, finish = _gather_stages(shard_refs, gathered_refs, *refs[2 * n + 2:])
        i, t = pl.program_id(0), pl.program_id(1)
        pl.when((i == 0) & (t == 0))(start)
        pl.when((i == steps // 2) & (t == 0))(forward)
        xb = x_ref[...].astype(BF16)

        @pl.when(t == 0)
        def _():
            xb_ref[...] = xb
        h_ref[...] = _dot(xb, w_ref[...]) + b_ref[...]
        pl.when((i == steps - 1) & (t == N_CHIPS - 1))(finish)

    outs = pl.pallas_call(
        body, name="a_in_proj", grid=(steps, N_CHIPS),
        in_specs=[pl.BlockSpec((tm, D), lambda i, t: (i, 0)),
                  pl.BlockSpec((None, D, nj), lambda i, t: (t, 0, 0)),
                  pl.BlockSpec((1, nj), lambda i, t: (0, t))] + _hbm_specs(n),
        out_specs=[pl.BlockSpec((tm, nj), lambda i, t: (i, t)), pl.BlockSpec((tm, D), lambda i, t: (i, 0))]
        + _hbm_specs(n),
        out_shape=[SDS((S, N_CHIPS * nj), F32), SDS((S, D), BF16)]
        + [SDS((N_CHIPS,) + s.shape, s.dtype) for s in later_shards],
        scratch_shapes=_gather_scratch(n),
        compiler_params=_params(("arbitrary", "arbitrary")),
    )(x, w4, b_in, *later_shards)
    return outs[0], outs[1], outs[2:]


def _fill_glu_ext(ext_ref, a_ref, g_ref, ah_ref, gh_ref, has_prev):
    u0h = ah_ref[...] * _sigmoid(gh_ref[...])
    ext_ref[0:HALO, :] = jnp.where(has_prev, u0h, 0.0)
    ext_ref[HALO:, :] = a_ref[...] * _sigmoid(g_ref[...])


def _fill_shifts(shift_ref, ext_ref):
    for s in range(1, 8):
        shift_ref[s - 1] = ext_ref[s:s + shift_ref.shape[1], :]


def _tap_windows(ext_ref, shift_ref, starts, r0, rows, lanes):
    for s in range(8):
        taps = [(k, st) for k, st in enumerate(starts) if st % 8 == s]
        if not taps:
            continue
        lo = min(st for _, st in taps)
        hi = max(st for _, st in taps)
        if s == 0:
            win = ext_ref[r0 + lo:r0 + hi + rows, lanes]
        else:
            win = shift_ref[s - 1, r0 + lo - s:r0 + hi - s + rows, lanes]
        for k, st in taps:
            yield k, win[st - lo:st - lo + rows]


def _a_conv_out(h, x, wdw, bdw, lng, lnb, wout, bout, pg, pb, tm):
    S, D = x.shape
    hb = tm // HALO
    d1, d2 = DILATIONS[1], DILATIONS[2]

    def body(a_ref, g_ref, z_ref, ah_ref, gh_ref, x_ref, wdw_ref, bdw_ref, lng_ref, lnb_ref, wout_ref,
             bout_ref, pg_ref, pb_ref, xhu_ref, rsu_ref, vb_ref, xh1_ref, rs1_ref, x1b_ref, x1p1_ref,
             x1p2_ref, ext_ref, u1_ref, x1_ref, sh_ref):
        i = pl.program_id(0)
        _fill_glu_ext(ext_ref, a_ref, g_ref, ah_ref, gh_ref, i > 0)
        _fill_shifts(sh_ref, ext_ref)
        starts = [HALO - (CONV_WIDTH - 1) + k for k in range(CONV_WIDTH)]
        for cc in range(D // 128):
            lanes = slice(cc * 128, (cc + 1) * 128)
            for r0 in range(0, tm, CONV_ROWS):
                acc = jnp.broadcast_to(bdw_ref[:, lanes], (CONV_ROWS, 128))
                for k, win in _tap_windows(ext_ref, sh_ref, starts, r0, CONV_ROWS, lanes):
                    acc = acc + wdw_ref[k:k + 1, lanes] * win
                u1_ref[r0:r0 + CONV_ROWS, lanes] = acc
        xhu, rsu = _ln_fwd(u1_ref[...])
        xhu_ref[...] = xhu
        rsu_ref[...] = rsu
        u2 = xhu * lng_ref[...] + lnb_ref[...]
        z = z_ref[...]
        v = (u2 * _sigmoid(u2)) * (z * _sigmoid(z))
        vb = v.astype(BF16)
        vb_ref[...] = vb
        s1 = ALPHA * x_ref[...] + _dot(vb, wout_ref[...]) + bout_ref[...]
        xh1, rs1 = _ln_fwd(s1)
        xh1_ref[...] = xh1
        rs1_ref[...] = rs1
        x1 = xh1 * pg_ref[...] + pb_ref[...]
        x1b_ref[...] = x1.astype(BF16)
        _to_chunks(x1_ref, x1)
        _deinterleave(x1_ref, x1p1_ref, d1, BF16)
        _deinterleave(x1_ref, x1p2_ref, d2, BF16)

    tile = lambda c: pl.BlockSpec((tm, D), lambda i, c=c: (i, c))
    halo = lambda c: pl.BlockSpec((HALO, D), lambda i, c=c: (jnp.maximum(i * hb - 1, 0), c))
    row = pl.BlockSpec((1, D), lambda i: (0, 0))
    stat = pl.BlockSpec((tm, 1), lambda i: (i, 0))
    return pl.pallas_call(
        body, name="a_conv_out", grid=(S // tm,),
        in_specs=[tile(0), tile(1), tile(2), halo(0), halo(1), tile(0),
                  pl.BlockSpec((HALO, D), lambda i: (0, 0)), row, row, row,
                  pl.BlockSpec((D, D), lambda i: (0, 0)), row, row, row],
        out_specs=[tile(0), stat, tile(0), tile(0), stat, tile(0),
                   pl.BlockSpec((d1, tm // d1, D), lambda i: (0, i, 0)),
                   pl.BlockSpec((d2, tm // d2, D), lambda i: (0, i, 0))],
        out_shape=[SDS((S, D), F32), SDS((S, 1), F32), SDS((S, D), BF16), SDS((S, D), F32), SDS((S, 1), F32),
                   SDS((S, D), BF16), SDS((d1, S // d1, D), BF16), SDS((d2, S // d2, D), BF16)],
        scratch_shapes=[pltpu.VMEM((HALO + tm, D), F32), pltpu.VMEM((tm, D), F32),
                        pltpu.VMEM((D // 128, tm, 128), F32), pltpu.VMEM((7, HALO + tm - 8, D), F32)],
        compiler_params=_params(("parallel",)),
    )(h, h, h, h, h, x, wdw, bdw, lng, lnb, wout, bout, pg, pb)


def _band(n, dilation):
    qi = lax.broadcasted_iota(jnp.int32, (BLOCK, 2 * BLOCK), 0)
    kj = lax.broadcasted_iota(jnp.int32, (BLOCK, 2 * BLOCK), 1)
    dist = qi + BLOCK - kj
    valid = (dist >= 0) & (dist <= BLOCK) & ((n > 0) | (kj >= BLOCK))
    return jnp.where(valid, dist.astype(F32) * float(-dilation), -jnp.inf)


def _attn_fwd(g, qkv, D):
    S = qkv.shape[0]
    d = DILATIONS[g]
    nb = S // (d * BLOCK)
    H = D // HEAD_DIM
    slopes = _slopes(H)

    def body(q_ref, kp_ref, kc_ref, vp_ref, vc_ref, o_ref, lse_ref):
        neg_dist = _band(pl.program_id(1), d)
        lane = lax.broadcasted_iota(jnp.int32, (BLOCK, LSE_LANES), 1)
        low = lane < HEAD_DIM
        lse = jnp.zeros((BLOCK, LSE_LANES), F32)
        for hp in range(H // 2):
            sl = slice(hp * 128, (hp + 1) * 128)
            q = q_ref[:, sl]
            k = jnp.concatenate([kp_ref[:, sl], kc_ref[:, sl]], axis=0)
            v = jnp.concatenate([vp_ref[:, sl], vc_ref[:, sl]], axis=0)
            o = []
            for a in range(2):
                h = 2 * hp + a
                s = _dot_t(jnp.where(low if a == 0 else ~low, q, jnp.zeros_like(q)), k)
                s = s + slopes[h] * neg_dist
                m = jnp.max(s, axis=1, keepdims=True)
                p = jnp.exp(s - m)
                l = jnp.sum(p, axis=1, keepdims=True)
                o.append(_dot(p.astype(BF16), v) * (1.0 / l))
                lse = jnp.where(lane == h, m + jnp.log(l), lse)
            o_ref[:, sl] = jnp.where(low, o[0], o[1])
        lse_ref[...] = lse

    cur = lambda c: pl.BlockSpec((BLOCK, D), lambda r, n, c=c: (r * nb + n, c))
    prev = lambda c: pl.BlockSpec((BLOCK, D), lambda r, n, c=c: (r * nb + jnp.maximum(n - 1, 0), c))
    return pl.pallas_call(
        body, name="attn_fwd_g%d" % g, grid=(d, nb),
        in_specs=[cur(0), prev(1), cur(1), prev(2), cur(2)],
        out_specs=[cur(0), pl.BlockSpec((BLOCK, LSE_LANES), lambda r, n: (r * nb + n, 0))],
        out_shape=[SDS((S, D), F32), SDS((S, LSE_LANES), F32)],
        compiler_params=_params(("parallel", "parallel")),
    )(qkv, qkv, qkv, qkv, qkv)


def _attn_bwd(g, qkv, do, lse, delta, D):
    S = qkv.shape[0]
    d = DILATIONS[g]
    nb = S // (d * BLOCK)
    H = D // HEAD_DIM
    slopes = _slopes(H)

    def body(q_ref, kp_ref, kc_ref, vp_ref, vc_ref, do_ref, lse_ref, dl_ref, dq_ref, dkv_ref, ck_ref, cv_ref):
        n = pl.program_id(1)

        @pl.when(n == 0)
        def _():
            ck_ref[...] = jnp.zeros_like(ck_ref)
            cv_ref[...] = jnp.zeros_like(cv_ref)

        @pl.when(n < nb)
        def _():
            neg_dist = _band(n, d)
            low = lax.broadcasted_iota(jnp.int32, (BLOCK, 128), 1) < HEAD_DIM
            low2 = lax.broadcasted_iota(jnp.int32, (2 * BLOCK, 128), 1) < HEAD_DIM
            for hp in range(H // 2):
                sl = slice(hp * 128, (hp + 1) * 128)
                q = q_ref[:, sl]
                do2 = do_ref[:, sl]
                k = jnp.concatenate([kp_ref[:, sl], kc_ref[:, sl]], axis=0)
                v = jnp.concatenate([vp_ref[:, sl], vc_ref[:, sl]], axis=0)
                zero = jnp.zeros_like(q)
                q_do = jnp.concatenate([jnp.concatenate([q, zero], axis=1),
                                        jnp.concatenate([zero, do2], axis=1)], axis=0)
                dq, dkv = [], []
                for a in range(2):
                    h = 2 * hp + a
                    keep = low if a == 0 else ~low
                    s = _dot_t(jnp.where(keep, q, zero), k)
                    s = s + slopes[h] * neg_dist
                    p = jnp.exp(s - lse_ref[:, h:h + 1])
                    dp = _dot_t(jnp.where(keep, do2, zero), v)
                    dsb = (p * (dp - dl_ref[:, h:h + 1])).astype(BF16)
                    dq.append(_dot(dsb, k))
                    dkv.append(_tdot(jnp.concatenate([dsb, p.astype(BF16)], axis=0), q_do))
                dq_ref[:, sl] = (jnp.where(low, dq[0], dq[1]) * (HEAD_DIM ** -0.5)).astype(BF16)
                dk2 = jnp.where(low2, dkv[0][:, :128], dkv[1][:, :128])
                dv2 = jnp.where(low2, dkv[0][:, 128:], dkv[1][:, 128:])
                dkv_ref[:, sl] = (ck_ref[:, sl] + dk2[:BLOCK]).astype(BF16)
                dkv_ref[:, D + hp * 128:D + (hp + 1) * 128] = (cv_ref[:, sl] + dv2[:BLOCK]).astype(BF16)
                ck_ref[:, sl] = dk2[BLOCK:]
                cv_ref[:, sl] = dv2[BLOCK:]

        @pl.when(n == nb)
        def _():
            dkv_ref[:, :D] = ck_ref[...].astype(BF16)
            dkv_ref[:, D:] = cv_ref[...].astype(BF16)

    nq = lambda n: jnp.minimum(n, nb - 1)
    cur = lambda c: pl.BlockSpec((BLOCK, D), lambda r, n, c=c: (r * nb + nq(n), c))
    prev = lambda c: pl.BlockSpec((BLOCK, D), lambda r, n, c=c: (r * nb + jnp.maximum(nq(n) - 1, 0), c))
    stat = pl.BlockSpec((BLOCK, LSE_LANES), lambda r, n: (r * nb + nq(n), 0))
    return pl.pallas_call(
        body, name="attn_bwd_g%d" % g, grid=(d, nb + 1),
        in_specs=[cur(0), prev(1), cur(1), prev(2), cur(2), cur(0), stat, stat],
        out_specs=[cur(0), pl.BlockSpec((BLOCK, 2 * D), lambda r, n: (r * nb + jnp.maximum(n - 1, 0), 0))],
        out_shape=[SDS((S, D), BF16), SDS((S, 2 * D), BF16)],
        scratch_shapes=[pltpu.VMEM((BLOCK, D), F32), pltpu.VMEM((BLOCK, D), F32)],
        compiler_params=_params(("parallel", "arbitrary")),
    )(qkv, qkv, qkv, qkv, qkv, do, lse, delta)


def _b_merge_out_loss(o0, o1, o2, l0, l1, l2, z2, xh1, target, wbo, bbo, pg0, pb0, pg1, pb1, tm):
    S, D = o0.shape
    H = D // HEAD_DIM
    d1, d2 = DILATIONS[1], DILATIONS[2]
    inv_d = 1.0 / D

    def body(o0_ref, o1_ref, o2_ref, l0_ref, l1_ref, l2_ref, z_ref, xh1_ref, t_ref, wbo_ref, bbo_ref,
             pg0_ref, pb0_ref, pg1_ref, pb1_ref,
             v2b_ref, ds2_ref, ds2b_ref, dz2b_ref, da0_ref, da1_ref, da2_ref, ls0_ref, ls1_ref, ls2_ref,
             dl0_ref, dl1_ref, dl2_ref, loss_ref, sums_ref,
             o1n_ref, o2n_ref, l1n_ref, l2n_ref, att_ref, st_ref):
        i = pl.program_id(0)
        _interleave(o1_ref, o1n_ref, d1)
        _interleave(o2_ref, o2n_ref, d2)
        for r in range(d1):
            l1n_ref[pl.ds(r, tm // d1, stride=d1), :] = l1_ref[r]
        for r in range(d2):
            l2n_ref[pl.ds(r, tm // d2, stride=d2), :] = l2_ref[r]
        la, lb, lc = l0_ref[...], l1n_ref[...], l2n_ref[...]
        m = jnp.maximum(jnp.maximum(la, lb), lc)
        ea, eb, ec = jnp.exp(la - m), jnp.exp(lb - m), jnp.exp(lc - m)
        den = ea + eb + ec
        wa, wb, wc = ea / den, eb / den, ec / den
        lse = m + jnp.log(den)
        for h in range(H):
            sl = slice(h * HEAD_DIM, (h + 1) * HEAD_DIM)
            cc, hl = divmod(h * HEAD_DIM, 128)
            att_ref[:, sl] = (wa[:, h:h + 1] * o0_ref[:, sl] + wb[:, h:h + 1] * o1n_ref[cc, :, hl:hl + HEAD_DIM]
                              + wc[:, h:h + 1] * o2n_ref[cc, :, hl:hl + HEAD_DIM])
        att = att_ref[...]
        z = z_ref[...]
        sz = _sigmoid(z)
        gate = z * sz
        v2b = (att * gate).astype(BF16)
        v2b_ref[...] = v2b
        x1 = xh1_ref[...] * pg0_ref[...] + pb0_ref[...]
        s2 = ALPHA * x1 + _dot(v2b, wbo_ref[...]) + bbo_ref[...]
        xh2, rs2 = _ln_fwd(s2)
        err = xh2 * pg1_ref[...] + pb1_ref[...] - t_ref[...]
        dy = err * inv_d
        ds2 = _ln_bwd(dy * pg1_ref[...], xh2, rs2)
        ds2b = ds2.astype(BF16)
        ds2_ref[...] = ds2
        ds2b_ref[...] = ds2b

        @pl.when(i == 0)
        def _():
            loss_ref[...] = jnp.zeros_like(loss_ref)
            sums_ref[...] = jnp.zeros_like(sums_ref)
        loss_ref[...] += 0.5 * inv_d * jnp.sum(err * err)
        sums_ref[0:1, :] += _colsum(dy * xh2)
        sums_ref[1:2, :] += _colsum(dy)
        sums_ref[2:3, :] += _colsum(ds2)

        dv2 = _dot_t(ds2b, wbo_ref[...])
        datt = dv2 * gate
        dz2b_ref[...] = (dv2 * att * _silu_grad(z, sz)).astype(BF16)
        prod = datt * att
        lane = lax.broadcasted_iota(jnp.int32, (tm, LSE_LANES), 1)
        dl = jnp.zeros((tm, LSE_LANES), F32)
        for h in range(H):
            sl = slice(h * HEAD_DIM, (h + 1) * HEAD_DIM)
            dl = jnp.where(lane == h, jnp.sum(prod[:, sl], axis=1, keepdims=True), dl)
        da0_ref[...] = datt.astype(BF16)
        ls0_ref[...] = lse
        dl0_ref[...] = dl
        _to_chunks(o1n_ref, datt)
        _deinterleave(o1n_ref, da1_ref, d1, BF16)
        _deinterleave(o1n_ref, da2_ref, d2, BF16)
        st_ref[...] = lse
        for r in range(d1):
            ls1_ref[r] = st_ref[pl.ds(r, tm // d1, stride=d1), :]
        for r in range(d2):
            ls2_ref[r] = st_ref[pl.ds(r, tm // d2, stride=d2), :]
        st_ref[...] = dl
        for r in range(d1):
            dl1_ref[r] = st_ref[pl.ds(r, tm // d1, stride=d1), :]
        for r in range(d2):
            dl2_ref[r] = st_ref[pl.ds(r, tm // d2, stride=d2), :]

    tile = pl.BlockSpec((tm, D), lambda i: (i, 0))
    stat = pl.BlockSpec((tm, LSE_LANES), lambda i: (i, 0))
    perm = lambda d, w: pl.BlockSpec((d, tm // d, w), lambda i: (0, i, 0))
    row = pl.BlockSpec((1, D), lambda i: (0, 0))
    acc = lambda w: pl.BlockSpec((8, w), lambda i: (0, 0))
    pshape = lambda d, w, dt: SDS((d, S // d, w), dt)
    return pl.pallas_call(
        body, name="b_merge_out_loss", grid=(S // tm,),
        in_specs=[tile, perm(d1, D), perm(d2, D), stat, perm(d1, LSE_LANES), perm(d2, LSE_LANES),
                  tile, tile, tile, pl.BlockSpec((D, D), lambda i: (0, 0)), row, row, row, row, row],
        out_specs=[tile, tile, tile, tile, tile, perm(d1, D), perm(d2, D),
                   stat, perm(d1, LSE_LANES), perm(d2, LSE_LANES),
                   stat, perm(d1, LSE_LANES), perm(d2, LSE_LANES), acc(LSE_LANES), acc(D)],
        out_shape=[SDS((S, D), BF16), SDS((S, D), F32), SDS((S, D), BF16), SDS((S, D), BF16),
                   SDS((S, D), BF16), pshape(d1, D, BF16), pshape(d2, D, BF16),
                   SDS((S, LSE_LANES), F32), pshape(d1, LSE_LANES, F32), pshape(d2, LSE_LANES, F32),
                   SDS((S, LSE_LANES), F32), pshape(d1, LSE_LANES, F32), pshape(d2, LSE_LANES, F32),
                   SDS((8, LSE_LANES), F32), SDS((8, D), F32)],
        scratch_shapes=[pltpu.VMEM((D // 128, tm, 128), F32), pltpu.VMEM((D // 128, tm, 128), F32),
                        pltpu.VMEM((tm, LSE_LANES), F32), pltpu.VMEM((tm, LSE_LANES), F32),
                        pltpu.VMEM((tm, D), F32), pltpu.VMEM((tm, LSE_LANES), F32)],
        compiler_params=_params(("arbitrary",)),
    )(o0, o1, o2, l0, l1, l2, z2, xh1, target, wbo, bbo, pg0, pb0, pg1, pb1)


def _b_dx1_ln1_bwd(ds2, dz2b, dq, dkv, xh1, rs1, wz, wg, pg0, tm):
    S, D = ds2.shape
    d1, d2 = DILATIONS[1], DILATIONS[2]

    def group_part(dq_blk, dkv_blk, w_ref):
        return (_dot_t(dq_blk, w_ref[:, 0:D]) + _dot_t(dkv_blk[:, 0:D], w_ref[:, D:2 * D])
                + _dot_t(dkv_blk[:, D:2 * D], w_ref[:, 2 * D:3 * D]))

    def body(ds2_ref, dz_ref, dq0_ref, dkv0_ref, dq1_ref, dkv1_ref, dq2_ref, dkv2_ref, xh1_ref, rs1_ref,
             wz_ref, w0_ref, w1_ref, w2_ref, pg0_ref, ds1_ref, ds1b_ref, sums_ref, acc_ref):
        i = pl.program_id(0)
        _to_chunks(acc_ref, ALPHA * ds2_ref[...] + _dot_t(dz_ref[...], wz_ref[...])
                   + group_part(dq0_ref[...], dkv0_ref[...], w0_ref))
        for d, dq_ref, dkv_ref, w_ref in ((d1, dq1_ref, dkv1_ref, w1_ref), (d2, dq2_ref, dkv2_ref, w2_ref)):
            rows = tm // d
            part = group_part(dq_ref[...].reshape(tm, D), dkv_ref[...].reshape(tm, 2 * D), w_ref)
            for r in range(d):
                idx = pl.ds(r, rows, stride=d)
                for cc in range(D // 128):
                    acc_ref[cc, idx, :] = acc_ref[cc, idx, :] + part[r * rows:(r + 1) * rows, cc * 128:(cc + 1) * 128]
        dx1 = _from_chunks(acc_ref)
        xh1 = xh1_ref[...]
        ds1 = _ln_bwd(dx1 * pg0_ref[...], xh1, rs1_ref[...])
        ds1_ref[...] = ds1
        ds1b_ref[...] = ds1.astype(BF16)

        @pl.when(i == 0)
        def _():
            sums_ref[...] = jnp.zeros_like(sums_ref)
        sums_ref[0:1, :] += _colsum(dx1 * xh1)
        sums_ref[1:2, :] += _colsum(dx1)
        sums_ref[2:3, :] += _colsum(ds1)

    tile = lambda w: pl.BlockSpec((tm, w), lambda i: (i, 0))
    perm = lambda d, w: pl.BlockSpec((d, tm // d, w), lambda i: (0, i, 0))
    whole = pl.BlockSpec(memory_space=pltpu.VMEM)
    return pl.pallas_call(
        body, name="b_dx1_ln1_bwd", grid=(S // tm,),
        in_specs=[tile(D), tile(D), tile(D), tile(2 * D), perm(d1, D), perm(d1, 2 * D), perm(d2, D),
                  perm(d2, 2 * D), tile(D), tile(1), whole, whole, whole, whole,
                  pl.BlockSpec((1, D), lambda i: (0, 0))],
        out_specs=[tile(D), tile(D), pl.BlockSpec((8, D), lambda i: (0, 0))],
        out_shape=[SDS((S, D), F32), SDS((S, D), BF16), SDS((8, D), F32)],
        scratch_shapes=[pltpu.VMEM((D // 128, tm, 128), F32)],
        compiler_params=_params(("arbitrary",)),
    )(ds2, dz2b, dq[0], dkv[0], dq[1].reshape(d1, S // d1, D), dkv[1].reshape(d1, S // d1, 2 * D),
      dq[2].reshape(d2, S // d2, D), dkv[2].reshape(d2, S // d2, 2 * D), xh1, rs1, wz, wg[0], wg[1], wg[2], pg0)


def _a_gate_bwd(ds1b, h, xhu, rsu, wout, lng, lnb, tm):
    S, D = xhu.shape

    def body(ds_ref, z_ref, xhu_ref, rsu_ref, w_ref, lng_ref, lnb_ref, du1_ref, dzb_ref, sums_ref):
        i = pl.program_id(0)
        dv = _dot_t(ds_ref[...], w_ref[...])
        xhu = xhu_ref[...]
        u2 = xhu * lng_ref[...] + lnb_ref[...]
        su = _sigmoid(u2)
        z = z_ref[...]
        sz = _sigmoid(z)
        dz = dv * (u2 * su) * _silu_grad(z, sz)
        du2 = dv * (z * sz) * _silu_grad(u2, su)
        du1 = _ln_bwd(du2 * lng_ref[...], xhu, rsu_ref[...])
        du1_ref[...] = du1
        dzb_ref[...] = dz.astype(BF16)

        @pl.when(i == 0)
        def _():
            sums_ref[...] = jnp.zeros_like(sums_ref)
        sums_ref[0:1, :] += _colsum(du2 * xhu)
        sums_ref[1:2, :] += _colsum(du2)
        sums_ref[2:3, :] += _colsum(du1)
        sums_ref[3:4, :] += _colsum(dz)

    tile = pl.BlockSpec((tm, D), lambda i: (i, 0))
    row = pl.BlockSpec((1, D), lambda i: (0, 0))
    return pl.pallas_call(
        body, name="a_gate_bwd", grid=(S // tm,),
        in_specs=[tile, pl.BlockSpec((tm, D), lambda i: (i, 2)), tile, pl.BlockSpec((tm, 1), lambda i: (i, 0)),
                  pl.BlockSpec((D, D), lambda i: (0, 0)), row, row],
        out_specs=[tile, tile, pl.BlockSpec((8, D), lambda i: (0, 0))],
        out_shape=[SDS((S, D), F32), SDS((S, D), BF16), SDS((8, D), F32)],
        compiler_params=_params(("arbitrary",)),
    )(ds1b, h, xhu, rsu, wout, lng, lnb)


def _a_conv_bwd(du1, h, wdw, tm):
    S, D = du1.shape
    hb = tm // HALO
    last_halo = S // HALO - 1
    n_tiles = S // tm

    def body(du_ref, dun_ref, a_ref, g_ref, ah_ref, gh_ref, wdw_ref, dag_ref, sums_ref, wsum_ref,
             dext_ref, ext_ref, dsh_ref, sh_ref, wacc_ref):
        i = pl.program_id(0)

        @pl.when(i == 0)
        def _():
            sums_ref[...] = jnp.zeros_like(sums_ref)
            wsum_ref[...] = jnp.zeros_like(wsum_ref)
            wacc_ref[...] = jnp.zeros_like(wacc_ref)
        dext_ref[0:tm, :] = du_ref[...]
        dext_ref[tm:, :] = jnp.where(i < n_tiles - 1, dun_ref[...], 0.0)
        _fill_shifts(dsh_ref, dext_ref)
        _fill_glu_ext(ext_ref, a_ref, g_ref, ah_ref, gh_ref, i > 0)
        _fill_shifts(sh_ref, ext_ref)
        back = [CONV_WIDTH - 1 - k for k in range(CONV_WIDTH)]
        fwd = [HALO - (CONV_WIDTH - 1) + k for k in range(CONV_WIDTH)]
        for cc in range(D // 128):
            lanes = slice(cc * 128, (cc + 1) * 128)
            hi_lanes = slice(D + cc * 128, D + (cc + 1) * 128)
            sa = jnp.zeros((1, 128), F32)
            sg = jnp.zeros((1, 128), F32)
            for r0 in range(0, tm, CONV_ROWS):
                acc = jnp.zeros((CONV_ROWS, 128), F32)
                for k, win in _tap_windows(dext_ref, dsh_ref, back, r0, CONV_ROWS, lanes):
                    acc = acc + wdw_ref[k:k + 1, lanes] * win
                a = a_ref[r0:r0 + CONV_ROWS, lanes]
                s = _sigmoid(g_ref[r0:r0 + CONV_ROWS, lanes])
                da = acc * s
                dg = acc * a * s * (1.0 - s)
                dag_ref[r0:r0 + CONV_ROWS, lanes] = da.astype(BF16)
                dag_ref[r0:r0 + CONV_ROWS, hi_lanes] = dg.astype(BF16)
                sa = sa + _colsum(da)
                sg = sg + _colsum(dg)
                du = du_ref[r0:r0 + CONV_ROWS, lanes]
                for k, win in _tap_windows(ext_ref, sh_ref, fwd, r0, CONV_ROWS, lanes):
                    p = du * win
                    fold = p[0:8]
                    for q in range(8, CONV_ROWS, 8):
                        fold = fold + p[q:q + 8]
                    wacc_ref[k, :, lanes] += fold
            sums_ref[0:1, lanes] += sa
            sums_ref[1:2, lanes] += sg

        @pl.when(i == n_tiles - 1)
        def _():
            for k in range(CONV_WIDTH):
                wsum_ref[k:k + 1, :] = _colsum(wacc_ref[k])

    tile = lambda c: pl.BlockSpec((tm, D), lambda i, c=c: (i, c))
    halo = lambda c: pl.BlockSpec((HALO, D), lambda i, c=c: (jnp.maximum(i * hb - 1, 0), c))
    return pl.pallas_call(
        body, name="a_conv_bwd", grid=(n_tiles,),
        in_specs=[tile(0), pl.BlockSpec((HALO, D), lambda i: (jnp.minimum((i + 1) * hb, last_halo), 0)),
                  tile(0), tile(1), halo(0), halo(1), pl.BlockSpec((HALO, D), lambda i: (0, 0))],
        out_specs=[pl.BlockSpec((tm, 2 * D), lambda i: (i, 0)), pl.BlockSpec((8, D), lambda i: (0, 0)),
                   pl.BlockSpec((HALO, D), lambda i: (0, 0))],
        out_shape=[SDS((S, 2 * D), BF16), SDS((8, D), F32), SDS((HALO, D), F32)],
        scratch_shapes=[pltpu.VMEM((tm + HALO, D), F32), pltpu.VMEM((HALO + tm, D), F32),
                        pltpu.VMEM((7, HALO + tm - 8, D), F32), pltpu.VMEM((7, HALO + tm - 8, D), F32),
                        pltpu.VMEM((HALO, 8, D), F32)],
        compiler_params=_params(("arbitrary",)),
    )(du1, du1, h, h, h, h, wdw)


def _a_dx(ds1, dag, dzb, w_in, tm):
    S, D = ds1.shape

    def body(ds_ref, dag_ref, dz_ref, w_ref, o_ref):
        o_ref[...] = (ALPHA * ds_ref[...] + _dot_t(dag_ref[...], w_ref[:, 0:2 * D])
                      + _dot_t(dz_ref[...], w_ref[:, 2 * D:3 * D]))

    tile = lambda w: pl.BlockSpec((tm, w), lambda i: (i, 0))
    return pl.pallas_call(
        body, name="a_dx", grid=(S // tm,),
        in_specs=[tile(D), tile(2 * D), tile(D), pl.BlockSpec(memory_space=pltpu.VMEM)],
        out_specs=tile(D), out_shape=SDS((S, D), F32),
        compiler_params=_params(("parallel",)),
    )(ds1, dag, dzb, w_in)


def _halves(w):
    return w.reshape(2, w.shape[0] // 2, w.shape[1])


def _unstack_cols(w4):
    return jnp.transpose(w4, (1, 0, 2)).reshape(w4.shape[1], N_CHIPS * w4.shape[2])


def _stack_cols(w):
    D, n = w.shape
    return jnp.transpose(w.reshape(D, N_CHIPS, n // N_CHIPS), (1, 0, 2))


def _pack_rows(rows, width):
    slab = jnp.concatenate([r.reshape(-1, width) for r in rows], axis=0)
    return jnp.pad(slab, ((0, SMALL_ROWS - slab.shape[0]), (0, 0)))


def kernel(x, a_w_in, a_b_in, a_w_dw, a_b_dw, a_ln_g, a_ln_b, a_w_out, a_b_out, kv_w, b_w_in, b_w_out, b_b_out, post_ln_g, post_ln_b, loss_target, m_a_w_in, m_a_b_in, m_a_w_dw, m_a_b_dw, m_a_ln_g, m_a_ln_b, m_a_w_out, m_a_b_out, m_kv_w, m_b_w_in, m_b_w_out, m_b_b_out, m_post_ln_g, m_post_ln_b, v_a_w_in, v_a_b_in, v_a_w_dw, v_a_b_dw, v_a_ln_g, v_a_ln_b, v_a_w_out, v_a_b_out, v_kv_w, v_b_w_in, v_b_w_out, v_b_b_out, v_post_ln_g, v_post_ln_b):
    S, D = x.shape[1], x.shape[2]
    dq4 = D // N_CHIPS
    tm = 256
    tm_mm = min(S, 1024)
    x2 = x.reshape(S, D)
    target = loss_target.reshape(S, D)
    jchip = 2 * lax.axis_index("x") + lax.axis_index("y")

    big_local = [a_w_in[0], kv_w, b_w_in[0], a_w_out[0], b_w_out[0]]
    small_local = _pack_rows([a_b_in.reshape(3, dq4), jnp.pad(a_w_dw[0], ((0, 1), (0, 0))), a_b_dw, a_ln_g,
                              a_ln_b, a_b_out], dq4)
    wire = [_halves(w.astype(BF16)) for w in big_local]
    whole = lambda g: g.reshape((N_CHIPS, 2 * g.shape[2], g.shape[3]))
    gathered = _all_gather_chips([wire[0], wire[3], _halves(small_local)])
    w_in4 = whole(gathered[0])
    w_in_a = _unstack_cols(w_in4)
    w_out_a = gathered[1].reshape(D, D)
    small = jnp.transpose(gathered[2].reshape(N_CHIPS, SMALL_ROWS, dq4), (1, 0, 2))
    b_in_full = jnp.transpose(small[0:3], (1, 0, 2)).reshape(1, 3 * D)
    wdw_full = small[3:3 + HALO].reshape(HALO, D)
    bdw_full, lng_full, lnb_full, bout_a_full = [small[35 + q].reshape(1, D) for q in range(4)]
    pg0, pg1 = post_ln_g[0:1], post_ln_g[1:2]
    pb0, pb1 = post_ln_b[0:1], post_ln_b[1:2]

    h, xb, later = _a_in_proj(x2, w_in4, b_in_full, [wire[1], wire[2], wire[4]], tm_mm)
    kv_full = _unstack_cols(whole(later[0]))
    b_in4 = whole(later[1])
    w_out_b = later[2].reshape(D, D)
    w_z = b_in4[3]
    w_g = [jnp.concatenate([b_in4[g], kv_full[:, g * D:(g + 1) * D], kv_full[:, (3 + g) * D:(4 + g) * D]], axis=1)
           for g in range(3)]
    xhu, rsu, vb, xh1, rs1, x1b, x1p1, x1p2 = _a_conv_out(
        h, x2, wdw_full, bdw_full, lng_full, lnb_full, w_out_a, bout_a_full, pg0, pb0, tm)
    x1g = [x1b, x1p1.reshape(S, D), x1p2.reshape(S, D)]
    qkv = [_mm_nn("b_qkv_g%d" % g, x1g[g], w_g[g], BF16, tm_mm, scale_first_tile=HEAD_DIM ** -0.5)
           for g in range(3)]
    z2 = _mm_nn("b_gate_proj", x1b, w_z, F32, tm_mm)
    og, lg = zip(*[_attn_fwd(g, qkv[g], D) for g in range(3)])
    d1, d2 = DILATIONS[1], DILATIONS[2]
    (v2b, ds2, ds2b, dz2b, da0, da1, da2, ls0, ls1, ls2, dl0, dl1, dl2, loss_acc, sums_b) = _b_merge_out_loss(
        og[0], og[1].reshape(d1, S // d1, D), og[2].reshape(d2, S // d2, D),
        lg[0], lg[1].reshape(d1, S // d1, LSE_LANES), lg[2].reshape(d2, S // d2, LSE_LANES),
        z2, xh1, target, w_out_b, b_b_out, pg0, pb0, pg1, pb1, tm)

    das = [da0, da1.reshape(S, D), da2.reshape(S, D)]
    lss = [ls0, ls1.reshape(S, LSE_LANES), ls2.reshape(S, LSE_LANES)]
    dls = [dl0, dl1.reshape(S, LSE_LANES), dl2.reshape(S, LSE_LANES)]
    dq, dkv = zip(*[_attn_bwd(g, qkv[g], das[g], lss[g], dls[g], D) for g in range(3)])
    ds1, ds1b, sums_1 = _b_dx1_ln1_bwd(ds2, dz2b, dq, dkv, xh1, rs1, w_z, w_g, pg0, tm)
    du1, dzab, sums_a = _a_gate_bwd(ds1b, h, xhu, rsu, w_out_a, lng_full, lnb_full, tm)
    dag, sums_c, wsum = _a_conv_bwd(du1, h, wdw_full, tm)
    grad_x = _a_dx(ds1, dag, dzab, w_in_a, tm)

    g_w_in = jnp.concatenate([_mm_tn("dw_a_in_ag", xb, dag, tm_mm), _mm_tn("dw_a_in_z", xb, dzab, tm_mm)], axis=1)
    g_w_out_a = _mm_tn("dw_a_out", vb, ds1b, tm_mm)
    g_w_out_b = _mm_tn("dw_b_out", v2b, ds2b, tm_mm)
    g_q = [_mm_tn("dw_b_q_g%d" % g, x1g[g], dq[g], tm_mm) for g in range(3)]
    g_z = _mm_tn("dw_b_z", x1b, dz2b, tm_mm)
    g_kvg = [_mm_tn("dw_kv_g%d" % g, x1g[g], dkv[g], tm_mm) for g in range(3)]
    g_kv = jnp.concatenate([t[:, :D] for t in g_kvg] + [t[:, D:] for t in g_kvg], axis=1)

    def by_chip_cols(gw):
        s4 = _stack_cols(gw)
        return s4.reshape(N_CHIPS, 2, D // 2, s4.shape[2])

    def by_chip_rows(gw):
        return gw.reshape(N_CHIPS, 2, D // 8, D)

    big_grads = [by_chip_cols(g_w_in), by_chip_cols(g_kv), jnp.stack(g_q + [g_z]).reshape(N_CHIPS, 2, D // 2, D),
                 by_chip_rows(g_w_out_a), by_chip_rows(g_w_out_b)]
    core = lax.axis_index("c").astype(jnp.int32).reshape(1)
    parts, parts_wire = _pair_sum(big_grads, _pair_exchange(big_grads), core)
    own_half = _chip_sum(parts, _chip_scatter(parts_wire), jchip.astype(jnp.int32).reshape(1))
    other_half = _pair_share(own_half)

    small_grads = _pack_rows([sums_c[0:1], sums_c[1:2], sums_a[3:4], wsum, sums_a[2:3], sums_a[0:1], sums_a[1:2],
                              sums_1[2:3], sums_b[2:3], sums_1[0:1], sums_b[0:1], sums_1[1:2], sums_b[1:2]], D)
    small_sum = _sum_devices(_gather_all_devices(small_grads))
    loss = lax.psum(loss_acc[0, 0], ("x", "y", "c"))

    big_m = [m_a_w_in[0], m_kv_w, m_b_w_in[0], m_a_w_out[0], m_b_w_out[0]]
    big_v = [v_a_w_in[0], v_kv_w, v_b_w_in[0], v_a_w_out[0], v_b_w_out[0]]
    shards, big_delta, big_new_m, big_new_v = [
        [a.reshape(2 * a.shape[1], a.shape[2]) for a in group] for group in _adamw_halves(
            [_halves(w) for w in big_local], own_half, other_half, [_halves(m) for m in big_m],
            [_halves(v) for v in big_v], core)]

    def chip_cols(rows):
        return lax.dynamic_slice_in_dim(rows, jchip * dq4, dq4, axis=1)

    g_b_in = lax.dynamic_slice_in_dim(small_sum[0:3].reshape(1, 3 * D), jchip * 3 * dq4, 3 * dq4, axis=1)
    small_g = [g_b_in, chip_cols(small_sum[3:3 + CONV_WIDTH]), chip_cols(small_sum[35:36]), chip_cols(small_sum[36:37]),
               chip_cols(small_sum[37:38]), chip_cols(small_sum[38:39]), small_sum[39:40], small_sum[40:42],
               small_sum[42:44]]
    small_w = [a_b_in, a_w_dw[0], a_b_dw, a_ln_g, a_ln_b, a_b_out, b_b_out, post_ln_g, post_ln_b]
    small_m = [m_a_b_in, m_a_w_dw[0], m_a_b_dw, m_a_ln_g, m_a_ln_b, m_a_b_out, m_b_b_out, m_post_ln_g, m_post_ln_b]
    small_v = [v_a_b_in, v_a_w_dw[0], v_a_b_dw, v_a_ln_g, v_a_ln_b, v_a_b_out, v_b_b_out, v_post_ln_g, v_post_ln_b]
    small_delta, small_new_m, small_new_v = _adamw("adamw_small", small_w, small_g, small_m, small_v, 1)

    def ordered(big, sm):
        return (big[0][None], sm[0], sm[1][None], sm[2], sm[3], sm[4], big[3][None], sm[5], big[1], big[2][None],
                big[4][None], sm[6], sm[7], sm[8])

    return (loss, grad_x.reshape(1, S, D), *ordered(shards, small_g), *ordered(big_delta, small_delta),
            *ordered(big_new_m, small_new_m), *ordered(big_new_v, small_new_v))
```

```python
import functools

import numpy as np
import jax
import jax.numpy as jnp
from jax import lax
from jax.experimental import pallas as pl
from jax.experimental.pallas import tpu as pltpu

F32 = jnp.float32
BF16 = jnp.bfloat16
MESH = pl.DeviceIdType.MESH
SDS = jax.ShapeDtypeStruct

HEAD_DIM = 64
BLOCK = 128
DILATIONS = (1, 4, 16)
ALIBI_MAX_EXP = 8.0
CONV_WIDTH = 31
HALO = 32
CONV_ROWS = 128
CHUNK_ROWS = 16
LSE_LANES = 128
DEPTH = 2
ALPHA = (2.0 * DEPTH) ** 0.25
LN_EPS = 1e-5
ADAM_LR = 0.001
ADAM_B1 = 0.9
ADAM_B2 = 0.999
ADAM_EPS = 1e-08
ADAM_WD = 0.01
ADAM_STEP = 10
N_CHIPS = 4
N_DEV = 8
VMEM_LIMIT = 56 * 2 ** 20
SMALL_ROWS = 48


def _params(sem=None):
    return pltpu.CompilerParams(dimension_semantics=sem, vmem_limit_bytes=VMEM_LIMIT)


def _sigmoid(x):
    return 1.0 / (1.0 + jnp.exp(-x))


def _silu_grad(x, s):
    return s * (1.0 + x * (1.0 - s))


def _ln_fwd(x):
    mu = jnp.mean(x, axis=-1, keepdims=True)
    xc = x - mu
    var = jnp.mean(xc * xc, axis=-1, keepdims=True)
    rstd = lax.rsqrt(var + LN_EPS)
    return xc * rstd, rstd


def _ln_bwd(dxhat, xhat, rstd):
    m1 = jnp.mean(dxhat, axis=-1, keepdims=True)
    m2 = jnp.mean(dxhat * xhat, axis=-1, keepdims=True)
    return rstd * (dxhat - m1 - xhat * m2)


def _dot(a, b):
    return jnp.dot(a, b, preferred_element_type=F32)


def _dot_t(a, b):
    return lax.dot_general(a, b, (((1,), (1,)), ((), ())), preferred_element_type=F32)


def _tdot(a, b):
    return lax.dot_general(a, b, (((0,), (0,)), ((), ())), preferred_element_type=F32)


def _colsum(x):
    return jnp.sum(x, axis=0, keepdims=True)


def _slopes(n_heads):
    return [float(np.float32(2.0 ** (-ALIBI_MAX_EXP * (h + 1) / n_heads))) for h in range(n_heads)]


def _for_chunks(rows, fn):
    def step(c, carry):
        fn(pl.ds(pl.multiple_of(c * CHUNK_ROWS, CHUNK_ROWS), CHUNK_ROWS))
        return carry
    lax.fori_loop(0, rows // CHUNK_ROWS, step, 0)


def _to_chunks(chunks_ref, x):
    for cc in range(chunks_ref.shape[0]):
        chunks_ref[cc] = x[:, cc * 128:(cc + 1) * 128]


def _deinterleave(chunks_ref, out_ref, d, dtype):
    rows = chunks_ref.shape[1] // d
    for r in range(d):
        for cc in range(chunks_ref.shape[0]):
            out_ref[r, :, cc * 128:(cc + 1) * 128] = chunks_ref[cc, pl.ds(r, rows, stride=d), :].astype(dtype)


def _interleave(in_ref, chunks_ref, d):
    rows = chunks_ref.shape[1] // d
    for r in range(d):
        for cc in range(chunks_ref.shape[0]):
            chunks_ref[cc, pl.ds(r, rows, stride=d), :] = in_ref[r, :, cc * 128:(cc + 1) * 128]


def _hbm_specs(n):
    return [pl.BlockSpec(memory_space=pl.ANY)] * n


def _position():
    x, y, c = lax.axis_index("x"), lax.axis_index("y"), lax.axis_index("c")
    return x, y, c


def _gather_stages(ins, outs, send_sems, recv_sems, local_sems):
    n = len(ins)

    def plan():
        x, y, c = _position()
        j = 2 * x + y
        me, sibling = (x, y, c), (x, y, 1 - c)
        chips = [(1 - x, y), (x, 1 - y), (1 - x, 1 - y)]

        def copy(i, k, src, dst, to):
            return pltpu.make_async_remote_copy(
                src_ref=src, dst_ref=dst, send_sem=send_sems.at[i, k], recv_sem=recv_sems.at[i, k],
                device_id=to, device_id_type=MESH)

        local = [pltpu.make_async_copy(ins[i], outs[i].at[j], local_sems.at[i]) for i in range(n)]
        first, landing, passed, passed_landing = [], [], [], []
        for i in range(n):
            for k, chip in enumerate(chips):
                pj = 2 * chip[0] + chip[1]
                first.append(copy(i, k, ins[i].at[c], outs[i].at[j, c], (*chip, c)))
                landing.append(copy(i, k, ins[i].at[c], outs[i].at[pj, c], me))
                passed.append(copy(i, 3 + k, outs[i].at[pj, c], outs[i].at[pj, c], sibling))
                passed_landing.append(copy(i, 3 + k, ins[i].at[c], outs[i].at[pj, 1 - c], me))
        return local, first, landing, passed, passed_landing

    def start():
        local, first, _, _, _ = plan()
        for cp in local + first:
            cp.start()

    def forward():
        _, _, landing, passed, _ = plan()
        for arrived, cp in zip(landing, passed):
            arrived.wait_recv()
            cp.start()

    def finish():
        local, first, _, passed, passed_landing = plan()
        for cp in passed_landing:
            cp.wait_recv()
        for cp in first + passed:
            cp.wait_send()
        for cp in local:
            cp.wait()

    return start, forward, finish


def _gather_scratch(n):
    return [pltpu.SemaphoreType.DMA((n, 6)), pltpu.SemaphoreType.DMA((n, 6)), pltpu.SemaphoreType.DMA((n,))]


def _all_gather_chips(shards):
    n = len(shards)

    def body(*refs):
        for stage in _gather_stages(refs[:n], refs[n:2 * n], *refs[2 * n:]):
            stage()

    return pl.pallas_call(
        body, name="all_gather_chips",
        out_shape=[SDS((N_CHIPS,) + s.shape, s.dtype) for s in shards],
        in_specs=_hbm_specs(n), out_specs=_hbm_specs(n), scratch_shapes=_gather_scratch(n),
    )(*shards)


def _pair_exchange(grads, tag):
    n = len(grads)

    def body(*refs):
        ins, outs = refs[:n], refs[n:2 * n]
        send_sems, recv_sems = refs[2 * n:]
        x, y, c = _position()
        sibling = (x, y, 1 - c)
        remote = []
        for i in range(n):
            for j in range(N_CHIPS):
                remote.append(pltpu.make_async_remote_copy(
                    src_ref=ins[i].at[j, 1 - c], dst_ref=outs[i].at[j],
                    send_sem=send_sems.at[i, j], recv_sem=recv_sems.at[i, j],
                    device_id=sibling, device_id_type=MESH))
        for cp in remote:
            cp.start()
        for cp in remote:
            cp.wait_recv()
        for cp in remote:
            cp.wait_send()

    return pl.pallas_call(
        body, name="grad_pair_exchange_" + tag,
        out_shape=[SDS((N_CHIPS,) + g.shape[2:], g.dtype) for g in grads],
        in_specs=_hbm_specs(n), out_specs=_hbm_specs(n),
        scratch_shapes=[pltpu.SemaphoreType.DMA((n, N_CHIPS)), pltpu.SemaphoreType.DMA((n, N_CHIPS))],
    )(*grads)


def _scatter_copies(ins, outs, send_sems, recv_sems):
    x, y, c = _position()
    chips = [(1 - x, y), (x, 1 - y), (1 - x, 1 - y)]
    return [pltpu.make_async_remote_copy(
        src_ref=ins[i].at[2 * chip[0] + chip[1]], dst_ref=outs[i].at[k],
        send_sem=send_sems.at[i, k], recv_sem=recv_sems.at[i, k],
        device_id=(*chip, c), device_id_type=MESH) for i in range(len(ins)) for k, chip in enumerate(chips)]


def _scatter_start(*refs):
    for cp in _scatter_copies(*refs):
        cp.start()


def _scatter_finish(*refs):
    copies = _scatter_copies(*refs)
    for cp in copies:
        cp.wait_recv()
    for cp in copies:
        cp.wait_send()


def _scatter_scratch(n):
    return [pltpu.SemaphoreType.DMA((n, 3)), pltpu.SemaphoreType.DMA((n, 3))]


def _chip_scatter(parts, tag):
    n = len(parts)

    def body(*refs):
        args = (refs[:n], refs[n:2 * n], *refs[2 * n:])
        _scatter_start(*args)
        _scatter_finish(*args)

    return pl.pallas_call(
        body, name="grad_chip_scatter_" + tag,
        out_shape=[SDS((3,) + p.shape[1:], p.dtype) for p in parts],
        in_specs=_hbm_specs(n), out_specs=_hbm_specs(n), scratch_shapes=_scatter_scratch(n),
    )(*parts)


def _pair_share(halves, tag):
    n = len(halves)

    def body(*refs):
        ins, outs = refs[:n], refs[n:2 * n]
        send_sems, recv_sems = refs[2 * n:]
        x, y, c = _position()
        remote = [pltpu.make_async_remote_copy(
            src_ref=ins[i], dst_ref=outs[i], send_sem=send_sems.at[i], recv_sem=recv_sems.at[i],
            device_id=(x, y, 1 - c), device_id_type=MESH) for i in range(n)]
        for cp in remote:
            cp.start()
        for cp in remote:
            cp.wait_recv()
        for cp in remote:
            cp.wait_send()

    return pl.pallas_call(
        body, name="grad_pair_share_" + tag,
        out_shape=[SDS(h.shape, h.dtype) for h in halves],
        in_specs=_hbm_specs(n), out_specs=_hbm_specs(n),
        scratch_shapes=[pltpu.SemaphoreType.DMA((n,)), pltpu.SemaphoreType.DMA((n,))],
    )(*halves)


def _gather_all_devices(slab):
    def body(in_ref, out_ref, send_sems, recv_sems, local_sem):
        x, y, c = _position()
        me = 4 * x + 2 * y + c
        local = pltpu.make_async_copy(in_ref, out_ref.at[me], local_sem)
        local.start()
        remote, landing = [], []
        for mask in range(1, N_DEV):
            px, py, pc = x ^ (mask >> 2), y ^ ((mask >> 1) & 1), c ^ (mask & 1)
            peer = 4 * px + 2 * py + pc
            remote.append(pltpu.make_async_remote_copy(
                src_ref=in_ref, dst_ref=out_ref.at[me], send_sem=send_sems.at[mask - 1],
                recv_sem=recv_sems.at[mask - 1], device_id=(px, py, pc), device_id_type=MESH))
            landing.append(pltpu.make_async_remote_copy(
                src_ref=in_ref, dst_ref=out_ref.at[peer], send_sem=send_sems.at[mask - 1],
                recv_sem=recv_sems.at[mask - 1], device_id=(px, py, pc), device_id_type=MESH))
        for cp in remote:
            cp.start()
        for cp in landing:
            cp.wait_recv()
        for cp in remote:
            cp.wait_send()
        local.wait()

    return pl.pallas_call(
        body, name="small_grad_gather",
        out_shape=SDS((N_DEV,) + slab.shape, slab.dtype),
        in_specs=_hbm_specs(1), out_specs=pl.BlockSpec(memory_space=pl.ANY),
        scratch_shapes=[pltpu.SemaphoreType.DMA((N_DEV - 1,)), pltpu.SemaphoreType.DMA((N_DEV - 1,)),
                        pltpu.SemaphoreType.DMA],
    )(slab)


def _row_splits(arrays):
    return min(a.shape[-2] for a in arrays) // 16


def _pair_sum(grads, recvd, core, tag):
    n = len(grads)
    splits = _row_splits(recvd)

    def body(core_ref, *refs):
        for i in range(n):
            s = refs[i][...] + refs[n + i][...]
            refs[2 * n + i][...] = s
            refs[3 * n + i][...] = s.astype(BF16)

    mine = [pl.BlockSpec((N_CHIPS, None, r.shape[1] // splits, r.shape[2]), lambda s, core: (0, core[0], s, 0))
            for r in recvd]
    block = [pl.BlockSpec((N_CHIPS, r.shape[1] // splits, r.shape[2]), lambda s, core: (0, s, 0)) for r in recvd]
    outs = pl.pallas_call(
        body, name="grad_pair_sum_" + tag,
        grid_spec=pltpu.PrefetchScalarGridSpec(
            num_scalar_prefetch=1, grid=(splits,), in_specs=mine + block, out_specs=block + block),
        out_shape=[SDS(r.shape, F32) for r in recvd] + [SDS(r.shape, BF16) for r in recvd],
        compiler_params=_params(("parallel",)),
    )(core, *grads, *recvd)
    return outs[:n], outs[n:]


def _chip_sum(parts, landed, chip, tag):
    n = len(parts)
    splits = _row_splits(landed)

    def body(chip_ref, *refs):
        for i in range(n):
            acc = refs[i][...]
            for k in range(3):
                acc = acc + refs[n + i][k].astype(F32)
            refs[2 * n + i][...] = acc

    rows = lambda p: p.shape[1] // splits
    return pl.pallas_call(
        body, name="grad_chip_sum_" + tag,
        grid_spec=pltpu.PrefetchScalarGridSpec(
            num_scalar_prefetch=1, grid=(splits,),
            in_specs=[pl.BlockSpec((None, rows(p), p.shape[2]), lambda s, chip: (chip[0], s, 0)) for p in parts]
            + [pl.BlockSpec((3, rows(p), p.shape[2]), lambda s, chip: (0, s, 0)) for p in parts],
            out_specs=[pl.BlockSpec((rows(p), p.shape[2]), lambda s, chip: (s, 0)) for p in parts]),
        out_shape=[SDS(p.shape[1:], F32) for p in parts],
        compiler_params=_params(("parallel",)),
    )(chip, *parts, *landed)


def _adamw_math(w, g, m, v):
    m = ADAM_B1 * m + (1.0 - ADAM_B1) * g
    v = ADAM_B2 * v + (1.0 - ADAM_B2) * (g * g)
    m_hat = m / (1.0 - ADAM_B1 ** ADAM_STEP)
    v_hat = v / (1.0 - ADAM_B2 ** ADAM_STEP)
    delta = -ADAM_LR * (m_hat / (jnp.sqrt(v_hat) + ADAM_EPS) + ADAM_WD * w)
    return delta, m, v


def _adamw(name, ws, gs, ms, vs, splits):
    n = len(ws)

    def body(*refs):
        for i in range(n):
            w, g, m, v = (refs[q * n + i][...] for q in range(4))
            delta, m, v = _adamw_math(w, g, m, v)
            refs[4 * n + i][...] = delta
            refs[5 * n + i][...] = m
            refs[6 * n + i][...] = v

    def spec(a):
        if splits == 1:
            return pl.BlockSpec(a.shape, lambda s: (0, 0))
        return pl.BlockSpec((a.shape[0] // splits, a.shape[1]), lambda s: (s, 0))

    specs = [spec(a) for a in ws]
    outs = pl.pallas_call(
        body, name=name, grid=(splits,),
        in_specs=specs * 4, out_specs=specs * 3,
        out_shape=[SDS(a.shape, F32) for a in ws] * 3,
        compiler_params=_params(("parallel",)),
    )(*ws, *gs, *ms, *vs)
    return outs[:n], outs[n:2 * n], outs[2 * n:]


def _adamw_halves(ws, own, other, ms, vs, core):
    n = len(ws)
    splits = _row_splits(own)

    def body(core_ref, *refs):
        mine = pl.program_id(0) == core_ref[0]
        for i in range(n):
            g = jnp.where(mine, refs[n + i][...], refs[2 * n + i][...])
            delta, m, v = _adamw_math(refs[i][...], g, refs[3 * n + i][...], refs[4 * n + i][...])
            refs[5 * n + i][...] = g
            refs[6 * n + i][...] = delta
            refs[7 * n + i][...] = m
            refs[8 * n + i][...] = v

    rows = lambda a: a.shape[0] // splits
    half = [pl.BlockSpec((None, rows(a), a.shape[1]), lambda hh, s, core: (hh, s, 0)) for a in own]
    flat = [pl.BlockSpec((rows(a), a.shape[1]), lambda hh, s, core: (s, 0)) for a in own]
    outs = pl.pallas_call(
        body, name="adamw_big",
        grid_spec=pltpu.PrefetchScalarGridSpec(
            num_scalar_prefetch=1, grid=(2, splits), in_specs=half + flat + flat + half + half, out_specs=half * 4),
        out_shape=[SDS(w.shape, F32) for w in ws] * 4,
        compiler_params=_params(("parallel", "parallel")),
    )(core, *ws, *own, *other, *ms, *vs)
    return outs[:n], outs[n:2 * n], outs[2 * n:3 * n], outs[3 * n:]


def _sum_devices(slabs):
    def body(in_ref, out_ref):
        acc = in_ref[0]
        for k in range(1, N_DEV):
            acc = acc + in_ref[k]
        out_ref[...] = acc

    return pl.pallas_call(
        body, name="small_grad_sum", out_shape=SDS(slabs.shape[1:], F32),
    )(slabs)


def _mm_nn(name, a, w, out_dtype, tm, scale_first_tile=None):
    S, K = a.shape
    N = w.shape[1]
    tn = K

    def body(a_ref, w_ref, o_ref):
        acc = _dot(a_ref[...], w_ref[...])
        if scale_first_tile is not None:
            acc = acc * jnp.where(pl.program_id(1) == 0, scale_first_tile, 1.0)
        o_ref[...] = acc.astype(out_dtype)

    return pl.pallas_call(
        body, name=name, grid=(S // tm, N // tn),
        in_specs=[pl.BlockSpec((tm, K), lambda i, t: (i, 0)), pl.BlockSpec((K, tn), lambda i, t: (0, t))],
        out_specs=pl.BlockSpec((tm, tn), lambda i, t: (i, t)),
        out_shape=SDS((S, N), out_dtype),
        compiler_params=_params(("parallel", "arbitrary")),
    )(a, w)


def _mm_tn(name, a, b, tk):
    S, M = a.shape
    N = b.shape[1]
    tn = M

    def body(a_ref, b_ref, o_ref):
        @pl.when(pl.program_id(1) == 0)
        def _():
            o_ref[...] = jnp.zeros_like(o_ref)
        o_ref[...] += _tdot(a_ref[...], b_ref[...])

    return pl.pallas_call(
        body, name=name, grid=(N // tn, S // tk),
        in_specs=[pl.BlockSpec((tk, M), lambda t, k: (k, 0)), pl.BlockSpec((tk, tn), lambda t, k: (k, t))],
        out_specs=pl.BlockSpec((M, tn), lambda t, k: (0, t)),
        out_shape=SDS((M, N), F32),
        compiler_params=_params(("parallel", "arbitrary")),
    )(a, b)


def _a_in_proj(x, w4, b_in, later_shards, tm):
    S, D = x.shape
    nj = w4.shape[2]
    n = len(later_shards)
    steps = S // tm

    def body(x_ref, w_ref, b_ref, *refs):
        shard_refs, (h_ref, xb_ref), gathered_refs = refs[:n], refs[n:n + 2], refs[n + 2:2 * n + 2]
        start, forward, finish = _gather_stages(shard_refs, gathered_refs, *refs[2 * n + 2:])
        i, t = pl.program_id(0), pl.program_id(1)
        pl.when((i == 0) & (t == 0))(start)
        pl.when((i == steps // 2) & (t == 0))(forward)
        xb = x_ref[...].astype(BF16)

        @pl.when(t == 0)
        def _():
            xb_ref[...] = xb
        h_ref[...] = _dot(xb, w_ref[...]) + b_ref[...]
        pl.when((i == steps - 1) & (t == N_CHIPS - 1))(finish)

    outs = pl.pallas_call(
        body, name="a_in_proj", grid=(steps, N_CHIPS),
        in_specs=[pl.BlockSpec((tm, D), lambda i, t: (i, 0)),
                  pl.BlockSpec((None, D, nj), lambda i, t: (t, 0, 0)),
                  pl.BlockSpec((1, nj), lambda i, t: (0, t))] + _hbm_specs(n),
        out_specs=[pl.BlockSpec((tm, nj), lambda i, t: (i, t)), pl.BlockSpec((tm, D), lambda i, t: (i, 0))]
        + _hbm_specs(n),
        out_shape=[SDS((S, N_CHIPS * nj), F32), SDS((S, D), BF16)]
        + [SDS((N_CHIPS,) + s.shape, s.dtype) for s in later_shards],
        scratch_shapes=_gather_scratch(n),
        compiler_params=_params(("arbitrary", "arbitrary")),
    )(x, w4, b_in, *later_shards)
    return outs[0], outs[1], outs[2:]


def _fill_glu_ext(ext_ref, a_ref, g_ref, ah_ref, gh_ref, has_prev):
    u0h = ah_ref[...] * _sigmoid(gh_ref[...])
    ext_ref[0:HALO, :] = jnp.where(has_prev, u0h, 0.0)
    ext_ref[HALO:, :] = a_ref[...] * _sigmoid(g_ref[...])


def _fill_shifts(shift_ref, ext_ref):
    for s in range(1, 8):
        shift_ref[s - 1] = ext_ref[s:s + shift_ref.shape[1], :]


def _tap_windows(ext_ref, shift_ref, starts, r0, rows, lanes):
    for s in range(8):
        taps = [(k, st) for k, st in enumerate(starts) if st % 8 == s]
        if not taps:
            continue
        lo = min(st for _, st in taps)
        hi = max(st for _, st in taps)
        if s == 0:
            win = ext_ref[r0 + lo:r0 + hi + rows, lanes]
        else:
            win = shift_ref[s - 1, r0 + lo - s:r0 + hi - s + rows, lanes]
        for k, st in taps:
            yield k, win[st - lo:st - lo + rows]


def _a_conv_out(h, x, wdw, bdw, lng, lnb, wout, bout, pg, pb, tm):
    S, D = x.shape
    hb = tm // HALO
    d1, d2 = DILATIONS[1], DILATIONS[2]

    def body(a_ref, g_ref, z_ref, ah_ref, gh_ref, x_ref, wdw_ref, bdw_ref, lng_ref, lnb_ref, wout_ref,
             bout_ref, pg_ref, pb_ref, xhu_ref, rsu_ref, vb_ref, xh1_ref, rs1_ref, x1b_ref, x1p1_ref,
             x1p2_ref, ext_ref, u1_ref, x1_ref, sh_ref):
        i = pl.program_id(0)
        _fill_glu_ext(ext_ref, a_ref, g_ref, ah_ref, gh_ref, i > 0)
        _fill_shifts(sh_ref, ext_ref)
        starts = [HALO - (CONV_WIDTH - 1) + k for k in range(CONV_WIDTH)]
        for cc in range(D // 128):
            lanes = slice(cc * 128, (cc + 1) * 128)
            for r0 in range(0, tm, CONV_ROWS):
                acc = jnp.broadcast_to(bdw_ref[:, lanes], (CONV_ROWS, 128))
                for k, win in _tap_windows(ext_ref, sh_ref, starts, r0, CONV_ROWS, lanes):
                    acc = acc + wdw_ref[k:k + 1, lanes] * win
                u1_ref[r0:r0 + CONV_ROWS, lanes] = acc
        def norm_gate(rows):
            xhu, rsu = _ln_fwd(u1_ref[rows, :])
            xhu_ref[rows, :] = xhu
            rsu_ref[rows, :] = rsu
            u2 = xhu * lng_ref[...] + lnb_ref[...]
            z = z_ref[rows, :]
            vb_ref[rows, :] = ((u2 * _sigmoid(u2)) * (z * _sigmoid(z))).astype(BF16)
        _for_chunks(tm, norm_gate)
        u1_ref[...] = _dot(vb_ref[...], wout_ref[...])

        def post_ln(rows):
            xh1, rs1 = _ln_fwd(ALPHA * x_ref[rows, :] + u1_ref[rows, :] + bout_ref[...])
            xh1_ref[rows, :] = xh1
            rs1_ref[rows, :] = rs1
            x1 = xh1 * pg_ref[...] + pb_ref[...]
            x1b_ref[rows, :] = x1.astype(BF16)
            for cc in range(D // 128):
                x1_ref[cc, rows, :] = x1[:, cc * 128:(cc + 1) * 128]
        _for_chunks(tm, post_ln)
        _deinterleave(x1_ref, x1p1_ref, d1, BF16)
        _deinterleave(x1_ref, x1p2_ref, d2, BF16)

    tile = lambda c: pl.BlockSpec((tm, D), lambda i, c=c: (i, c))
    halo = lambda c: pl.BlockSpec((HALO, D), lambda i, c=c: (jnp.maximum(i * hb - 1, 0), c))
    row = pl.BlockSpec((1, D), lambda i: (0, 0))
    stat = pl.BlockSpec((tm, 1), lambda i: (i, 0))
    return pl.pallas_call(
        body, name="a_conv_out", grid=(S // tm,),
        in_specs=[tile(0), tile(1), tile(2), halo(0), halo(1), tile(0),
                  pl.BlockSpec((HALO, D), lambda i: (0, 0)), row, row, row,
                  pl.BlockSpec((D, D), lambda i: (0, 0)), row, row, row],
        out_specs=[tile(0), stat, tile(0), tile(0), stat, tile(0),
                   pl.BlockSpec((d1, tm // d1, D), lambda i: (0, i, 0)),
                   pl.BlockSpec((d2, tm // d2, D), lambda i: (0, i, 0))],
        out_shape=[SDS((S, D), F32), SDS((S, 1), F32), SDS((S, D), BF16), SDS((S, D), F32), SDS((S, 1), F32),
                   SDS((S, D), BF16), SDS((d1, S // d1, D), BF16), SDS((d2, S // d2, D), BF16)],
        scratch_shapes=[pltpu.VMEM((HALO + tm, D), F32), pltpu.VMEM((tm, D), F32),
                        pltpu.VMEM((D // 128, tm, 128), F32), pltpu.VMEM((7, HALO + tm - 8, D), F32)],
        compiler_params=_params(("parallel",)),
    )(h, h, h, h, h, x, wdw, bdw, lng, lnb, wout, bout, pg, pb)


def _band(n, dilation):
    qi = lax.broadcasted_iota(jnp.int32, (BLOCK, 2 * BLOCK), 0)
    kj = lax.broadcasted_iota(jnp.int32, (BLOCK, 2 * BLOCK), 1)
    dist = qi + BLOCK - kj
    valid = (dist >= 0) & (dist <= BLOCK) & ((n > 0) | (kj >= BLOCK))
    return jnp.where(valid, dist.astype(F32) * float(-dilation), -jnp.inf)


def _attn_fwd(g, qkv, D):
    S = qkv.shape[0]
    d = DILATIONS[g]
    nb = S // (d * BLOCK)
    H = D // HEAD_DIM
    slopes = _slopes(H)

    def body(q_ref, kp_ref, kc_ref, vp_ref, vc_ref, o_ref, lse_ref):
        neg_dist = _band(pl.program_id(1), d)
        lane = lax.broadcasted_iota(jnp.int32, (BLOCK, LSE_LANES), 1)
        low = lane < HEAD_DIM
        lse = jnp.zeros((BLOCK, LSE_LANES), F32)
        for hp in range(H // 2):
            sl = slice(hp * 128, (hp + 1) * 128)
            q = q_ref[:, sl]
            k = jnp.concatenate([kp_ref[:, sl], kc_ref[:, sl]], axis=0)
            v = jnp.concatenate([vp_ref[:, sl], vc_ref[:, sl]], axis=0)
            o = []
            for a in range(2):
                h = 2 * hp + a
                s = _dot_t(jnp.where(low if a == 0 else ~low, q, jnp.zeros_like(q)), k)
                s = s + slopes[h] * neg_dist
                m = jnp.max(s, axis=1, keepdims=True)
                p = jnp.exp(s - m)
                l = jnp.sum(p, axis=1, keepdims=True)
                o.append(_dot(p.astype(BF16), v) * (1.0 / l))
                lse = jnp.where(lane == h, m + jnp.log(l), lse)
            o_ref[:, sl] = jnp.where(low, o[0], o[1])
        lse_ref[...] = lse

    cur = lambda c: pl.BlockSpec((BLOCK, D), lambda r, n, c=c: (r * nb + n, c))
    prev = lambda c: pl.BlockSpec((BLOCK, D), lambda r, n, c=c: (r * nb + jnp.maximum(n - 1, 0), c))
    return pl.pallas_call(
        body, name="attn_fwd_g%d" % g, grid=(d, nb),
        in_specs=[cur(0), prev(1), cur(1), prev(2), cur(2)],
        out_specs=[cur(0), pl.BlockSpec((BLOCK, LSE_LANES), lambda r, n: (r * nb + n, 0))],
        out_shape=[SDS((S, D), F32), SDS((S, LSE_LANES), F32)],
        compiler_params=_params(("parallel", "parallel")),
    )(qkv, qkv, qkv, qkv, qkv)


def _attn_bwd(g, qkv, do, lse, delta, D):
    S = qkv.shape[0]
    d = DILATIONS[g]
    nb = S // (d * BLOCK)
    H = D // HEAD_DIM
    slopes = _slopes(H)

    def body(q_ref, kp_ref, kc_ref, vp_ref, vc_ref, do_ref, lse_ref, dl_ref, dq_ref, dkv_ref, ck_ref, cv_ref):
        n = pl.program_id(1)

        @pl.when(n == 0)
        def _():
            ck_ref[...] = jnp.zeros_like(ck_ref)
            cv_ref[...] = jnp.zeros_like(cv_ref)

        @pl.when(n < nb)
        def _():
            neg_dist = _band(n, d)
            low = lax.broadcasted_iota(jnp.int32, (BLOCK, 128), 1) < HEAD_DIM
            low2 = lax.broadcasted_iota(jnp.int32, (2 * BLOCK, 128), 1) < HEAD_DIM
            for hp in range(H // 2):
                sl = slice(hp * 128, (hp + 1) * 128)
                q = q_ref[:, sl]
                do2 = do_ref[:, sl]
                k = jnp.concatenate([kp_ref[:, sl], kc_ref[:, sl]], axis=0)
                v = jnp.concatenate([vp_ref[:, sl], vc_ref[:, sl]], axis=0)
                zero = jnp.zeros_like(q)
                q_do = jnp.concatenate([jnp.concatenate([q, zero], axis=1),
                                        jnp.concatenate([zero, do2], axis=1)], axis=0)
                dq, dkv = [], []
                for a in range(2):
                    h = 2 * hp + a
                    keep = low if a == 0 else ~low
                    s = _dot_t(jnp.where(keep, q, zero), k)
                    s = s + slopes[h] * neg_dist
                    p = jnp.exp(s - lse_ref[:, h:h + 1])
                    dp = _dot_t(jnp.where(keep, do2, zero), v)
                    dsb = (p * (dp - dl_ref[:, h:h + 1])).astype(BF16)
                    dq.append(_dot(dsb, k))
                    dkv.append(_tdot(jnp.concatenate([dsb, p.astype(BF16)], axis=0), q_do))
                dq_ref[:, sl] = (jnp.where(low, dq[0], dq[1]) * (HEAD_DIM ** -0.5)).astype(BF16)
                dk2 = jnp.where(low2, dkv[0][:, :128], dkv[1][:, :128])
                dv2 = jnp.where(low2, dkv[0][:, 128:], dkv[1][:, 128:])
                dkv_ref[:, sl] = (ck_ref[:, sl] + dk2[:BLOCK]).astype(BF16)
                dkv_ref[:, D + hp * 128:D + (hp + 1) * 128] = (cv_ref[:, sl] + dv2[:BLOCK]).astype(BF16)
                ck_ref[:, sl] = dk2[BLOCK:]
                cv_ref[:, sl] = dv2[BLOCK:]

        @pl.when(n == nb)
        def _():
            dkv_ref[:, :D] = ck_ref[...].astype(BF16)
            dkv_ref[:, D:] = cv_ref[...].astype(BF16)

    nq = lambda n: jnp.minimum(n, nb - 1)
    cur = lambda c: pl.BlockSpec((BLOCK, D), lambda r, n, c=c: (r * nb + nq(n), c))
    prev = lambda c: pl.BlockSpec((BLOCK, D), lambda r, n, c=c: (r * nb + jnp.maximum(nq(n) - 1, 0), c))
    stat = pl.BlockSpec((BLOCK, LSE_LANES), lambda r, n: (r * nb + nq(n), 0))
    return pl.pallas_call(
        body, name="attn_bwd_g%d" % g, grid=(d, nb + 1),
        in_specs=[cur(0), prev(1), cur(1), prev(2), cur(2), cur(0), stat, stat],
        out_specs=[cur(0), pl.BlockSpec((BLOCK, 2 * D), lambda r, n: (r * nb + jnp.maximum(n - 1, 0), 0))],
        out_shape=[SDS((S, D), BF16), SDS((S, 2 * D), BF16)],
        scratch_shapes=[pltpu.VMEM((BLOCK, D), F32), pltpu.VMEM((BLOCK, D), F32)],
        compiler_params=_params(("parallel", "arbitrary")),
    )(qkv, qkv, qkv, qkv, qkv, do, lse, delta)


def _b_merge_out_loss(o0, o1, o2, l0, l1, l2, z2, xh1, target, wbo, bbo, pg0, pb0, pg1, pb1, tm):
    S, D = o0.shape
    H = D // HEAD_DIM
    d1, d2 = DILATIONS[1], DILATIONS[2]
    inv_d = 1.0 / D

    def body(o0_ref, o1_ref, o2_ref, l0_ref, l1_ref, l2_ref, z_ref, xh1_ref, t_ref, wbo_ref, bbo_ref,
             pg0_ref, pb0_ref, pg1_ref, pb1_ref,
             v2b_ref, ds2_ref, ds2b_ref, dz2b_ref, da0_ref, da1_ref, da2_ref, ls0_ref, ls1_ref, ls2_ref,
             dl0_ref, dl1_ref, dl2_ref, loss_ref, sums_ref,
             o1n_ref, o2n_ref, l1n_ref, l2n_ref, att_ref, wa_ref, wb_ref, wc_ref, y_ref):
        i = pl.program_id(0)

        @pl.when(i == 0)
        def _():
            loss_ref[...] = jnp.zeros_like(loss_ref)
            sums_ref[...] = jnp.zeros_like(sums_ref)
        _interleave(o1_ref, o1n_ref, d1)
        _interleave(o2_ref, o2n_ref, d2)
        for r in range(d1):
            l1n_ref[pl.ds(r, tm // d1, stride=d1), :] = l1_ref[r]
        for r in range(d2):
            l2n_ref[pl.ds(r, tm // d2, stride=d2), :] = l2_ref[r]
        la, lb, lc = l0_ref[...], l1n_ref[...], l2n_ref[...]
        m = jnp.maximum(jnp.maximum(la, lb), lc)
        ea, eb, ec = jnp.exp(la - m), jnp.exp(lb - m), jnp.exp(lc - m)
        den = ea + eb + ec
        wa_ref[...] = ea / den
        wb_ref[...] = eb / den
        wc_ref[...] = ec / den
        ls0_ref[...] = m + jnp.log(den)

        def merge(rows):
            wa, wb, wc = wa_ref[rows, :], wb_ref[rows, :], wc_ref[rows, :]
            for h in range(H):
                sl = slice(h * HEAD_DIM, (h + 1) * HEAD_DIM)
                cc, hl = divmod(h * HEAD_DIM, 128)
                att_ref[rows, sl] = (wa[:, h:h + 1] * o0_ref[rows, sl]
                                     + wb[:, h:h + 1] * o1n_ref[cc, rows, hl:hl + HEAD_DIM]
                                     + wc[:, h:h + 1] * o2n_ref[cc, rows, hl:hl + HEAD_DIM])
            z = z_ref[rows, :]
            v2b_ref[rows, :] = (att_ref[rows, :] * (z * _sigmoid(z))).astype(BF16)
        _for_chunks(tm, merge)
        y_ref[...] = _dot(v2b_ref[...], wbo_ref[...])

        def post_ln(rows):
            x1 = xh1_ref[rows, :] * pg0_ref[...] + pb0_ref[...]
            xh2, rs2 = _ln_fwd(ALPHA * x1 + y_ref[rows, :] + bbo_ref[...])
            err = xh2 * pg1_ref[...] + pb1_ref[...] - t_ref[rows, :]
            dy = err * inv_d
            ds2 = _ln_bwd(dy * pg1_ref[...], xh2, rs2)
            ds2_ref[rows, :] = ds2
            ds2b_ref[rows, :] = ds2.astype(BF16)
            loss_ref[...] += 0.5 * inv_d * jnp.sum(err * err)
            sums_ref[0:1, :] += _colsum(dy * xh2)
            sums_ref[1:2, :] += _colsum(dy)
            sums_ref[2:3, :] += _colsum(ds2)
        _for_chunks(tm, post_ln)
        y_ref[...] = _dot_t(ds2b_ref[...], wbo_ref[...])

        def gate_bwd(rows):
            z = z_ref[rows, :]
            sz = _sigmoid(z)
            dv2 = y_ref[rows, :]
            att = att_ref[rows, :]
            datt = dv2 * (z * sz)
            dz2b_ref[rows, :] = (dv2 * att * _silu_grad(z, sz)).astype(BF16)
            prod = datt * att
            lane = lax.broadcasted_iota(jnp.int32, (CHUNK_ROWS, LSE_LANES), 1)
            dl = jnp.zeros((CHUNK_ROWS, LSE_LANES), F32)
            for h in range(H):
                sl = slice(h * HEAD_DIM, (h + 1) * HEAD_DIM)
                dl = jnp.where(lane == h, jnp.sum(prod[:, sl], axis=1, keepdims=True), dl)
            dl0_ref[rows, :] = dl
            da0_ref[rows, :] = datt.astype(BF16)
            for cc in range(D // 128):
                o1n_ref[cc, rows, :] = datt[:, cc * 128:(cc + 1) * 128]
        _for_chunks(tm, gate_bwd)
        _deinterleave(o1n_ref, da1_ref, d1, BF16)
        _deinterleave(o1n_ref, da2_ref, d2, BF16)
        for r in range(d1):
            ls1_ref[r] = ls0_ref[pl.ds(r, tm // d1, stride=d1), :]
            dl1_ref[r] = dl0_ref[pl.ds(r, tm // d1, stride=d1), :]
        for r in range(d2):
            ls2_ref[r] = ls0_ref[pl.ds(r, tm // d2, stride=d2), :]
            dl2_ref[r] = dl0_ref[pl.ds(r, tm // d2, stride=d2), :]

    tile = pl.BlockSpec((tm, D), lambda i: (i, 0))
    stat = pl.BlockSpec((tm, LSE_LANES), lambda i: (i, 0))
    perm = lambda d, w: pl.BlockSpec((d, tm // d, w), lambda i: (0, i, 0))
    row = pl.BlockSpec((1, D), lambda i: (0, 0))
    acc = lambda w: pl.BlockSpec((8, w), lambda i: (0, 0))
    pshape = lambda d, w, dt: SDS((d, S // d, w), dt)
    return pl.pallas_call(
        body, name="b_merge_out_loss", grid=(S // tm,),
        in_specs=[tile, perm(d1, D), perm(d2, D), stat, perm(d1, LSE_LANES), perm(d2, LSE_LANES),
                  tile, tile, tile, pl.BlockSpec((D, D), lambda i: (0, 0)), row, row, row, row, row],
        out_specs=[tile, tile, tile, tile, tile, perm(d1, D), perm(d2, D),
                   stat, perm(d1, LSE_LANES), perm(d2, LSE_LANES),
                   stat, perm(d1, LSE_LANES), perm(d2, LSE_LANES), acc(LSE_LANES), acc(D)],
        out_shape=[SDS((S, D), BF16), SDS((S, D), F32), SDS((S, D), BF16), SDS((S, D), BF16),
                   SDS((S, D), BF16), pshape(d1, D, BF16), pshape(d2, D, BF16),
                   SDS((S, LSE_LANES), F32), pshape(d1, LSE_LANES, F32), pshape(d2, LSE_LANES, F32),
                   SDS((S, LSE_LANES), F32), pshape(d1, LSE_LANES, F32), pshape(d2, LSE_LANES, F32),
                   SDS((8, LSE_LANES), F32), SDS((8, D), F32)],
        scratch_shapes=[pltpu.VMEM((D // 128, tm, 128), F32), pltpu.VMEM((D // 128, tm, 128), F32),
                        pltpu.VMEM((tm, LSE_LANES), F32), pltpu.VMEM((tm, LSE_LANES), F32),
                        pltpu.VMEM((tm, D), F32), pltpu.VMEM((tm, LSE_LANES), F32),
                        pltpu.VMEM((tm, LSE_LANES), F32), pltpu.VMEM((tm, LSE_LANES), F32),
                        pltpu.VMEM((tm, D), F32)],
        compiler_params=_params(("arbitrary",)),
    )(o0, o1, o2, l0, l1, l2, z2, xh1, target, wbo, bbo, pg0, pb0, pg1, pb1)


def _b_dx1_ln1_bwd(ds2, dz2b, dq, dkv, xh1, rs1, wz, wg, pg0, tm):
    S, D = ds2.shape
    d1, d2 = DILATIONS[1], DILATIONS[2]

    def group_part(dq_blk, dkv_blk, w_ref):
        return (_dot_t(dq_blk, w_ref[:, 0:D]) + _dot_t(dkv_blk[:, 0:D], w_ref[:, D:2 * D])
                + _dot_t(dkv_blk[:, D:2 * D], w_ref[:, 2 * D:3 * D]))

    def body(ds2_ref, dz_ref, dq0_ref, dkv0_ref, dq1_ref, dkv1_ref, dq2_ref, dkv2_ref, xh1_ref, rs1_ref,
             wz_ref, w0_ref, w1_ref, w2_ref, pg0_ref, ds1_ref, ds1b_ref, sums_ref, acc_ref):
        i = pl.program_id(0)
        _to_chunks(acc_ref, ALPHA * ds2_ref[...] + _dot_t(dz_ref[...], wz_ref[...])
                   + group_part(dq0_ref[...], dkv0_ref[...], w0_ref))
        for d, dq_ref, dkv_ref, w_ref in ((d1, dq1_ref, dkv1_ref, w1_ref), (d2, dq2_ref, dkv2_ref, w2_ref)):
            rows = tm // d
            part = group_part(dq_ref[...].reshape(tm, D), dkv_ref[...].reshape(tm, 2 * D), w_ref)
            for r in range(d):
                idx = pl.ds(r, rows, stride=d)
                for cc in range(D // 128):
                    acc_ref[cc, idx, :] = acc_ref[cc, idx, :] + part[r * rows:(r + 1) * rows, cc * 128:(cc + 1) * 128]

        @pl.when(i == 0)
        def _():
            sums_ref[...] = jnp.zeros_like(sums_ref)

        def ln_bwd(rows):
            dx1 = jnp.concatenate([acc_ref[cc, rows, :] for cc in range(D // 128)], axis=1)
            xh1 = xh1_ref[rows, :]
            ds1 = _ln_bwd(dx1 * pg0_ref[...], xh1, rs1_ref[rows, :])
            ds1_ref[rows, :] = ds1
            ds1b_ref[rows, :] = ds1.astype(BF16)
            sums_ref[0:1, :] += _colsum(dx1 * xh1)
            sums_ref[1:2, :] += _colsum(dx1)
            sums_ref[2:3, :] += _colsum(ds1)
        _for_chunks(tm, ln_bwd)

    tile = lambda w: pl.BlockSpec((tm, w), lambda i: (i, 0))
    perm = lambda d, w: pl.BlockSpec((d, tm // d, w), lambda i: (0, i, 0))
    whole = pl.BlockSpec(memory_space=pltpu.VMEM)
    return pl.pallas_call(
        body, name="b_dx1_ln1_bwd", grid=(S // tm,),
        in_specs=[tile(D), tile(D), tile(D), tile(2 * D), perm(d1, D), perm(d1, 2 * D), perm(d2, D),
                  perm(d2, 2 * D), tile(D), tile(1), whole, whole, whole, whole,
                  pl.BlockSpec((1, D), lambda i: (0, 0))],
        out_specs=[tile(D), tile(D), pl.BlockSpec((8, D), lambda i: (0, 0))],
        out_shape=[SDS((S, D), F32), SDS((S, D), BF16), SDS((8, D), F32)],
        scratch_shapes=[pltpu.VMEM((D // 128, tm, 128), F32)],
        compiler_params=_params(("arbitrary",)),
    )(ds2, dz2b, dq[0], dkv[0], dq[1].reshape(d1, S // d1, D), dkv[1].reshape(d1, S // d1, 2 * D),
      dq[2].reshape(d2, S // d2, D), dkv[2].reshape(d2, S // d2, 2 * D), xh1, rs1, wz, wg[0], wg[1], wg[2], pg0)


def _a_gate_bwd(ds1b, h, xhu, rsu, wout, lng, lnb, tm):
    S, D = xhu.shape

    def body(ds_ref, z_ref, xhu_ref, rsu_ref, w_ref, lng_ref, lnb_ref, du1_ref, dzb_ref, sums_ref, dv_ref):
        @pl.when(pl.program_id(0) == 0)
        def _():
            sums_ref[...] = jnp.zeros_like(sums_ref)
        dv_ref[...] = _dot_t(ds_ref[...], w_ref[...])

        def chain(rows):
            dv = dv_ref[rows, :]
            xhu = xhu_ref[rows, :]
            u2 = xhu * lng_ref[...] + lnb_ref[...]
            su = _sigmoid(u2)
            z = z_ref[rows, :]
            sz = _sigmoid(z)
            dz = dv * (u2 * su) * _silu_grad(z, sz)
            du2 = dv * (z * sz) * _silu_grad(u2, su)
            du1 = _ln_bwd(du2 * lng_ref[...], xhu, rsu_ref[rows, :])
            du1_ref[rows, :] = du1
            dzb_ref[rows, :] = dz.astype(BF16)
            sums_ref[0:1, :] += _colsum(du2 * xhu)
            sums_ref[1:2, :] += _colsum(du2)
            sums_ref[2:3, :] += _colsum(du1)
            sums_ref[3:4, :] += _colsum(dz)
        _for_chunks(tm, chain)

    tile = pl.BlockSpec((tm, D), lambda i: (i, 0))
    row = pl.BlockSpec((1, D), lambda i: (0, 0))
    return pl.pallas_call(
        body, name="a_gate_bwd", grid=(S // tm,),
        in_specs=[tile, pl.BlockSpec((tm, D), lambda i: (i, 2)), tile, pl.BlockSpec((tm, 1), lambda i: (i, 0)),
                  pl.BlockSpec((D, D), lambda i: (0, 0)), row, row],
        out_specs=[tile, tile, pl.BlockSpec((8, D), lambda i: (0, 0))],
        out_shape=[SDS((S, D), F32), SDS((S, D), BF16), SDS((8, D), F32)],
        scratch_shapes=[pltpu.VMEM((tm, D), F32)],
        compiler_params=_params(("arbitrary",)),
    )(ds1b, h, xhu, rsu, wout, lng, lnb)


def _a_conv_bwd(du1, h, wdw, parts, tm):
    S, D = du1.shape
    hb = tm // HALO
    last_halo = S // HALO - 1
    n_tiles = S // tm
    n = len(parts)

    def body(du_ref, dun_ref, a_ref, g_ref, ah_ref, gh_ref, wdw_ref, *refs):
        part_refs, (dag_ref, sums_ref, wsum_ref), landed_refs = refs[:n], refs[n:n + 3], refs[n + 3:2 * n + 3]
        dext_ref, ext_ref, dsh_ref, sh_ref, wacc_ref, send_sems, recv_sems = refs[2 * n + 3:]
        scatter = (part_refs, landed_refs, send_sems, recv_sems)
        i = pl.program_id(0)

        @pl.when(i == 0)
        def _():
            _scatter_start(*scatter)
            sums_ref[...] = jnp.zeros_like(sums_ref)
            wsum_ref[...] = jnp.zeros_like(wsum_ref)
            wacc_ref[...] = jnp.zeros_like(wacc_ref)
        dext_ref[0:tm, :] = du_ref[...]
        dext_ref[tm:, :] = jnp.where(i < n_tiles - 1, dun_ref[...], 0.0)
        _fill_shifts(dsh_ref, dext_ref)
        _fill_glu_ext(ext_ref, a_ref, g_ref, ah_ref, gh_ref, i > 0)
        _fill_shifts(sh_ref, ext_ref)
        back = [CONV_WIDTH - 1 - k for k in range(CONV_WIDTH)]
        fwd = [HALO - (CONV_WIDTH - 1) + k for k in range(CONV_WIDTH)]
        for cc in range(D // 128):
            lanes = slice(cc * 128, (cc + 1) * 128)
            hi_lanes = slice(D + cc * 128, D + (cc + 1) * 128)
            sa = jnp.zeros((1, 128), F32)
            sg = jnp.zeros((1, 128), F32)
            for r0 in range(0, tm, CONV_ROWS):
                acc = jnp.zeros((CONV_ROWS, 128), F32)
                for k, win in _tap_windows(dext_ref, dsh_ref, back, r0, CONV_ROWS, lanes):
                    acc = acc + wdw_ref[k:k + 1, lanes] * win
                a = a_ref[r0:r0 + CONV_ROWS, lanes]
                s = _sigmoid(g_ref[r0:r0 + CONV_ROWS, lanes])
                da = acc * s
                dg = acc * a * s * (1.0 - s)
                dag_ref[r0:r0 + CONV_ROWS, lanes] = da.astype(BF16)
                dag_ref[r0:r0 + CONV_ROWS, hi_lanes] = dg.astype(BF16)
                sa = sa + _colsum(da)
                sg = sg + _colsum(dg)
                du = du_ref[r0:r0 + CONV_ROWS, lanes]
                for k, win in _tap_windows(ext_ref, sh_ref, fwd, r0, CONV_ROWS, lanes):
                    p = du * win
                    fold = p[0:8]
                    for q in range(8, CONV_ROWS, 8):
                        fold = fold + p[q:q + 8]
                    wacc_ref[k, :, lanes] += fold
            sums_ref[0:1, lanes] += sa
            sums_ref[1:2, lanes] += sg

        @pl.when(i == n_tiles - 1)
        def _():
            for k in range(CONV_WIDTH):
                wsum_ref[k:k + 1, :] = _colsum(wacc_ref[k])
            _scatter_finish(*scatter)

    tile = lambda c: pl.BlockSpec((tm, D), lambda i, c=c: (i, c))
    halo = lambda c: pl.BlockSpec((HALO, D), lambda i, c=c: (jnp.maximum(i * hb - 1, 0), c))
    outs = pl.pallas_call(
        body, name="a_conv_bwd", grid=(n_tiles,),
        in_specs=[tile(0), pl.BlockSpec((HALO, D), lambda i: (jnp.minimum((i + 1) * hb, last_halo), 0)),
                  tile(0), tile(1), halo(0), halo(1), pl.BlockSpec((HALO, D), lambda i: (0, 0))] + _hbm_specs(n),
        out_specs=[pl.BlockSpec((tm, 2 * D), lambda i: (i, 0)), pl.BlockSpec((8, D), lambda i: (0, 0)),
                   pl.BlockSpec((HALO, D), lambda i: (0, 0))] + _hbm_specs(n),
        out_shape=[SDS((S, 2 * D), BF16), SDS((8, D), F32), SDS((HALO, D), F32)]
        + [SDS((3,) + p.shape[1:], p.dtype) for p in parts],
        scratch_shapes=[pltpu.VMEM((tm + HALO, D), F32), pltpu.VMEM((HALO + tm, D), F32),
                        pltpu.VMEM((7, HALO + tm - 8, D), F32), pltpu.VMEM((7, HALO + tm - 8, D), F32),
                        pltpu.VMEM((HALO, 8, D), F32)] + _scatter_scratch(n),
        compiler_params=_params(("arbitrary",)),
    )(du1, du1, h, h, h, h, wdw, *parts)
    return outs[0], outs[1], outs[2], outs[3:]


def _a_dx(ds1, dag, dzb, w_in, tm):
    S, D = ds1.shape

    def body(ds_ref, dag_ref, dz_ref, w_ref, o_ref):
        o_ref[...] = (ALPHA * ds_ref[...] + _dot_t(dag_ref[...], w_ref[:, 0:2 * D])
                      + _dot_t(dz_ref[...], w_ref[:, 2 * D:3 * D]))

    tile = lambda w: pl.BlockSpec((tm, w), lambda i: (i, 0))
    return pl.pallas_call(
        body, name="a_dx", grid=(S // tm,),
        in_specs=[tile(D), tile(2 * D), tile(D), pl.BlockSpec(memory_space=pltpu.VMEM)],
        out_specs=tile(D), out_shape=SDS((S, D), F32),
        compiler_params=_params(("parallel",)),
    )(ds1, dag, dzb, w_in)


def _halves(w):
    return w.reshape(2, w.shape[0] // 2, w.shape[1])


def _unstack_cols(w4):
    return jnp.transpose(w4, (1, 0, 2)).reshape(w4.shape[1], N_CHIPS * w4.shape[2])


def _stack_cols(w):
    D, n = w.shape
    return jnp.transpose(w.reshape(D, N_CHIPS, n // N_CHIPS), (1, 0, 2))


def _pack_rows(rows, width):
    slab = jnp.concatenate([r.reshape(-1, width) for r in rows], axis=0)
    return jnp.pad(slab, ((0, SMALL_ROWS - slab.shape[0]), (0, 0)))


def kernel(x, a_w_in, a_b_in, a_w_dw, a_b_dw, a_ln_g, a_ln_b, a_w_out, a_b_out, kv_w, b_w_in, b_w_out, b_b_out, post_ln_g, post_ln_b, loss_target, m_a_w_in, m_a_b_in, m_a_w_dw, m_a_b_dw, m_a_ln_g, m_a_ln_b, m_a_w_out, m_a_b_out, m_kv_w, m_b_w_in, m_b_w_out, m_b_b_out, m_post_ln_g, m_post_ln_b, v_a_w_in, v_a_b_in, v_a_w_dw, v_a_b_dw, v_a_ln_g, v_a_ln_b, v_a_w_out, v_a_b_out, v_kv_w, v_b_w_in, v_b_w_out, v_b_b_out, v_post_ln_g, v_post_ln_b):
    S, D = x.shape[1], x.shape[2]
    dq4 = D // N_CHIPS
    tm = 256
    tm_mm = min(S, 1024)
    x2 = x.reshape(S, D)
    target = loss_target.reshape(S, D)
    jchip = 2 * lax.axis_index("x") + lax.axis_index("y")

    big_local = [a_w_in[0], kv_w, b_w_in[0], a_w_out[0], b_w_out[0]]
    small_local = _pack_rows([a_b_in.reshape(3, dq4), jnp.pad(a_w_dw[0], ((0, 1), (0, 0))), a_b_dw, a_ln_g,
                              a_ln_b, a_b_out], dq4)
    wire = [_halves(w.astype(BF16)) for w in big_local]
    whole = lambda g: g.reshape((N_CHIPS, 2 * g.shape[2], g.shape[3]))
    gathered = _all_gather_chips([wire[0], wire[3], _halves(small_local)])
    w_in4 = whole(gathered[0])
    w_in_a = _unstack_cols(w_in4)
    w_out_a = gathered[1].reshape(D, D)
    small = jnp.transpose(gathered[2].reshape(N_CHIPS, SMALL_ROWS, dq4), (1, 0, 2))
    b_in_full = jnp.transpose(small[0:3], (1, 0, 2)).reshape(1, 3 * D)
    wdw_full = small[3:3 + HALO].reshape(HALO, D)
    bdw_full, lng_full, lnb_full, bout_a_full = [small[35 + q].reshape(1, D) for q in range(4)]
    pg0, pg1 = post_ln_g[0:1], post_ln_g[1:2]
    pb0, pb1 = post_ln_b[0:1], post_ln_b[1:2]

    h, xb, later = _a_in_proj(x2, w_in4, b_in_full, [wire[1], wire[2], wire[4]], tm_mm)
    kv_full = _unstack_cols(whole(later[0]))
    b_in4 = whole(later[1])
    w_out_b = later[2].reshape(D, D)
    w_z = b_in4[3]
    w_g = [jnp.concatenate([b_in4[g], kv_full[:, g * D:(g + 1) * D], kv_full[:, (3 + g) * D:(4 + g) * D]], axis=1)
           for g in range(3)]
    xhu, rsu, vb, xh1, rs1, x1b, x1p1, x1p2 = _a_conv_out(
        h, x2, wdw_full, bdw_full, lng_full, lnb_full, w_out_a, bout_a_full, pg0, pb0, tm)
    x1g = [x1b, x1p1.reshape(S, D), x1p2.reshape(S, D)]
    qkv = [_mm_nn("b_qkv_g%d" % g, x1g[g], w_g[g], BF16, tm_mm, scale_first_tile=HEAD_DIM ** -0.5)
           for g in range(3)]
    z2 = _mm_nn("b_gate_proj", x1b, w_z, F32, tm_mm)
    og, lg = zip(*[_attn_fwd(g, qkv[g], D) for g in range(3)])
    d1, d2 = DILATIONS[1], DILATIONS[2]
    (v2b, ds2, ds2b, dz2b, da0, da1, da2, ls0, ls1, ls2, dl0, dl1, dl2, loss_acc, sums_b) = _b_merge_out_loss(
        og[0], og[1].reshape(d1, S // d1, D), og[2].reshape(d2, S // d2, D),
        lg[0], lg[1].reshape(d1, S // d1, LSE_LANES), lg[2].reshape(d2, S // d2, LSE_LANES),
        z2, xh1, target, w_out_b, b_b_out, pg0, pb0, pg1, pb1, tm)

    das = [da0, da1.reshape(S, D), da2.reshape(S, D)]
    lss = [ls0, ls1.reshape(S, LSE_LANES), ls2.reshape(S, LSE_LANES)]
    dls = [dl0, dl1.reshape(S, LSE_LANES), dl2.reshape(S, LSE_LANES)]
    dq, dkv = zip(*[_attn_bwd(g, qkv[g], das[g], lss[g], dls[g], D) for g in range(3)])
    ds1, ds1b, sums_1 = _b_dx1_ln1_bwd(ds2, dz2b, dq, dkv, xh1, rs1, w_z, w_g, pg0, tm)
    du1, dzab, sums_a = _a_gate_bwd(ds1b, h, xhu, rsu, w_out_a, lng_full, lnb_full, tm)

    def by_chip_cols(gw):
        s4 = _stack_cols(gw)
        return s4.reshape(N_CHIPS, 2, D // 2, s4.shape[2])

    def by_chip_rows(gw):
        return gw.reshape(N_CHIPS, 2, D // 8, D)

    core = lax.axis_index("c").astype(jnp.int32).reshape(1)
    chip = jchip.astype(jnp.int32).reshape(1)

    g_w_out_b = _mm_tn("dw_b_out", v2b, ds2b, tm_mm)
    g_q = [_mm_tn("dw_b_q_g%d" % g, x1g[g], dq[g], tm_mm) for g in range(3)]
    g_z = _mm_tn("dw_b_z", x1b, dz2b, tm_mm)
    g_kvg = [_mm_tn("dw_kv_g%d" % g, x1g[g], dkv[g], tm_mm) for g in range(3)]
    g_kv = jnp.concatenate([t[:, :D] for t in g_kvg] + [t[:, D:] for t in g_kvg], axis=1)
    grads_b = [by_chip_cols(g_kv), jnp.stack(g_q + [g_z]).reshape(N_CHIPS, 2, D // 2, D), by_chip_rows(g_w_out_b)]
    parts_b, wire_b = _pair_sum(grads_b, _pair_exchange(grads_b, "b"), core, "b")
    dag, sums_c, wsum, landed_b = _a_conv_bwd(du1, h, wdw_full, wire_b, tm)
    own_b = _chip_sum(parts_b, landed_b, chip, "b")
    other_b = _pair_share(own_b, "b")
    grad_x = _a_dx(ds1, dag, dzab, w_in_a, tm)

    g_w_in = jnp.concatenate([_mm_tn("dw_a_in_ag", xb, dag, tm_mm), _mm_tn("dw_a_in_z", xb, dzab, tm_mm)], axis=1)
    g_w_out_a = _mm_tn("dw_a_out", vb, ds1b, tm_mm)
    grads_a = [by_chip_cols(g_w_in), by_chip_rows(g_w_out_a)]
    parts_a, wire_a = _pair_sum(grads_a, _pair_exchange(grads_a, "a"), core, "a")
    own_a = _chip_sum(parts_a, _chip_scatter(wire_a, "a"), chip, "a")
    other_a = _pair_share(own_a, "a")
    own_half = [own_a[0], own_b[0], own_b[1], own_a[1], own_b[2]]
    other_half = [other_a[0], other_b[0], other_b[1], other_a[1], other_b[2]]

    small_grads = _pack_rows([sums_c[0:1], sums_c[1:2], sums_a[3:4], wsum, sums_a[2:3], sums_a[0:1], sums_a[1:2],
                              sums_1[2:3], sums_b[2:3], sums_1[0:1], sums_b[0:1], sums_1[1:2], sums_b[1:2]], D)
    small_sum = _sum_devices(_gather_all_devices(small_grads))
    loss = lax.psum(loss_acc[0, 0], ("x", "y", "c"))

    big_m = [m_a_w_in[0], m_kv_w, m_b_w_in[0], m_a_w_out[0], m_b_w_out[0]]
    big_v = [v_a_w_in[0], v_kv_w, v_b_w_in[0], v_a_w_out[0], v_b_w_out[0]]
    shards, big_delta, big_new_m, big_new_v = [
        [a.reshape(2 * a.shape[1], a.shape[2]) for a in group] for group in _adamw_halves(
            [_halves(w) for w in big_local], own_half, other_half, [_halves(m) for m in big_m],
            [_halves(v) for v in big_v], core)]

    def chip_cols(rows):
        return lax.dynamic_slice_in_dim(rows, jchip * dq4, dq4, axis=1)

    g_b_in = lax.dynamic_slice_in_dim(small_sum[0:3].reshape(1, 3 * D), jchip * 3 * dq4, 3 * dq4, axis=1)
    small_g = [g_b_in, chip_cols(small_sum[3:3 + CONV_WIDTH]), chip_cols(small_sum[35:36]), chip_cols(small_sum[36:37]),
               chip_cols(small_sum[37:38]), chip_cols(small_sum[38:39]), small_sum[39:40], small_sum[40:42],
               small_sum[42:44]]
    small_w = [a_b_in, a_w_dw[0], a_b_dw, a_ln_g, a_ln_b, a_b_out, b_b_out, post_ln_g, post_ln_b]
    small_m = [m_a_b_in, m_a_w_dw[0], m_a_b_dw, m_a_ln_g, m_a_ln_b, m_a_b_out, m_b_b_out, m_post_ln_g, m_post_ln_b]
    small_v = [v_a_b_in, v_a_w_dw[0], v_a_b_dw, v_a_ln_g, v_a_ln_b, v_a_b_out, v_b_b_out, v_post_ln_g, v_post_ln_b]
    small_delta, small_new_m, small_new_v = _adamw("adamw_small", small_w, small_g, small_m, small_v, 1)

    def ordered(big, sm):
        return (big[0][None], sm[0], sm[1][None], sm[2], sm[3], sm[4], big[3][None], sm[5], big[1], big[2][None],
                big[4][None], sm[6], sm[7], sm[8])

    return (loss, grad_x.reshape(1, S, D), *ordered(shards, small_g), *ordered(big_delta, small_delta),
            *ordered(big_new_m, small_new_m), *ordered(big_new_v, small_new_v))
```

```python
import functools

import numpy as np
import jax
import jax.numpy as jnp
from jax import lax
from jax.experimental import pallas as pl
from jax.experimental.pallas import tpu as pltpu

F32 = jnp.float32
BF16 = jnp.bfloat16
MESH = pl.DeviceIdType.MESH
SDS = jax.ShapeDtypeStruct

HEAD_DIM = 64
BLOCK = 128
DILATIONS = (1, 4, 16)
ALIBI_MAX_EXP = 8.0
CONV_WIDTH = 31
HALO = 32
CONV_ROWS = 128
LSE_LANES = 128
DEPTH = 2
ALPHA = (2.0 * DEPTH) ** 0.25
LN_EPS = 1e-5
ADAM_LR = 0.001
ADAM_B1 = 0.9
ADAM_B2 = 0.999
ADAM_EPS = 1e-08
ADAM_WD = 0.01
ADAM_STEP = 10
N_CHIPS = 4
N_DEV = 8
VMEM_LIMIT = 56 * 2 ** 20
SMALL_ROWS = 48


def _params(sem=None):
    return pltpu.CompilerParams(dimension_semantics=sem, vmem_limit_bytes=VMEM_LIMIT)


def _sigmoid(x):
    return 1.0 / (1.0 + jnp.exp(-x))


def _silu_grad(x, s):
    return s * (1.0 + x * (1.0 - s))


def _ln_fwd(x):
    mu = jnp.mean(x, axis=-1, keepdims=True)
    xc = x - mu
    var = jnp.mean(xc * xc, axis=-1, keepdims=True)
    rstd = lax.rsqrt(var + LN_EPS)
    return xc * rstd, rstd


def _ln_bwd(dxhat, xhat, rstd):
    m1 = jnp.mean(dxhat, axis=-1, keepdims=True)
    m2 = jnp.mean(dxhat * xhat, axis=-1, keepdims=True)
    return rstd * (dxhat - m1 - xhat * m2)


def _dot(a, b):
    return jnp.dot(a, b, preferred_element_type=F32)


def _dot_t(a, b):
    return lax.dot_general(a, b, (((1,), (1,)), ((), ())), preferred_element_type=F32)


def _tdot(a, b):
    return lax.dot_general(a, b, (((0,), (0,)), ((), ())), preferred_element_type=F32)


def _colsum(x):
    return jnp.sum(x, axis=0, keepdims=True)


def _slopes(n_heads):
    return [float(np.float32(2.0 ** (-ALIBI_MAX_EXP * (h + 1) / n_heads))) for h in range(n_heads)]


def _to_chunks(chunks_ref, x):
    for cc in range(chunks_ref.shape[0]):
        chunks_ref[cc] = x[:, cc * 128:(cc + 1) * 128]


def _deinterleave(chunks_ref, out_ref, d, dtype):
    rows = chunks_ref.shape[1] // d
    for r in range(d):
        for cc in range(chunks_ref.shape[0]):
            out_ref[r, :, cc * 128:(cc + 1) * 128] = chunks_ref[cc, pl.ds(r, rows, stride=d), :].astype(dtype)


def _interleave(in_ref, chunks_ref, d):
    rows = chunks_ref.shape[1] // d
    for r in range(d):
        for cc in range(chunks_ref.shape[0]):
            chunks_ref[cc, pl.ds(r, rows, stride=d), :] = in_ref[r, :, cc * 128:(cc + 1) * 128]


def _hbm_specs(n):
    return [pl.BlockSpec(memory_space=pl.ANY)] * n


def _position():
    x, y, c = lax.axis_index("x"), lax.axis_index("y"), lax.axis_index("c")
    return x, y, c


def _gather_stages(ins, outs, send_sems, recv_sems, local_sems):
    n = len(ins)

    def plan():
        x, y, c = _position()
        j = 2 * x + y
        me, sibling = (x, y, c), (x, y, 1 - c)
        chips = [(1 - x, y), (x, 1 - y), (1 - x, 1 - y)]

        def copy(i, k, src, dst, to):
            return pltpu.make_async_remote_copy(
                src_ref=src, dst_ref=dst, send_sem=send_sems.at[i, k], recv_sem=recv_sems.at[i, k],
                device_id=to, device_id_type=MESH)

        local = [pltpu.make_async_copy(ins[i], outs[i].at[j], local_sems.at[i]) for i in range(n)]
        first, landing, passed, passed_landing = [], [], [], []
        for i in range(n):
            for k, chip in enumerate(chips):
                pj = 2 * chip[0] + chip[1]
                first.append(copy(i, k, ins[i].at[c], outs[i].at[j, c], (*chip, c)))
                landing.append(copy(i, k, ins[i].at[c], outs[i].at[pj, c], me))
                passed.append(copy(i, 3 + k, outs[i].at[pj, c], outs[i].at[pj, c], sibling))
                passed_landing.append(copy(i, 3 + k, ins[i].at[c], outs[i].at[pj, 1 - c], me))
        return local, first, landing, passed, passed_landing

    def start():
        local, first, _, _, _ = plan()
        for cp in local + first:
            cp.start()

    def forward():
        _, _, landing, passed, _ = plan()
        for arrived, cp in zip(landing, passed):
            arrived.wait_recv()
            cp.start()

    def finish():
        local, first, _, passed, passed_landing = plan()
        for cp in passed_landing:
            cp.wait_recv()
        for cp in first + passed:
            cp.wait_send()
        for cp in local:
            cp.wait()

    return start, forward, finish


def _gather_scratch(n):
    return [pltpu.SemaphoreType.DMA((n, 6)), pltpu.SemaphoreType.DMA((n, 6)), pltpu.SemaphoreType.DMA((n,))]


def _all_gather_chips(shards):
    n = len(shards)

    def body(*refs):
        for stage in _gather_stages(refs[:n], refs[n:2 * n], *refs[2 * n:]):
            stage()

    return pl.pallas_call(
        body, name="all_gather_chips",
        out_shape=[SDS((N_CHIPS,) + s.shape, s.dtype) for s in shards],
        in_specs=_hbm_specs(n), out_specs=_hbm_specs(n), scratch_shapes=_gather_scratch(n),
    )(*shards)


def _pair_exchange(grads, tag):
    n = len(grads)

    def body(*refs):
        ins, outs = refs[:n], refs[n:2 * n]
        send_sems, recv_sems = refs[2 * n:]
        x, y, c = _position()
        sibling = (x, y, 1 - c)
        remote = []
        for i in range(n):
            for j in range(N_CHIPS):
                remote.append(pltpu.make_async_remote_copy(
                    src_ref=ins[i].at[j, 1 - c], dst_ref=outs[i].at[j],
                    send_sem=send_sems.at[i, j], recv_sem=recv_sems.at[i, j],
                    device_id=sibling, device_id_type=MESH))
        for cp in remote:
            cp.start()
        for cp in remote:
            cp.wait_recv()
        for cp in remote:
            cp.wait_send()

    return pl.pallas_call(
        body, name="grad_pair_exchange_" + tag,
        out_shape=[SDS((N_CHIPS,) + g.shape[2:], g.dtype) for g in grads],
        in_specs=_hbm_specs(n), out_specs=_hbm_specs(n),
        scratch_shapes=[pltpu.SemaphoreType.DMA((n, N_CHIPS)), pltpu.SemaphoreType.DMA((n, N_CHIPS))],
    )(*grads)


def _scatter_copies(ins, outs, send_sems, recv_sems):
    x, y, c = _position()
    chips = [(1 - x, y), (x, 1 - y), (1 - x, 1 - y)]
    return [pltpu.make_async_remote_copy(
        src_ref=ins[i].at[2 * chip[0] + chip[1]], dst_ref=outs[i].at[k],
        send_sem=send_sems.at[i, k], recv_sem=recv_sems.at[i, k],
        device_id=(*chip, c), device_id_type=MESH) for i in range(len(ins)) for k, chip in enumerate(chips)]


def _scatter_start(*refs):
    for cp in _scatter_copies(*refs):
        cp.start()


def _scatter_finish(*refs):
    copies = _scatter_copies(*refs)
    for cp in copies:
        cp.wait_recv()
    for cp in copies:
        cp.wait_send()


def _scatter_scratch(n):
    return [pltpu.SemaphoreType.DMA((n, 3)), pltpu.SemaphoreType.DMA((n, 3))]


def _chip_scatter(parts, tag):
    n = len(parts)

    def body(*refs):
        args = (refs[:n], refs[n:2 * n], *refs[2 * n:])
        _scatter_start(*args)
        _scatter_finish(*args)

    return pl.pallas_call(
        body, name="grad_chip_scatter_" + tag,
        out_shape=[SDS((3,) + p.shape[1:], p.dtype) for p in parts],
        in_specs=_hbm_specs(n), out_specs=_hbm_specs(n), scratch_shapes=_scatter_scratch(n),
    )(*parts)


def _pair_share(halves, tag):
    n = len(halves)

    def body(*refs):
        ins, outs = refs[:n], refs[n:2 * n]
        send_sems, recv_sems = refs[2 * n:]
        x, y, c = _position()
        remote = [pltpu.make_async_remote_copy(
            src_ref=ins[i], dst_ref=outs[i], send_sem=send_sems.at[i], recv_sem=recv_sems.at[i],
            device_id=(x, y, 1 - c), device_id_type=MESH) for i in range(n)]
        for cp in remote:
            cp.start()
        for cp in remote:
            cp.wait_recv()
        for cp in remote:
            cp.wait_send()

    return pl.pallas_call(
        body, name="grad_pair_share_" + tag,
        out_shape=[SDS(h.shape, h.dtype) for h in halves],
        in_specs=_hbm_specs(n), out_specs=_hbm_specs(n),
        scratch_shapes=[pltpu.SemaphoreType.DMA((n,)), pltpu.SemaphoreType.DMA((n,))],
    )(*halves)


def _gather_all_devices(slab):
    def body(in_ref, out_ref, send_sems, recv_sems, local_sem):
        x, y, c = _position()
        me = 4 * x + 2 * y + c
        local = pltpu.make_async_copy(in_ref, out_ref.at[me], local_sem)
        local.start()
        remote, landing = [], []
        for mask in range(1, N_DEV):
            px, py, pc = x ^ (mask >> 2), y ^ ((mask >> 1) & 1), c ^ (mask & 1)
            peer = 4 * px + 2 * py + pc
            remote.append(pltpu.make_async_remote_copy(
                src_ref=in_ref, dst_ref=out_ref.at[me], send_sem=send_sems.at[mask - 1],
                recv_sem=recv_sems.at[mask - 1], device_id=(px, py, pc), device_id_type=MESH))
            landing.append(pltpu.make_async_remote_copy(
                src_ref=in_ref, dst_ref=out_ref.at[peer], send_sem=send_sems.at[mask - 1],
                recv_sem=recv_sems.at[mask - 1], device_id=(px, py, pc), device_id_type=MESH))
        for cp in remote:
            cp.start()
        for cp in landing:
            cp.wait_recv()
        for cp in remote:
            cp.wait_send()
        local.wait()

    return pl.pallas_call(
        body, name="small_grad_gather",
        out_shape=SDS((N_DEV,) + slab.shape, slab.dtype),
        in_specs=_hbm_specs(1), out_specs=pl.BlockSpec(memory_space=pl.ANY),
        scratch_shapes=[pltpu.SemaphoreType.DMA((N_DEV - 1,)), pltpu.SemaphoreType.DMA((N_DEV - 1,)),
                        pltpu.SemaphoreType.DMA],
    )(slab)


def _row_splits(arrays):
    return min(a.shape[-2] for a in arrays) // 16


def _pair_sum(grads, recvd, core, tag):
    n = len(grads)
    splits = _row_splits(recvd)

    def body(core_ref, *refs):
        for i in range(n):
            s = refs[i][...] + refs[n + i][...]
            refs[2 * n + i][...] = s
            refs[3 * n + i][...] = s.astype(BF16)

    mine = [pl.BlockSpec((N_CHIPS, None, r.shape[1] // splits, r.shape[2]), lambda s, core: (0, core[0], s, 0))
            for r in recvd]
    block = [pl.BlockSpec((N_CHIPS, r.shape[1] // splits, r.shape[2]), lambda s, core: (0, s, 0)) for r in recvd]
    outs = pl.pallas_call(
        body, name="grad_pair_sum_" + tag,
        grid_spec=pltpu.PrefetchScalarGridSpec(
            num_scalar_prefetch=1, grid=(splits,), in_specs=mine + block, out_specs=block + block),
        out_shape=[SDS(r.shape, F32) for r in recvd] + [SDS(r.shape, BF16) for r in recvd],
        compiler_params=_params(("parallel",)),
    )(core, *grads, *recvd)
    return outs[:n], outs[n:]


def _chip_sum(parts, landed, chip, tag):
    n = len(parts)
    splits = _row_splits(landed)

    def body(chip_ref, *refs):
        for i in range(n):
            acc = refs[i][...]
            for k in range(3):
                acc = acc + refs[n + i][k].astype(F32)
            refs[2 * n + i][...] = acc

    rows = lambda p: p.shape[1] // splits
    return pl.pallas_call(
        body, name="grad_chip_sum_" + tag,
        grid_spec=pltpu.PrefetchScalarGridSpec(
            num_scalar_prefetch=1, grid=(splits,),
            in_specs=[pl.BlockSpec((None, rows(p), p.shape[2]), lambda s, chip: (chip[0], s, 0)) for p in parts]
            + [pl.BlockSpec((3, rows(p), p.shape[2]), lambda s, chip: (0, s, 0)) for p in parts],
            out_specs=[pl.BlockSpec((rows(p), p.shape[2]), lambda s, chip: (s, 0)) for p in parts]),
        out_shape=[SDS(p.shape[1:], F32) for p in parts],
        compiler_params=_params(("parallel",)),
    )(chip, *parts, *landed)


def _adamw_math(w, g, m, v):
    m = ADAM_B1 * m + (1.0 - ADAM_B1) * g
    v = ADAM_B2 * v + (1.0 - ADAM_B2) * (g * g)
    m_hat = m / (1.0 - ADAM_B1 ** ADAM_STEP)
    v_hat = v / (1.0 - ADAM_B2 ** ADAM_STEP)
    delta = -ADAM_LR * (m_hat / (jnp.sqrt(v_hat) + ADAM_EPS) + ADAM_WD * w)
    return delta, m, v


def _adamw(name, ws, gs, ms, vs, splits):
    n = len(ws)

    def body(*refs):
        for i in range(n):
            w, g, m, v = (refs[q * n + i][...] for q in range(4))
            delta, m, v = _adamw_math(w, g, m, v)
            refs[4 * n + i][...] = delta
            refs[5 * n + i][...] = m
            refs[6 * n + i][...] = v

    def spec(a):
        if splits == 1:
            return pl.BlockSpec(a.shape, lambda s: (0, 0))
        return pl.BlockSpec((a.shape[0] // splits, a.shape[1]), lambda s: (s, 0))

    specs = [spec(a) for a in ws]
    outs = pl.pallas_call(
        body, name=name, grid=(splits,),
        in_specs=specs * 4, out_specs=specs * 3,
        out_shape=[SDS(a.shape, F32) for a in ws] * 3,
        compiler_params=_params(("parallel",)),
    )(*ws, *gs, *ms, *vs)
    return outs[:n], outs[n:2 * n], outs[2 * n:]


def _adamw_halves(ws, own, other, ms, vs, core):
    n = len(ws)
    splits = _row_splits(own)

    def body(core_ref, *refs):
        mine = pl.program_id(0) == core_ref[0]
        for i in range(n):
            g = jnp.where(mine, refs[n + i][...], refs[2 * n + i][...])
            delta, m, v = _adamw_math(refs[i][...], g, refs[3 * n + i][...], refs[4 * n + i][...])
            refs[5 * n + i][...] = g
            refs[6 * n + i][...] = delta
            refs[7 * n + i][...] = m
            refs[8 * n + i][...] = v

    rows = lambda a: a.shape[0] // splits
    half = [pl.BlockSpec((None, rows(a), a.shape[1]), lambda hh, s, core: (hh, s, 0)) for a in own]
    flat = [pl.BlockSpec((rows(a), a.shape[1]), lambda hh, s, core: (s, 0)) for a in own]
    outs = pl.pallas_call(
        body, name="adamw_big",
        grid_spec=pltpu.PrefetchScalarGridSpec(
            num_scalar_prefetch=1, grid=(2, splits), in_specs=half + flat + flat + half + half, out_specs=half * 4),
        out_shape=[SDS(w.shape, F32) for w in ws] * 4,
        compiler_params=_params(("parallel", "parallel")),
    )(core, *ws, *own, *other, *ms, *vs)
    return outs[:n], outs[n:2 * n], outs[2 * n:3 * n], outs[3 * n:]


def _sum_devices(slabs):
    def body(in_ref, out_ref):
        acc = in_ref[0]
        for k in range(1, N_DEV):
            acc = acc + in_ref[k]
        out_ref[...] = acc

    return pl.pallas_call(
        body, name="small_grad_sum", out_shape=SDS(slabs.shape[1:], F32),
    )(slabs)


def _mm_nn(name, a, w, out_dtype, tm, scale_first_tile=None):
    S, K = a.shape
    N = w.shape[1]
    tn = K

    def body(a_ref, w_ref, o_ref):
        acc = _dot(a_ref[...], w_ref[...])
        if scale_first_tile is not None:
            acc = acc * jnp.where(pl.program_id(1) == 0, scale_first_tile, 1.0)
        o_ref[...] = acc.astype(out_dtype)

    return pl.pallas_call(
        body, name=name, grid=(S // tm, N // tn),
        in_specs=[pl.BlockSpec((tm, K), lambda i, t: (i, 0)), pl.BlockSpec((K, tn), lambda i, t: (0, t))],
        out_specs=pl.BlockSpec((tm, tn), lambda i, t: (i, t)),
        out_shape=SDS((S, N), out_dtype),
        compiler_params=_params(("parallel", "arbitrary")),
    )(a, w)


def _mm_tn(name, a, b, tk):
    S, M = a.shape
    N = b.shape[1]
    tn = M

    def body(a_ref, b_ref, o_ref):
        @pl.when(pl.program_id(1) == 0)
        def _():
            o_ref[...] = jnp.zeros_like(o_ref)
        o_ref[...] += _tdot(a_ref[...], b_ref[...])

    return pl.pallas_call(
        body, name=name, grid=(N // tn, S // tk),
        in_specs=[pl.BlockSpec((tk, M), lambda t, k: (k, 0)), pl.BlockSpec((tk, tn), lambda t, k: (k, t))],
        out_specs=pl.BlockSpec((M, tn), lambda t, k: (0, t)),
        out_shape=SDS((M, N), F32),
        compiler_params=_params(("parallel", "arbitrary")),
    )(a, b)


def _a_in_proj(x, w4, b_in, later_shards, tm):
    S, D = x.shape
    nj = w4.shape[2]
    n = len(later_shards)
    steps = S // tm

    def body(x_ref, w_ref, b_ref, *refs):
        shard_refs, (h_ref, xb_ref), gathered_refs = refs[:n], refs[n:n + 2], refs[n + 2:2 * n + 2]
        start, forward, finish = _gather_stages(shard_refs, gathered_refs, *refs[2 * n + 2:])
        i, t = pl.program_id(0), pl.program_id(1)
        pl.when((i == 0) & (t == 0))(start)
        pl.when((i == steps // 2) & (t == 0))(forward)
        xb = x_ref[...].astype(BF16)

        @pl.when(t == 0)
        def _():
            xb_ref[...] = xb
        h_ref[...] = _dot(xb, w_ref[...]) + b_ref[...]
        pl.when((i == steps - 1) & (t == N_CHIPS - 1))(finish)

    outs = pl.pallas_call(
        body, name="a_in_proj", grid=(steps, N_CHIPS),
        in_specs=[pl.BlockSpec((tm, D), lambda i, t: (i, 0)),
                  pl.BlockSpec((None, D, nj), lambda i, t: (t, 0, 0)),
                  pl.BlockSpec((1, nj), lambda i, t: (0, t))] + _hbm_specs(n),
        out_specs=[pl.BlockSpec((tm, nj), lambda i, t: (i, t)), pl.BlockSpec((tm, D), lambda i, t: (i, 0))]
        + _hbm_specs(n),
        out_shape=[SDS((S, N_CHIPS * nj), F32), SDS((S, D), BF16)]
        + [SDS((N_CHIPS,) + s.shape, s.dtype) for s in later_shards],
        scratch_shapes=_gather_scratch(n),
        compiler_params=_params(("arbitrary", "arbitrary")),
    )(x, w4, b_in, *later_shards)
    return outs[0], outs[1], outs[2:]


def _fill_glu_ext(ext_ref, a_ref, g_ref, ah_ref, gh_ref, has_prev):
    u0h = ah_ref[...] * _sigmoid(gh_ref[...])
    ext_ref[0:HALO, :] = jnp.where(has_prev, u0h, 0.0)
    ext_ref[HALO:, :] = a_ref[...] * _sigmoid(g_ref[...])


def _fill_shifts(shift_ref, ext_ref):
    for s in range(1, 8):
        shift_ref[s - 1] = ext_ref[s:s + shift_ref.shape[1], :]


def _tap_windows(ext_ref, shift_ref, starts, r0, rows, lanes):
    for s in range(8):
        taps = [(k, st) for k, st in enumerate(starts) if st % 8 == s]
        if not taps:
            continue
        lo = min(st for _, st in taps)
        hi = max(st for _, st in taps)
        if s == 0:
            win = ext_ref[r0 + lo:r0 + hi + rows, lanes]
        else:
            win = shift_ref[s - 1, r0 + lo - s:r0 + hi - s + rows, lanes]
        for k, st in taps:
            yield k, win[st - lo:st - lo + rows]


def _a_conv_out(h, x, wdw, bdw, lng, lnb, wout, bout, pg, pb, tm):
    S, D = x.shape
    hb = tm // HALO
    d1, d2 = DILATIONS[1], DILATIONS[2]

    def body(a_ref, g_ref, z_ref, ah_ref, gh_ref, x_ref, wdw_ref, bdw_ref, lng_ref, lnb_ref, wout_ref,
             bout_ref, pg_ref, pb_ref, xhu_ref, rsu_ref, vb_ref, xh1_ref, rs1_ref, x1b_ref, x1p1_ref,
             x1p2_ref, ext_ref, u1_ref, x1_ref, sh_ref):
        i = pl.program_id(0)
        _fill_glu_ext(ext_ref, a_ref, g_ref, ah_ref, gh_ref, i > 0)
        _fill_shifts(sh_ref, ext_ref)
        starts = [HALO - (CONV_WIDTH - 1) + k for k in range(CONV_WIDTH)]
        for cc in range(D // 128):
            lanes = slice(cc * 128, (cc + 1) * 128)
            for r0 in range(0, tm, CONV_ROWS):
                acc = jnp.broadcast_to(bdw_ref[:, lanes], (CONV_ROWS, 128))
                for k, win in _tap_windows(ext_ref, sh_ref, starts, r0, CONV_ROWS, lanes):
                    acc = acc + wdw_ref[k:k + 1, lanes] * win
                u1_ref[r0:r0 + CONV_ROWS, lanes] = acc
        xhu, rsu = _ln_fwd(u1_ref[...])
        xhu_ref[...] = xhu
        rsu_ref[...] = rsu
        u2 = xhu * lng_ref[...] + lnb_ref[...]
        z = z_ref[...]
        v = (u2 * _sigmoid(u2)) * (z * _sigmoid(z))
        vb = v.astype(BF16)
        vb_ref[...] = vb
        s1 = ALPHA * x_ref[...] + _dot(vb, wout_ref[...]) + bout_ref[...]
        xh1, rs1 = _ln_fwd(s1)
        xh1_ref[...] = xh1
        rs1_ref[...] = rs1
        x1 = xh1 * pg_ref[...] + pb_ref[...]
        x1b_ref[...] = x1.astype(BF16)
        _to_chunks(x1_ref, x1)
        _deinterleave(x1_ref, x1p1_ref, d1, BF16)
        _deinterleave(x1_ref, x1p2_ref, d2, BF16)

    tile = lambda c: pl.BlockSpec((tm, D), lambda i, c=c: (i, c))
    halo = lambda c: pl.BlockSpec((HALO, D), lambda i, c=c: (jnp.maximum(i * hb - 1, 0), c))
    row = pl.BlockSpec((1, D), lambda i: (0, 0))
    stat = pl.BlockSpec((tm, 1), lambda i: (i, 0))
    return pl.pallas_call(
        body, name="a_conv_out", grid=(S // tm,),
        in_specs=[tile(0), tile(1), tile(2), halo(0), halo(1), tile(0),
                  pl.BlockSpec((HALO, D), lambda i: (0, 0)), row, row, row,
                  pl.BlockSpec((D, D), lambda i: (0, 0)), row, row, row],
        out_specs=[tile(0), stat, tile(0), tile(0), stat, tile(0),
                   pl.BlockSpec((d1, tm // d1, D), lambda i: (0, i, 0)),
                   pl.BlockSpec((d2, tm // d2, D), lambda i: (0, i, 0))],
        out_shape=[SDS((S, D), F32), SDS((S, 1), F32), SDS((S, D), BF16), SDS((S, D), F32), SDS((S, 1), F32),
                   SDS((S, D), BF16), SDS((d1, S // d1, D), BF16), SDS((d2, S // d2, D), BF16)],
        scratch_shapes=[pltpu.VMEM((HALO + tm, D), F32), pltpu.VMEM((tm, D), F32),
                        pltpu.VMEM((D // 128, tm, 128), F32), pltpu.VMEM((7, HALO + tm - 8, D), F32)],
        compiler_params=_params(("parallel",)),
    )(h, h, h, h, h, x, wdw, bdw, lng, lnb, wout, bout, pg, pb)


def _band(n, dilation):
    qi = lax.broadcasted_iota(jnp.int32, (BLOCK, 2 * BLOCK), 0)
    kj = lax.broadcasted_iota(jnp.int32, (BLOCK, 2 * BLOCK), 1)
    dist = qi + BLOCK - kj
    valid = (dist >= 0) & (dist <= BLOCK) & ((n > 0) | (kj >= BLOCK))
    return jnp.where(valid, dist.astype(F32) * float(-dilation), -jnp.inf)


def _attn_fwd(g, qkv, D):
    S = qkv.shape[0]
    d = DILATIONS[g]
    nb = S // (d * BLOCK)
    H = D // HEAD_DIM
    slopes = _slopes(H)

    def body(q_ref, kp_ref, kc_ref, vp_ref, vc_ref, o_ref, lse_ref):
        neg_dist = _band(pl.program_id(1), d)
        lane = lax.broadcasted_iota(jnp.int32, (BLOCK, LSE_LANES), 1)
        low = lane < HEAD_DIM
        lse = jnp.zeros((BLOCK, LSE_LANES), F32)
        for hp in range(H // 2):
            sl = slice(hp * 128, (hp + 1) * 128)
            q = q_ref[:, sl]
            k = jnp.concatenate([kp_ref[:, sl], kc_ref[:, sl]], axis=0)
            v = jnp.concatenate([vp_ref[:, sl], vc_ref[:, sl]], axis=0)
            o = []
            for a in range(2):
                h = 2 * hp + a
                s = _dot_t(jnp.where(low if a == 0 else ~low, q, jnp.zeros_like(q)), k)
                s = s + slopes[h] * neg_dist
                m = jnp.max(s, axis=1, keepdims=True)
                p = jnp.exp(s - m)
                l = jnp.sum(p, axis=1, keepdims=True)
                o.append(_dot(p.astype(BF16), v) * (1.0 / l))
                lse = jnp.where(lane == h, m + jnp.log(l), lse)
            o_ref[:, sl] = jnp.where(low, o[0], o[1])
        lse_ref[...] = lse

    cur = lambda c: pl.BlockSpec((BLOCK, D), lambda r, n, c=c: (r * nb + n, c))
    prev = lambda c: pl.BlockSpec((BLOCK, D), lambda r, n, c=c: (r * nb + jnp.maximum(n - 1, 0), c))
    return pl.pallas_call(
        body, name="attn_fwd_g%d" % g, grid=(d, nb),
        in_specs=[cur(0), prev(1), cur(1), prev(2), cur(2)],
        out_specs=[cur(0), pl.BlockSpec((BLOCK, LSE_LANES), lambda r, n: (r * nb + n, 0))],
        out_shape=[SDS((S, D), F32), SDS((S, LSE_LANES), F32)],
        compiler_params=_params(("parallel", "parallel")),
    )(qkv, qkv, qkv, qkv, qkv)


def _attn_bwd(g, qkv, do, lse, delta, D):
    S = qkv.shape[0]
    d = DILATIONS[g]
    nb = S // (d * BLOCK)
    H = D // HEAD_DIM
    slopes = _slopes(H)

    def body(q_ref, kp_ref, kc_ref, vp_ref, vc_ref, do_ref, lse_ref, dl_ref, dq_ref, dkv_ref, ck_ref, cv_ref):
        n = pl.program_id(1)

        @pl.when(n == 0)
        def _():
            ck_ref[...] = jnp.zeros_like(ck_ref)
            cv_ref[...] = jnp.zeros_like(cv_ref)

        @pl.when(n < nb)
        def _():
            neg_dist = _band(n, d)
            low = lax.broadcasted_iota(jnp.int32, (BLOCK, 128), 1) < HEAD_DIM
            low2 = lax.broadcasted_iota(jnp.int32, (2 * BLOCK, 128), 1) < HEAD_DIM
            for hp in range(H // 2):
                sl = slice(hp * 128, (hp + 1) * 128)
                q = q_ref[:, sl]
                do2 = do_ref[:, sl]
                k = jnp.concatenate([kp_ref[:, sl], kc_ref[:, sl]], axis=0)
                v = jnp.concatenate([vp_ref[:, sl], vc_ref[:, sl]], axis=0)
                zero = jnp.zeros_like(q)
                q_do = jnp.concatenate([jnp.concatenate([q, zero], axis=1),
                                        jnp.concatenate([zero, do2], axis=1)], axis=0)
                dq, dkv = [], []
                for a in range(2):
                    h = 2 * hp + a
                    keep = low if a == 0 else ~low
                    s = _dot_t(jnp.where(keep, q, zero), k)
                    s = s + slopes[h] * neg_dist
                    p = jnp.exp(s - lse_ref[:, h:h + 1])
                    dp = _dot_t(jnp.where(keep, do2, zero), v)
                    dsb = (p * (dp - dl_ref[:, h:h + 1])).astype(BF16)
                    dq.append(_dot(dsb, k))
                    dkv.append(_tdot(jnp.concatenate([dsb, p.astype(BF16)], axis=0), q_do))
                dq_ref[:, sl] = (jnp.where(low, dq[0], dq[1]) * (HEAD_DIM ** -0.5)).astype(BF16)
                dk2 = jnp.where(low2, dkv[0][:, :128], dkv[1][:, :128])
                dv2 = jnp.where(low2, dkv[0][:, 128:], dkv[1][:, 128:])
                dkv_ref[:, sl] = (ck_ref[:, sl] + dk2[:BLOCK]).astype(BF16)
                dkv_ref[:, D + hp * 128:D + (hp + 1) * 128] = (cv_ref[:, sl] + dv2[:BLOCK]).astype(BF16)
                ck_ref[:, sl] = dk2[BLOCK:]
                cv_ref[:, sl] = dv2[BLOCK:]

        @pl.when(n == nb)
        def _():
            dkv_ref[:, :D] = ck_ref[...].astype(BF16)
            dkv_ref[:, D:] = cv_ref[...].astype(BF16)

    nq = lambda n: jnp.minimum(n, nb - 1)
    cur = lambda c: pl.BlockSpec((BLOCK, D), lambda r, n, c=c: (r * nb + nq(n), c))
    prev = lambda c: pl.BlockSpec((BLOCK, D), lambda r, n, c=c: (r * nb + jnp.maximum(nq(n) - 1, 0), c))
    stat = pl.BlockSpec((BLOCK, LSE_LANES), lambda r, n: (r * nb + nq(n), 0))
    return pl.pallas_call(
        body, name="attn_bwd_g%d" % g, grid=(d, nb + 1),
        in_specs=[cur(0), prev(1), cur(1), prev(2), cur(2), cur(0), stat, stat],
        out_specs=[cur(0), pl.BlockSpec((BLOCK, 2 * D), lambda r, n: (r * nb + jnp.maximum(n - 1, 0), 0))],
        out_shape=[SDS((S, D), BF16), SDS((S, 2 * D), BF16)],
        scratch_shapes=[pltpu.VMEM((BLOCK, D), F32), pltpu.VMEM((BLOCK, D), F32)],
        compiler_params=_params(("parallel", "arbitrary")),
    )(qkv, qkv, qkv, qkv, qkv, do, lse, delta)


def _b_merge_out_loss(o0, o1, o2, l0, l1, l2, z2, xh1, target, wbo, bbo, pg0, pb0, pg1, pb1, tm):
    S, D = o0.shape
    H = D // HEAD_DIM
    d1, d2 = DILATIONS[1], DILATIONS[2]
    inv_d = 1.0 / D

    def body(o0_ref, o1_ref, o2_ref, l0_ref, l1_ref, l2_ref, z_ref, xh1_ref, t_ref, wbo_ref, bbo_ref,
             pg0_ref, pb0_ref, pg1_ref, pb1_ref,
             v2b_ref, ds2_ref, ds2b_ref, dz2b_ref, da0_ref, da1_ref, da2_ref, ls0_ref, ls1_ref, ls2_ref,
             dl0_ref, dl1_ref, dl2_ref, loss_ref, sums_ref,
             o1n_ref, o2n_ref, l1n_ref, l2n_ref, att_ref):
        i = pl.program_id(0)
        _interleave(o1_ref, o1n_ref, d1)
        _interleave(o2_ref, o2n_ref, d2)
        for r in range(d1):
            l1n_ref[pl.ds(r, tm // d1, stride=d1), :] = l1_ref[r]
        for r in range(d2):
            l2n_ref[pl.ds(r, tm // d2, stride=d2), :] = l2_ref[r]
        la, lb, lc = l0_ref[...], l1n_ref[...], l2n_ref[...]
        m = jnp.maximum(jnp.maximum(la, lb), lc)
        ea, eb, ec = jnp.exp(la - m), jnp.exp(lb - m), jnp.exp(lc - m)
        den = ea + eb + ec
        wa, wb, wc = ea / den, eb / den, ec / den
        ls0_ref[...] = m + jnp.log(den)
        for h in range(H):
            sl = slice(h * HEAD_DIM, (h + 1) * HEAD_DIM)
            cc, hl = divmod(h * HEAD_DIM, 128)
            att_ref[:, sl] = (wa[:, h:h + 1] * o0_ref[:, sl] + wb[:, h:h + 1] * o1n_ref[cc, :, hl:hl + HEAD_DIM]
                              + wc[:, h:h + 1] * o2n_ref[cc, :, hl:hl + HEAD_DIM])
        att = att_ref[...]
        z = z_ref[...]
        sz = _sigmoid(z)
        gate = z * sz
        v2b = (att * gate).astype(BF16)
        v2b_ref[...] = v2b
        x1 = xh1_ref[...] * pg0_ref[...] + pb0_ref[...]
        s2 = ALPHA * x1 + _dot(v2b, wbo_ref[...]) + bbo_ref[...]
        xh2, rs2 = _ln_fwd(s2)
        err = xh2 * pg1_ref[...] + pb1_ref[...] - t_ref[...]
        dy = err * inv_d
        ds2 = _ln_bwd(dy * pg1_ref[...], xh2, rs2)
        ds2b = ds2.astype(BF16)
        ds2_ref[...] = ds2
        ds2b_ref[...] = ds2b

        @pl.when(i == 0)
        def _():
            loss_ref[...] = jnp.zeros_like(loss_ref)
            sums_ref[...] = jnp.zeros_like(sums_ref)
        loss_ref[...] += 0.5 * inv_d * jnp.sum(err * err)
        sums_ref[0:1, :] += _colsum(dy * xh2)
        sums_ref[1:2, :] += _colsum(dy)
        sums_ref[2:3, :] += _colsum(ds2)

        dv2 = _dot_t(ds2b, wbo_ref[...])
        datt = dv2 * gate
        dz2b_ref[...] = (dv2 * att * _silu_grad(z, sz)).astype(BF16)
        prod = datt * att
        lane = lax.broadcasted_iota(jnp.int32, (tm, LSE_LANES), 1)
        dl = jnp.zeros((tm, LSE_LANES), F32)
        for h in range(H):
            sl = slice(h * HEAD_DIM, (h + 1) * HEAD_DIM)
            dl = jnp.where(lane == h, jnp.sum(prod[:, sl], axis=1, keepdims=True), dl)
        da0_ref[...] = datt.astype(BF16)
        dl0_ref[...] = dl
        _to_chunks(o1n_ref, datt)
        _deinterleave(o1n_ref, da1_ref, d1, BF16)
        _deinterleave(o1n_ref, da2_ref, d2, BF16)
        for r in range(d1):
            ls1_ref[r] = ls0_ref[pl.ds(r, tm // d1, stride=d1), :]
            dl1_ref[r] = dl0_ref[pl.ds(r, tm // d1, stride=d1), :]
        for r in range(d2):
            ls2_ref[r] = ls0_ref[pl.ds(r, tm // d2, stride=d2), :]
            dl2_ref[r] = dl0_ref[pl.ds(r, tm // d2, stride=d2), :]

    tile = pl.BlockSpec((tm, D), lambda i: (i, 0))
    stat = pl.BlockSpec((tm, LSE_LANES), lambda i: (i, 0))
    perm = lambda d, w: pl.BlockSpec((d, tm // d, w), lambda i: (0, i, 0))
    row = pl.BlockSpec((1, D), lambda i: (0, 0))
    acc = lambda w: pl.BlockSpec((8, w), lambda i: (0, 0))
    pshape = lambda d, w, dt: SDS((d, S // d, w), dt)
    return pl.pallas_call(
        body, name="b_merge_out_loss", grid=(S // tm,),
        in_specs=[tile, perm(d1, D), perm(d2, D), stat, perm(d1, LSE_LANES), perm(d2, LSE_LANES),
                  tile, tile, tile, pl.BlockSpec((D, D), lambda i: (0, 0)), row, row, row, row, row],
        out_specs=[tile, tile, tile, tile, tile, perm(d1, D), perm(d2, D),
                   stat, perm(d1, LSE_LANES), perm(d2, LSE_LANES),
                   stat, perm(d1, LSE_LANES), perm(d2, LSE_LANES), acc(LSE_LANES), acc(D)],
        out_shape=[SDS((S, D), BF16), SDS((S, D), F32), SDS((S, D), BF16), SDS((S, D), BF16),
                   SDS((S, D), BF16), pshape(d1, D, BF16), pshape(d2, D, BF16),
                   SDS((S, LSE_LANES), F32), pshape(d1, LSE_LANES, F32), pshape(d2, LSE_LANES, F32),
                   SDS((S, LSE_LANES), F32), pshape(d1, LSE_LANES, F32), pshape(d2, LSE_LANES, F32),
                   SDS((8, LSE_LANES), F32), SDS((8, D), F32)],
        scratch_shapes=[pltpu.VMEM((D // 128, tm, 128), F32), pltpu.VMEM((D // 128, tm, 128), F32),
                        pltpu.VMEM((tm, LSE_LANES), F32), pltpu.VMEM((tm, LSE_LANES), F32),
                        pltpu.VMEM((tm, D), F32)],
        compiler_params=_params(("arbitrary",)),
    )(o0, o1, o2, l0, l1, l2, z2, xh1, target, wbo, bbo, pg0, pb0, pg1, pb1)


def _b_dx1_ln1_bwd(ds2, dz2b, dq, dkv, xh1, rs1, wz, wg, pg0, tm):
    S, D = ds2.shape
    d1, d2 = DILATIONS[1], DILATIONS[2]

    def group_part(dq_blk, dkv_blk, w_ref):
        return (_dot_t(dq_blk, w_ref[:, 0:D]) + _dot_t(dkv_blk[:, 0:D], w_ref[:, D:2 * D])
                + _dot_t(dkv_blk[:, D:2 * D], w_ref[:, 2 * D:3 * D]))

    def body(ds2_ref, dz_ref, dq0_ref, dkv0_ref, dq1_ref, dkv1_ref, dq2_ref, dkv2_ref, xh1_ref, rs1_ref,
             wz_ref, w0_ref, w1_ref, w2_ref, pg0_ref, ds1_ref, ds1b_ref, sums_ref, acc_ref):
        i = pl.program_id(0)
        _to_chunks(acc_ref, ALPHA * ds2_ref[...] + _dot_t(dz_ref[...], wz_ref[...])
                   + group_part(dq0_ref[...], dkv0_ref[...], w0_ref))
        for d, dq_ref, dkv_ref, w_ref in ((d1, dq1_ref, dkv1_ref, w1_ref), (d2, dq2_ref, dkv2_ref, w2_ref)):
            rows = tm // d
            part = group_part(dq_ref[...].reshape(tm, D), dkv_ref[...].reshape(tm, 2 * D), w_ref)
            for r in range(d):
                idx = pl.ds(r, rows, stride=d)
                for cc in range(D // 128):
                    acc_ref[cc, idx, :] = acc_ref[cc, idx, :] + part[r * rows:(r + 1) * rows, cc * 128:(cc + 1) * 128]
        dx1 = jnp.concatenate([acc_ref[cc] for cc in range(D // 128)], axis=1)
        xh1 = xh1_ref[...]
        ds1 = _ln_bwd(dx1 * pg0_ref[...], xh1, rs1_ref[...])
        ds1_ref[...] = ds1
        ds1b_ref[...] = ds1.astype(BF16)

        @pl.when(i == 0)
        def _():
            sums_ref[...] = jnp.zeros_like(sums_ref)
        sums_ref[0:1, :] += _colsum(dx1 * xh1)
        sums_ref[1:2, :] += _colsum(dx1)
        sums_ref[2:3, :] += _colsum(ds1)

    tile = lambda w: pl.BlockSpec((tm, w), lambda i: (i, 0))
    perm = lambda d, w: pl.BlockSpec((d, tm // d, w), lambda i: (0, i, 0))
    whole = pl.BlockSpec(memory_space=pltpu.VMEM)
    return pl.pallas_call(
        body, name="b_dx1_ln1_bwd", grid=(S // tm,),
        in_specs=[tile(D), tile(D), tile(D), tile(2 * D), perm(d1, D), perm(d1, 2 * D), perm(d2, D),
                  perm(d2, 2 * D), tile(D), tile(1), whole, whole, whole, whole,
                  pl.BlockSpec((1, D), lambda i: (0, 0))],
        out_specs=[tile(D), tile(D), pl.BlockSpec((8, D), lambda i: (0, 0))],
        out_shape=[SDS((S, D), F32), SDS((S, D), BF16), SDS((8, D), F32)],
        scratch_shapes=[pltpu.VMEM((D // 128, tm, 128), F32)],
        compiler_params=_params(("arbitrary",)),
    )(ds2, dz2b, dq[0], dkv[0], dq[1].reshape(d1, S // d1, D), dkv[1].reshape(d1, S // d1, 2 * D),
      dq[2].reshape(d2, S // d2, D), dkv[2].reshape(d2, S // d2, 2 * D), xh1, rs1, wz, wg[0], wg[1], wg[2], pg0)


def _a_gate_bwd(ds1b, h, xhu, rsu, wout, lng, lnb, tm):
    S, D = xhu.shape

    def body(ds_ref, z_ref, xhu_ref, rsu_ref, w_ref, lng_ref, lnb_ref, du1_ref, dzb_ref, sums_ref):
        i = pl.program_id(0)
        dv = _dot_t(ds_ref[...], w_ref[...])
        xhu = xhu_ref[...]
        u2 = xhu * lng_ref[...] + lnb_ref[...]
        su = _sigmoid(u2)
        z = z_ref[...]
        sz = _sigmoid(z)
        dz = dv * (u2 * su) * _silu_grad(z, sz)
        du2 = dv * (z * sz) * _silu_grad(u2, su)
        du1 = _ln_bwd(du2 * lng_ref[...], xhu, rsu_ref[...])
        du1_ref[...] = du1
        dzb_ref[...] = dz.astype(BF16)

        @pl.when(i == 0)
        def _():
            sums_ref[...] = jnp.zeros_like(sums_ref)
        sums_ref[0:1, :] += _colsum(du2 * xhu)
        sums_ref[1:2, :] += _colsum(du2)
        sums_ref[2:3, :] += _colsum(du1)
        sums_ref[3:4, :] += _colsum(dz)

    tile = pl.BlockSpec((tm, D), lambda i: (i, 0))
    row = pl.BlockSpec((1, D), lambda i: (0, 0))
    return pl.pallas_call(
        body, name="a_gate_bwd", grid=(S // tm,),
        in_specs=[tile, pl.BlockSpec((tm, D), lambda i: (i, 2)), tile, pl.BlockSpec((tm, 1), lambda i: (i, 0)),
                  pl.BlockSpec((D, D), lambda i: (0, 0)), row, row],
        out_specs=[tile, tile, pl.BlockSpec((8, D), lambda i: (0, 0))],
        out_shape=[SDS((S, D), F32), SDS((S, D), BF16), SDS((8, D), F32)],
        compiler_params=_params(("arbitrary",)),
    )(ds1b, h, xhu, rsu, wout, lng, lnb)


def _a_conv_bwd(du1, h, wdw, parts, tm):
    S, D = du1.shape
    hb = tm // HALO
    last_halo = S // HALO - 1
    n_tiles = S // tm
    n = len(parts)

    def body(du_ref, dun_ref, a_ref, g_ref, ah_ref, gh_ref, wdw_ref, *refs):
        part_refs, (dag_ref, sums_ref, wsum_ref), landed_refs = refs[:n], refs[n:n + 3], refs[n + 3:2 * n + 3]
        dext_ref, ext_ref, dsh_ref, sh_ref, wacc_ref, send_sems, recv_sems = refs[2 * n + 3:]
        scatter = (part_refs, landed_refs, send_sems, recv_sems)
        i = pl.program_id(0)

        @pl.when(i == 0)
        def _():
            _scatter_start(*scatter)
            sums_ref[...] = jnp.zeros_like(sums_ref)
            wsum_ref[...] = jnp.zeros_like(wsum_ref)
            wacc_ref[...] = jnp.zeros_like(wacc_ref)
        dext_ref[0:tm, :] = du_ref[...]
        dext_ref[tm:, :] = jnp.where(i < n_tiles - 1, dun_ref[...], 0.0)
        _fill_shifts(dsh_ref, dext_ref)
        _fill_glu_ext(ext_ref, a_ref, g_ref, ah_ref, gh_ref, i > 0)
        _fill_shifts(sh_ref, ext_ref)
        back = [CONV_WIDTH - 1 - k for k in range(CONV_WIDTH)]
        fwd = [HALO - (CONV_WIDTH - 1) + k for k in range(CONV_WIDTH)]
        for cc in range(D // 128):
            lanes = slice(cc * 128, (cc + 1) * 128)
            hi_lanes = slice(D + cc * 128, D + (cc + 1) * 128)
            sa = jnp.zeros((1, 128), F32)
            sg = jnp.zeros((1, 128), F32)
            for r0 in range(0, tm, CONV_ROWS):
                acc = jnp.zeros((CONV_ROWS, 128), F32)
                for k, win in _tap_windows(dext_ref, dsh_ref, back, r0, CONV_ROWS, lanes):
                    acc = acc + wdw_ref[k:k + 1, lanes] * win
                a = a_ref[r0:r0 + CONV_ROWS, lanes]
                s = _sigmoid(g_ref[r0:r0 + CONV_ROWS, lanes])
                da = acc * s
                dg = acc * a * s * (1.0 - s)
                dag_ref[r0:r0 + CONV_ROWS, lanes] = da.astype(BF16)
                dag_ref[r0:r0 + CONV_ROWS, hi_lanes] = dg.astype(BF16)
                sa = sa + _colsum(da)
                sg = sg + _colsum(dg)
                du = du_ref[r0:r0 + CONV_ROWS, lanes]
                for k, win in _tap_windows(ext_ref, sh_ref, fwd, r0, CONV_ROWS, lanes):
                    p = du * win
                    fold = p[0:8]
                    for q in range(8, CONV_ROWS, 8):
                        fold = fold + p[q:q + 8]
                    wacc_ref[k, :, lanes] += fold
            sums_ref[0:1, lanes] += sa
            sums_ref[1:2, lanes] += sg

        @pl.when(i == n_tiles - 1)
        def _():
            for k in range(CONV_WIDTH):
                wsum_ref[k:k + 1, :] = _colsum(wacc_ref[k])
            _scatter_finish(*scatter)

    tile = lambda c: pl.BlockSpec((tm, D), lambda i, c=c: (i, c))
    halo = lambda c: pl.BlockSpec((HALO, D), lambda i, c=c: (jnp.maximum(i * hb - 1, 0), c))
    outs = pl.pallas_call(
        body, name="a_conv_bwd", grid=(n_tiles,),
        in_specs=[tile(0), pl.BlockSpec((HALO, D), lambda i: (jnp.minimum((i + 1) * hb, last_halo), 0)),
                  tile(0), tile(1), halo(0), halo(1), pl.BlockSpec((HALO, D), lambda i: (0, 0))] + _hbm_specs(n),
        out_specs=[pl.BlockSpec((tm, 2 * D), lambda i: (i, 0)), pl.BlockSpec((8, D), lambda i: (0, 0)),
                   pl.BlockSpec((HALO, D), lambda i: (0, 0))] + _hbm_specs(n),
        out_shape=[SDS((S, 2 * D), BF16), SDS((8, D), F32), SDS((HALO, D), F32)]
        + [SDS((3,) + p.shape[1:], p.dtype) for p in parts],
        scratch_shapes=[pltpu.VMEM((tm + HALO, D), F32), pltpu.VMEM((HALO + tm, D), F32),
                        pltpu.VMEM((7, HALO + tm - 8, D), F32), pltpu.VMEM((7, HALO + tm - 8, D), F32),
                        pltpu.VMEM((HALO, 8, D), F32)] + _scatter_scratch(n),
        compiler_params=_params(("arbitrary",)),
    )(du1, du1, h, h, h, h, wdw, *parts)
    return outs[0], outs[1], outs[2], outs[3:]


def _a_dx(ds1, dag, dzb, w_in, tm):
    S, D = ds1.shape

    def body(ds_ref, dag_ref, dz_ref, w_ref, o_ref):
        o_ref[...] = (ALPHA * ds_ref[...] + _dot_t(dag_ref[...], w_ref[:, 0:2 * D])
                      + _dot_t(dz_ref[...], w_ref[:, 2 * D:3 * D]))

    tile = lambda w: pl.BlockSpec((tm, w), lambda i: (i, 0))
    return pl.pallas_call(
        body, name="a_dx", grid=(S // tm,),
        in_specs=[tile(D), tile(2 * D), tile(D), pl.BlockSpec(memory_space=pltpu.VMEM)],
        out_specs=tile(D), out_shape=SDS((S, D), F32),
        compiler_params=_params(("parallel",)),
    )(ds1, dag, dzb, w_in)


def _halves(w):
    return w.reshape(2, w.shape[0] // 2, w.shape[1])


def _unstack_cols(w4):
    return jnp.transpose(w4, (1, 0, 2)).reshape(w4.shape[1], N_CHIPS * w4.shape[2])


def _stack_cols(w):
    D, n = w.shape
    return jnp.transpose(w.reshape(D, N_CHIPS, n // N_CHIPS), (1, 0, 2))


def _pack_rows(rows, width):
    slab = jnp.concatenate([r.reshape(-1, width) for r in rows], axis=0)
    return jnp.pad(slab, ((0, SMALL_ROWS - slab.shape[0]), (0, 0)))


def kernel(x, a_w_in, a_b_in, a_w_dw, a_b_dw, a_ln_g, a_ln_b, a_w_out, a_b_out, kv_w, b_w_in, b_w_out, b_b_out, post_ln_g, post_ln_b, loss_target, m_a_w_in, m_a_b_in, m_a_w_dw, m_a_b_dw, m_a_ln_g, m_a_ln_b, m_a_w_out, m_a_b_out, m_kv_w, m_b_w_in, m_b_w_out, m_b_b_out, m_post_ln_g, m_post_ln_b, v_a_w_in, v_a_b_in, v_a_w_dw, v_a_b_dw, v_a_ln_g, v_a_ln_b, v_a_w_out, v_a_b_out, v_kv_w, v_b_w_in, v_b_w_out, v_b_b_out, v_post_ln_g, v_post_ln_b):
    S, D = x.shape[1], x.shape[2]
    dq4 = D // N_CHIPS
    tm = 256
    tm_mm = min(S, 1024)
    x2 = x.reshape(S, D)
    target = loss_target.reshape(S, D)
    jchip = 2 * lax.axis_index("x") + lax.axis_index("y")

    big_local = [a_w_in[0], kv_w, b_w_in[0], a_w_out[0], b_w_out[0]]
    small_local = _pack_rows([a_b_in.reshape(3, dq4), jnp.pad(a_w_dw[0], ((0, 1), (0, 0))), a_b_dw, a_ln_g,
                              a_ln_b, a_b_out], dq4)
    wire = [_halves(w.astype(BF16)) for w in big_local]
    whole = lambda g: g.reshape((N_CHIPS, 2 * g.shape[2], g.shape[3]))
    gathered = _all_gather_chips([wire[0], wire[3], _halves(small_local)])
    w_in4 = whole(gathered[0])
    w_in_a = _unstack_cols(w_in4)
    w_out_a = gathered[1].reshape(D, D)
    small = jnp.transpose(gathered[2].reshape(N_CHIPS, SMALL_ROWS, dq4), (1, 0, 2))
    b_in_full = jnp.transpose(small[0:3], (1, 0, 2)).reshape(1, 3 * D)
    wdw_full = small[3:3 + HALO].reshape(HALO, D)
    bdw_full, lng_full, lnb_full, bout_a_full = [small[35 + q].reshape(1, D) for q in range(4)]
    pg0, pg1 = post_ln_g[0:1], post_ln_g[1:2]
    pb0, pb1 = post_ln_b[0:1], post_ln_b[1:2]

    h, xb, later = _a_in_proj(x2, w_in4, b_in_full, [wire[1], wire[2], wire[4]], tm_mm)
    kv_full = _unstack_cols(whole(later[0]))
    b_in4 = whole(later[1])
    w_out_b = later[2].reshape(D, D)
    w_z = b_in4[3]
    w_g = [jnp.concatenate([b_in4[g], kv_full[:, g * D:(g + 1) * D], kv_full[:, (3 + g) * D:(4 + g) * D]], axis=1)
           for g in range(3)]
    xhu, rsu, vb, xh1, rs1, x1b, x1p1, x1p2 = _a_conv_out(
        h, x2, wdw_full, bdw_full, lng_full, lnb_full, w_out_a, bout_a_full, pg0, pb0, tm)
    x1g = [x1b, x1p1.reshape(S, D), x1p2.reshape(S, D)]
    qkv = [_mm_nn("b_qkv_g%d" % g, x1g[g], w_g[g], BF16, tm_mm, scale_first_tile=HEAD_DIM ** -0.5)
           for g in range(3)]
    z2 = _mm_nn("b_gate_proj", x1b, w_z, F32, tm_mm)
    og, lg = zip(*[_attn_fwd(g, qkv[g], D) for g in range(3)])
    d1, d2 = DILATIONS[1], DILATIONS[2]
    (v2b, ds2, ds2b, dz2b, da0, da1, da2, ls0, ls1, ls2, dl0, dl1, dl2, loss_acc, sums_b) = _b_merge_out_loss(
        og[0], og[1].reshape(d1, S // d1, D), og[2].reshape(d2, S // d2, D),
        lg[0], lg[1].reshape(d1, S // d1, LSE_LANES), lg[2].reshape(d2, S // d2, LSE_LANES),
        z2, xh1, target, w_out_b, b_b_out, pg0, pb0, pg1, pb1, tm)

    das = [da0, da1.reshape(S, D), da2.reshape(S, D)]
    lss = [ls0, ls1.reshape(S, LSE_LANES), ls2.reshape(S, LSE_LANES)]
    dls = [dl0, dl1.reshape(S, LSE_LANES), dl2.reshape(S, LSE_LANES)]
    dq, dkv = zip(*[_attn_bwd(g, qkv[g], das[g], lss[g], dls[g], D) for g in range(3)])
    ds1, ds1b, sums_1 = _b_dx1_ln1_bwd(ds2, dz2b, dq, dkv, xh1, rs1, w_z, w_g, pg0, tm)
    du1, dzab, sums_a = _a_gate_bwd(ds1b, h, xhu, rsu, w_out_a, lng_full, lnb_full, tm)

    def by_chip_cols(gw):
        s4 = _stack_cols(gw)
        return s4.reshape(N_CHIPS, 2, D // 2, s4.shape[2])

    def by_chip_rows(gw):
        return gw.reshape(N_CHIPS, 2, D // 8, D)

    core = lax.axis_index("c").astype(jnp.int32).reshape(1)
    chip = jchip.astype(jnp.int32).reshape(1)

    g_w_out_b = _mm_tn("dw_b_out", v2b, ds2b, tm_mm)
    g_q = [_mm_tn("dw_b_q_g%d" % g, x1g[g], dq[g], tm_mm) for g in range(3)]
    g_z = _mm_tn("dw_b_z", x1b, dz2b, tm_mm)
    g_kvg = [_mm_tn("dw_kv_g%d" % g, x1g[g], dkv[g], tm_mm) for g in range(3)]
    g_kv = jnp.concatenate([t[:, :D] for t in g_kvg] + [t[:, D:] for t in g_kvg], axis=1)
    grads_b = [by_chip_cols(g_kv), jnp.stack(g_q + [g_z]).reshape(N_CHIPS, 2, D // 2, D), by_chip_rows(g_w_out_b)]
    parts_b, wire_b = _pair_sum(grads_b, _pair_exchange(grads_b, "b"), core, "b")
    dag, sums_c, wsum, landed_b = _a_conv_bwd(du1, h, wdw_full, wire_b, tm)
    own_b = _chip_sum(parts_b, landed_b, chip, "b")
    other_b = _pair_share(own_b, "b")
    grad_x = _a_dx(ds1, dag, dzab, w_in_a, 2 * tm)

    g_w_in = jnp.concatenate([_mm_tn("dw_a_in_ag", xb, dag, tm_mm), _mm_tn("dw_a_in_z", xb, dzab, tm_mm)], axis=1)
    g_w_out_a = _mm_tn("dw_a_out", vb, ds1b, tm_mm)
    grads_a = [by_chip_cols(g_w_in), by_chip_rows(g_w_out_a)]
    parts_a, wire_a = _pair_sum(grads_a, _pair_exchange(grads_a, "a"), core, "a")
    own_a = _chip_sum(parts_a, _chip_scatter(wire_a, "a"), chip, "a")
    other_a = _pair_share(own_a, "a")
    own_half = [own_a[0], own_b[0], own_b[1], own_a[1], own_b[2]]
    other_half = [other_a[0], other_b[0], other_b[1], other_a[1], other_b[2]]

    small_grads = _pack_rows([sums_c[0:1], sums_c[1:2], sums_a[3:4], wsum, sums_a[2:3], sums_a[0:1], sums_a[1:2],
                              sums_1[2:3], sums_b[2:3], sums_1[0:1], sums_b[0:1], sums_1[1:2], sums_b[1:2]], D)
    small_sum = _sum_devices(_gather_all_devices(small_grads))
    loss = lax.psum(loss_acc[0, 0], ("x", "y", "c"))

    big_m = [m_a_w_in[0], m_kv_w, m_b_w_in[0], m_a_w_out[0], m_b_w_out[0]]
    big_v = [v_a_w_in[0], v_kv_w, v_b_w_in[0], v_a_w_out[0], v_b_w_out[0]]
    shards, big_delta, big_new_m, big_new_v = [
        [a.reshape(2 * a.shape[1], a.shape[2]) for a in group] for group in _adamw_halves(
            [_halves(w) for w in big_local], own_half, other_half, [_halves(m) for m in big_m],
            [_halves(v) for v in big_v], core)]

    def chip_cols(rows):
        return lax.dynamic_slice_in_dim(rows, jchip * dq4, dq4, axis=1)

    g_b_in = lax.dynamic_slice_in_dim(small_sum[0:3].reshape(1, 3 * D), jchip * 3 * dq4, 3 * dq4, axis=1)
    small_g = [g_b_in, chip_cols(small_sum[3:3 + CONV_WIDTH]), chip_cols(small_sum[35:36]), chip_cols(small_sum[36:37]),
               chip_cols(small_sum[37:38]), chip_cols(small_sum[38:39]), small_sum[39:40], small_sum[40:42],
               small_sum[42:44]]
    small_w = [a_b_in, a_w_dw[0], a_b_dw, a_ln_g, a_ln_b, a_b_out, b_b_out, post_ln_g, post_ln_b]
    small_m = [m_a_b_in, m_a_w_dw[0], m_a_b_dw, m_a_ln_g, m_a_ln_b, m_a_b_out, m_b_b_out, m_post_ln_g, m_post_ln_b]
    small_v = [v_a_b_in, v_a_w_dw[0], v_a_b_dw, v_a_ln_g, v_a_ln_b, v_a_b_out, v_b_b_out, v_post_ln_g, v_post_ln_b]
    small_delta, small_new_m, small_new_v = _adamw("adamw_small", small_w, small_g, small_m, small_v, 1)

    def ordered(big, sm):
        return (big[0][None], sm[0], sm[1][None], sm[2], sm[3], sm[4], big[3][None], sm[5], big[1], big[2][None],
                big[4][None], sm[6], sm[7], sm[8])

    return (loss, grad_x.reshape(1, S, D), *ordered(shards, small_g), *ordered(big_delta, small_delta),
            *ordered(big_new_m, small_new_m), *ordered(big_new_v, small_new_v))
```

```python
import functools

import numpy as np
import jax
import jax.numpy as jnp
from jax import lax
from jax.experimental import pallas as pl
from jax.experimental.pallas import tpu as pltpu

F32 = jnp.float32
BF16 = jnp.bfloat16
MESH = pl.DeviceIdType.MESH
SDS = jax.ShapeDtypeStruct

HEAD_DIM = 64
BLOCK = 128
DILATIONS = (1, 4, 16)
ALIBI_MAX_EXP = 8.0
CONV_WIDTH = 31
HALO = 32
CONV_ROWS = 128
LSE_LANES = 128
DEPTH = 2
ALPHA = (2.0 * DEPTH) ** 0.25
LN_EPS = 1e-5
ADAM_LR = 0.001
ADAM_B1 = 0.9
ADAM_B2 = 0.999
ADAM_EPS = 1e-08
ADAM_WD = 0.01
ADAM_STEP = 10
N_CHIPS = 4
N_DEV = 8
VMEM_LIMIT = 56 * 2 ** 20
SMALL_ROWS = 48


def _params(sem=None):
    return pltpu.CompilerParams(dimension_semantics=sem, vmem_limit_bytes=VMEM_LIMIT)


def _sigmoid(x):
    return 1.0 / (1.0 + jnp.exp(-x))


def _silu_grad(x, s):
    return s * (1.0 + x * (1.0 - s))


def _ln_fwd(x):
    mu = jnp.mean(x, axis=-1, keepdims=True)
    xc = x - mu
    var = jnp.mean(xc * xc, axis=-1, keepdims=True)
    rstd = lax.rsqrt(var + LN_EPS)
    return xc * rstd, rstd


def _ln_bwd(dxhat, xhat, rstd):
    m1 = jnp.mean(dxhat, axis=-1, keepdims=True)
    m2 = jnp.mean(dxhat * xhat, axis=-1, keepdims=True)
    return rstd * (dxhat - m1 - xhat * m2)


def _dot(a, b):
    return jnp.dot(a, b, preferred_element_type=F32)


def _dot_t(a, b):
    return lax.dot_general(a, b, (((1,), (1,)), ((), ())), preferred_element_type=F32)


def _tdot(a, b):
    return lax.dot_general(a, b, (((0,), (0,)), ((), ())), preferred_element_type=F32)


def _colsum(x):
    return jnp.sum(x, axis=0, keepdims=True)


def _slopes(n_heads):
    return [float(np.float32(2.0 ** (-ALIBI_MAX_EXP * (h + 1) / n_heads))) for h in range(n_heads)]


def _to_chunks(chunks_ref, x):
    for cc in range(chunks_ref.shape[0]):
        chunks_ref[cc] = x[:, cc * 128:(cc + 1) * 128]


def _deinterleave(chunks_ref, out_ref, d, dtype):
    rows = chunks_ref.shape[1] // d
    for r in range(d):
        for cc in range(chunks_ref.shape[0]):
            out_ref[r, :, cc * 128:(cc + 1) * 128] = chunks_ref[cc, pl.ds(r, rows, stride=d), :].astype(dtype)


def _interleave(in_ref, chunks_ref, d):
    rows = chunks_ref.shape[1] // d
    for r in range(d):
        for cc in range(chunks_ref.shape[0]):
            chunks_ref[cc, pl.ds(r, rows, stride=d), :] = in_ref[r, :, cc * 128:(cc + 1) * 128]


def _hbm_specs(n):
    return [pl.BlockSpec(memory_space=pl.ANY)] * n


def _position():
    x, y, c = lax.axis_index("x"), lax.axis_index("y"), lax.axis_index("c")
    return x, y, c


def _gather_stages(ins, outs, send_sems, recv_sems, local_sems):
    n = len(ins)

    def plan():
        x, y, c = _position()
        j = 2 * x + y
        me, sibling = (x, y, c), (x, y, 1 - c)
        chips = [(1 - x, y), (x, 1 - y), (1 - x, 1 - y)]

        def copy(i, k, src, dst, to):
            return pltpu.make_async_remote_copy(
                src_ref=src, dst_ref=dst, send_sem=send_sems.at[i, k], recv_sem=recv_sems.at[i, k],
                device_id=to, device_id_type=MESH)

        local = [pltpu.make_async_copy(ins[i], outs[i].at[j], local_sems.at[i]) for i in range(n)]
        first, landing, passed, passed_landing = [], [], [], []
        for i in range(n):
            for k, chip in enumerate(chips):
                pj = 2 * chip[0] + chip[1]
                first.append(copy(i, k, ins[i].at[c], outs[i].at[j, c], (*chip, c)))
                landing.append(copy(i, k, ins[i].at[c], outs[i].at[pj, c], me))
                passed.append(copy(i, 3 + k, outs[i].at[pj, c], outs[i].at[pj, c], sibling))
                passed_landing.append(copy(i, 3 + k, ins[i].at[c], outs[i].at[pj, 1 - c], me))
        return local, first, landing, passed, passed_landing

    def start():
        local, first, _, _, _ = plan()
        for cp in local + first:
            cp.start()

    def forward():
        _, _, landing, passed, _ = plan()
        for arrived, cp in zip(landing, passed):
            arrived.wait_recv()
            cp.start()

    def finish():
        local, first, _, passed, passed_landing = plan()
        for cp in passed_landing:
            cp.wait_recv()
        for cp in first + passed:
            cp.wait_send()
        for cp in local:
            cp.wait()

    return start, forward, finish


def _gather_scratch(n):
    return [pltpu.SemaphoreType.DMA((n, 6)), pltpu.SemaphoreType.DMA((n, 6)), pltpu.SemaphoreType.DMA((n,))]


def _all_gather_chips(shards):
    n = len(shards)

    def body(*refs):
        for stage in _gather_stages(refs[:n], refs[n:2 * n], *refs[2 * n:]):
            stage()

    return pl.pallas_call(
        body, name="all_gather_chips",
        out_shape=[SDS((N_CHIPS,) + s.shape, s.dtype) for s in shards],
        in_specs=_hbm_specs(n), out_specs=_hbm_specs(n), scratch_shapes=_gather_scratch(n),
    )(*shards)


class _PairExchange:
    slots = N_CHIPS

    @staticmethod
    def out_shape(g):
        return SDS((N_CHIPS,) + g.shape[2:], g.dtype)

    @staticmethod
    def copies(ins, outs, send_sems, recv_sems):
        x, y, c = _position()
        return [pltpu.make_async_remote_copy(
            src_ref=ins[i].at[j, 1 - c], dst_ref=outs[i].at[j],
            send_sem=send_sems.at[i, j], recv_sem=recv_sems.at[i, j],
            device_id=(x, y, 1 - c), device_id_type=MESH) for i in range(len(ins)) for j in range(N_CHIPS)]


class _ChipScatter:
    slots = 3

    @staticmethod
    def out_shape(p):
        return SDS((3,) + p.shape[1:], p.dtype)

    @staticmethod
    def copies(ins, outs, send_sems, recv_sems):
        x, y, c = _position()
        chips = [(1 - x, y), (x, 1 - y), (1 - x, 1 - y)]
        return [pltpu.make_async_remote_copy(
            src_ref=ins[i].at[2 * chip[0] + chip[1]], dst_ref=outs[i].at[k],
            send_sem=send_sems.at[i, k], recv_sem=recv_sems.at[i, k],
            device_id=(*chip, c), device_id_type=MESH) for i in range(len(ins)) for k, chip in enumerate(chips)]


def _exchange_start(kind, *refs):
    for cp in kind.copies(*refs):
        cp.start()


def _exchange_finish(kind, *refs):
    copies = kind.copies(*refs)
    for cp in copies:
        cp.wait_recv()
    for cp in copies:
        cp.wait_send()


def _exchange_scratch(kind, n):
    return [pltpu.SemaphoreType.DMA((n, kind.slots)), pltpu.SemaphoreType.DMA((n, kind.slots))]


def _exchange(kind, name, arrays):
    n = len(arrays)

    def body(*refs):
        args = (refs[:n], refs[n:2 * n], *refs[2 * n:])
        _exchange_start(kind, *args)
        _exchange_finish(kind, *args)

    return pl.pallas_call(
        body, name=name, out_shape=[kind.out_shape(a) for a in arrays],
        in_specs=_hbm_specs(n), out_specs=_hbm_specs(n), scratch_shapes=_exchange_scratch(kind, n),
    )(*arrays)


def _pair_share(halves, tag):
    n = len(halves)

    def body(*refs):
        ins, outs = refs[:n], refs[n:2 * n]
        send_sems, recv_sems = refs[2 * n:]
        x, y, c = _position()
        remote = [pltpu.make_async_remote_copy(
            src_ref=ins[i], dst_ref=outs[i], send_sem=send_sems.at[i], recv_sem=recv_sems.at[i],
            device_id=(x, y, 1 - c), device_id_type=MESH) for i in range(n)]
        for cp in remote:
            cp.start()
        for cp in remote:
            cp.wait_recv()
        for cp in remote:
            cp.wait_send()

    return pl.pallas_call(
        body, name="grad_pair_share_" + tag,
        out_shape=[SDS(h.shape, h.dtype) for h in halves],
        in_specs=_hbm_specs(n), out_specs=_hbm_specs(n),
        scratch_shapes=[pltpu.SemaphoreType.DMA((n,)), pltpu.SemaphoreType.DMA((n,))],
    )(*halves)


def _gather_all_devices(slab):
    def body(in_ref, out_ref, send_sems, recv_sems, local_sem):
        x, y, c = _position()
        me = 4 * x + 2 * y + c
        local = pltpu.make_async_copy(in_ref, out_ref.at[me], local_sem)
        local.start()
        remote, landing = [], []
        for mask in range(1, N_DEV):
            px, py, pc = x ^ (mask >> 2), y ^ ((mask >> 1) & 1), c ^ (mask & 1)
            peer = 4 * px + 2 * py + pc
            remote.append(pltpu.make_async_remote_copy(
                src_ref=in_ref, dst_ref=out_ref.at[me], send_sem=send_sems.at[mask - 1],
                recv_sem=recv_sems.at[mask - 1], device_id=(px, py, pc), device_id_type=MESH))
            landing.append(pltpu.make_async_remote_copy(
                src_ref=in_ref, dst_ref=out_ref.at[peer], send_sem=send_sems.at[mask - 1],
                recv_sem=recv_sems.at[mask - 1], device_id=(px, py, pc), device_id_type=MESH))
        for cp in remote:
            cp.start()
        for cp in landing:
            cp.wait_recv()
        for cp in remote:
            cp.wait_send()
        local.wait()

    return pl.pallas_call(
        body, name="small_grad_gather",
        out_shape=SDS((N_DEV,) + slab.shape, slab.dtype),
        in_specs=_hbm_specs(1), out_specs=pl.BlockSpec(memory_space=pl.ANY),
        scratch_shapes=[pltpu.SemaphoreType.DMA((N_DEV - 1,)), pltpu.SemaphoreType.DMA((N_DEV - 1,)),
                        pltpu.SemaphoreType.DMA],
    )(slab)


def _row_splits(arrays):
    return min(a.shape[-2] for a in arrays) // 16


def _pair_sum(grads, recvd, core, tag):
    n = len(grads)
    splits = _row_splits(recvd)

    def body(core_ref, *refs):
        for i in range(n):
            s = refs[i][...] + refs[n + i][...]
            refs[2 * n + i][...] = s
            refs[3 * n + i][...] = s.astype(BF16)

    mine = [pl.BlockSpec((N_CHIPS, None, r.shape[1] // splits, r.shape[2]), lambda s, core: (0, core[0], s, 0))
            for r in recvd]
    block = [pl.BlockSpec((N_CHIPS, r.shape[1] // splits, r.shape[2]), lambda s, core: (0, s, 0)) for r in recvd]
    outs = pl.pallas_call(
        body, name="grad_pair_sum_" + tag,
        grid_spec=pltpu.PrefetchScalarGridSpec(
            num_scalar_prefetch=1, grid=(splits,), in_specs=mine + block, out_specs=block + block),
        out_shape=[SDS(r.shape, F32) for r in recvd] + [SDS(r.shape, BF16) for r in recvd],
        compiler_params=_params(("parallel",)),
    )(core, *grads, *recvd)
    return outs[:n], outs[n:]


def _chip_sum(parts, landed, chip, tag):
    n = len(parts)
    splits = _row_splits(landed)

    def body(chip_ref, *refs):
        for i in range(n):
            acc = refs[i][...]
            for k in range(3):
                acc = acc + refs[n + i][k].astype(F32)
            refs[2 * n + i][...] = acc

    rows = lambda p: p.shape[1] // splits
    return pl.pallas_call(
        body, name="grad_chip_sum_" + tag,
        grid_spec=pltpu.PrefetchScalarGridSpec(
            num_scalar_prefetch=1, grid=(splits,),
            in_specs=[pl.BlockSpec((None, rows(p), p.shape[2]), lambda s, chip: (chip[0], s, 0)) for p in parts]
            + [pl.BlockSpec((3, rows(p), p.shape[2]), lambda s, chip: (0, s, 0)) for p in parts],
            out_specs=[pl.BlockSpec((rows(p), p.shape[2]), lambda s, chip: (s, 0)) for p in parts]),
        out_shape=[SDS(p.shape[1:], F32) for p in parts],
        compiler_params=_params(("parallel",)),
    )(chip, *parts, *landed)


def _adamw_math(w, g, m, v):
    m = ADAM_B1 * m + (1.0 - ADAM_B1) * g
    v = ADAM_B2 * v + (1.0 - ADAM_B2) * (g * g)
    m_hat = m / (1.0 - ADAM_B1 ** ADAM_STEP)
    v_hat = v / (1.0 - ADAM_B2 ** ADAM_STEP)
    delta = -ADAM_LR * (m_hat / (jnp.sqrt(v_hat) + ADAM_EPS) + ADAM_WD * w)
    return delta, m, v


def _adamw(name, ws, gs, ms, vs, splits):
    n = len(ws)

    def body(*refs):
        for i in range(n):
            w, g, m, v = (refs[q * n + i][...] for q in range(4))
            delta, m, v = _adamw_math(w, g, m, v)
            refs[4 * n + i][...] = delta
            refs[5 * n + i][...] = m
            refs[6 * n + i][...] = v

    def spec(a):
        if splits == 1:
            return pl.BlockSpec(a.shape, lambda s: (0, 0))
        return pl.BlockSpec((a.shape[0] // splits, a.shape[1]), lambda s: (s, 0))

    specs = [spec(a) for a in ws]
    outs = pl.pallas_call(
        body, name=name, grid=(splits,),
        in_specs=specs * 4, out_specs=specs * 3,
        out_shape=[SDS(a.shape, F32) for a in ws] * 3,
        compiler_params=_params(("parallel",)),
    )(*ws, *gs, *ms, *vs)
    return outs[:n], outs[n:2 * n], outs[2 * n:]


def _adamw_halves(ws, own, other, ms, vs, core):
    n = len(ws)
    splits = _row_splits(own)

    def body(core_ref, *refs):
        mine = pl.program_id(0) == core_ref[0]
        for i in range(n):
            g = jnp.where(mine, refs[n + i][...], refs[2 * n + i][...])
            delta, m, v = _adamw_math(refs[i][...], g, refs[3 * n + i][...], refs[4 * n + i][...])
            refs[5 * n + i][...] = g
            refs[6 * n + i][...] = delta
            refs[7 * n + i][...] = m
            refs[8 * n + i][...] = v

    rows = lambda a: a.shape[0] // splits
    half = [pl.BlockSpec((None, rows(a), a.shape[1]), lambda hh, s, core: (hh, s, 0)) for a in own]
    flat = [pl.BlockSpec((rows(a), a.shape[1]), lambda hh, s, core: (s, 0)) for a in own]
    outs = pl.pallas_call(
        body, name="adamw_big",
        grid_spec=pltpu.PrefetchScalarGridSpec(
            num_scalar_prefetch=1, grid=(2, splits), in_specs=half + flat + flat + half + half, out_specs=half * 4),
        out_shape=[SDS(w.shape, F32) for w in ws] * 4,
        compiler_params=_params(("parallel", "parallel")),
    )(core, *ws, *own, *other, *ms, *vs)
    return outs[:n], outs[n:2 * n], outs[2 * n:3 * n], outs[3 * n:]


def _sum_devices(slabs):
    def body(in_ref, out_ref):
        acc = in_ref[0]
        for k in range(1, N_DEV):
            acc = acc + in_ref[k]
        out_ref[...] = acc

    return pl.pallas_call(
        body, name="small_grad_sum", out_shape=SDS(slabs.shape[1:], F32),
    )(slabs)


def _mm_nn(name, a, w, out_dtype, tm, scale_first_tile=None):
    S, K = a.shape
    N = w.shape[1]
    tn = K

    def body(a_ref, w_ref, o_ref):
        acc = _dot(a_ref[...], w_ref[...])
        if scale_first_tile is not None:
            acc = acc * jnp.where(pl.program_id(1) == 0, scale_first_tile, 1.0)
        o_ref[...] = acc.astype(out_dtype)

    return pl.pallas_call(
        body, name=name, grid=(S // tm, N // tn),
        in_specs=[pl.BlockSpec((tm, K), lambda i, t: (i, 0)), pl.BlockSpec((K, tn), lambda i, t: (0, t))],
        out_specs=pl.BlockSpec((tm, tn), lambda i, t: (i, t)),
        out_shape=SDS((S, N), out_dtype),
        compiler_params=_params(("parallel", "arbitrary")),
    )(a, w)


def _mm_tn(name, a, b, tk):
    S, M = a.shape
    N = b.shape[1]
    tn = M

    def body(a_ref, b_ref, o_ref):
        @pl.when(pl.program_id(1) == 0)
        def _():
            o_ref[...] = jnp.zeros_like(o_ref)
        o_ref[...] += _tdot(a_ref[...], b_ref[...])

    return pl.pallas_call(
        body, name=name, grid=(N // tn, S // tk),
        in_specs=[pl.BlockSpec((tk, M), lambda t, k: (k, 0)), pl.BlockSpec((tk, tn), lambda t, k: (k, t))],
        out_specs=pl.BlockSpec((M, tn), lambda t, k: (0, t)),
        out_shape=SDS((M, N), F32),
        compiler_params=_params(("parallel", "arbitrary")),
    )(a, b)


def _a_in_proj(x, w4, b_in, later_shards, tm):
    S, D = x.shape
    nj = w4.shape[2]
    n = len(later_shards)
    steps = S // tm

    def body(x_ref, w_ref, b_ref, *refs):
        shard_refs, (h_ref, xb_ref), gathered_refs = refs[:n], refs[n:n + 2], refs[n + 2:2 * n + 2]
        start, forward, finish = _gather_stages(shard_refs, gathered_refs, *refs[2 * n + 2:])
        i, t = pl.program_id(0), pl.program_id(1)
        pl.when((i == 0) & (t == 0))(start)
        pl.when((i == steps // 2) & (t == 0))(forward)
        xb = x_ref[...].astype(BF16)

        @pl.when(t == 0)
        def _():
            xb_ref[...] = xb
        h_ref[...] = _dot(xb, w_ref[...]) + b_ref[...]
        pl.when((i == steps - 1) & (t == N_CHIPS - 1))(finish)

    outs = pl.pallas_call(
        body, name="a_in_proj", grid=(steps, N_CHIPS),
        in_specs=[pl.BlockSpec((tm, D), lambda i, t: (i, 0)),
                  pl.BlockSpec((None, D, nj), lambda i, t: (t, 0, 0)),
                  pl.BlockSpec((1, nj), lambda i, t: (0, t))] + _hbm_specs(n),
        out_specs=[pl.BlockSpec((tm, nj), lambda i, t: (i, t)), pl.BlockSpec((tm, D), lambda i, t: (i, 0))]
        + _hbm_specs(n),
        out_shape=[SDS((S, N_CHIPS * nj), F32), SDS((S, D), BF16)]
        + [SDS((N_CHIPS,) + s.shape, s.dtype) for s in later_shards],
        scratch_shapes=_gather_scratch(n),
        compiler_params=_params(("arbitrary", "arbitrary")),
    )(x, w4, b_in, *later_shards)
    return outs[0], outs[1], outs[2:]


def _fill_glu_ext(ext_ref, a_ref, g_ref, ah_ref, gh_ref, has_prev):
    u0h = ah_ref[...] * _sigmoid(gh_ref[...])
    ext_ref[0:HALO, :] = jnp.where(has_prev, u0h, 0.0)
    ext_ref[HALO:, :] = a_ref[...] * _sigmoid(g_ref[...])


def _fill_shifts(shift_ref, ext_ref):
    for s in range(1, 8):
        shift_ref[s - 1] = ext_ref[s:s + shift_ref.shape[1], :]


def _tap_windows(ext_ref, shift_ref, starts, r0, rows, lanes):
    for s in range(8):
        taps = [(k, st) for k, st in enumerate(starts) if st % 8 == s]
        if not taps:
            continue
        lo = min(st for _, st in taps)
        hi = max(st for _, st in taps)
        if s == 0:
            win = ext_ref[r0 + lo:r0 + hi + rows, lanes]
        else:
            win = shift_ref[s - 1, r0 + lo - s:r0 + hi - s + rows, lanes]
        for k, st in taps:
            yield k, win[st - lo:st - lo + rows]


def _a_conv_out(h, x, wdw, bdw, lng, lnb, wout, bout, pg, pb, tm):
    S, D = x.shape
    hb = tm // HALO
    d1, d2 = DILATIONS[1], DILATIONS[2]

    def body(a_ref, g_ref, z_ref, ah_ref, gh_ref, x_ref, wdw_ref, bdw_ref, lng_ref, lnb_ref, wout_ref,
             bout_ref, pg_ref, pb_ref, xhu_ref, rsu_ref, vb_ref, xh1_ref, rs1_ref, x1b_ref, x1p1_ref,
             x1p2_ref, ext_ref, u1_ref, x1_ref, sh_ref):
        i = pl.program_id(0)
        _fill_glu_ext(ext_ref, a_ref, g_ref, ah_ref, gh_ref, i > 0)
        _fill_shifts(sh_ref, ext_ref)
        starts = [HALO - (CONV_WIDTH - 1) + k for k in range(CONV_WIDTH)]
        for cc in range(D // 128):
            lanes = slice(cc * 128, (cc + 1) * 128)
            for r0 in range(0, tm, CONV_ROWS):
                acc = jnp.broadcast_to(bdw_ref[:, lanes], (CONV_ROWS, 128))
                for k, win in _tap_windows(ext_ref, sh_ref, starts, r0, CONV_ROWS, lanes):
                    acc = acc + wdw_ref[k:k + 1, lanes] * win
                u1_ref[r0:r0 + CONV_ROWS, lanes] = acc
        xhu, rsu = _ln_fwd(u1_ref[...])
        xhu_ref[...] = xhu
        rsu_ref[...] = rsu
        u2 = xhu * lng_ref[...] + lnb_ref[...]
        z = z_ref[...]
        v = (u2 * _sigmoid(u2)) * (z * _sigmoid(z))
        vb = v.astype(BF16)
        vb_ref[...] = vb
        s1 = ALPHA * x_ref[...] + _dot(vb, wout_ref[...]) + bout_ref[...]
        xh1, rs1 = _ln_fwd(s1)
        xh1_ref[...] = xh1
        rs1_ref[...] = rs1
        x1 = xh1 * pg_ref[...] + pb_ref[...]
        x1b_ref[...] = x1.astype(BF16)
        _to_chunks(x1_ref, x1)
        _deinterleave(x1_ref, x1p1_ref, d1, BF16)
        _deinterleave(x1_ref, x1p2_ref, d2, BF16)

    tile = lambda c: pl.BlockSpec((tm, D), lambda i, c=c: (i, c))
    halo = lambda c: pl.BlockSpec((HALO, D), lambda i, c=c: (jnp.maximum(i * hb - 1, 0), c))
    row = pl.BlockSpec((1, D), lambda i: (0, 0))
    stat = pl.BlockSpec((tm, 1), lambda i: (i, 0))
    return pl.pallas_call(
        body, name="a_conv_out", grid=(S // tm,),
        in_specs=[tile(0), tile(1), tile(2), halo(0), halo(1), tile(0),
                  pl.BlockSpec((HALO, D), lambda i: (0, 0)), row, row, row,
                  pl.BlockSpec((D, D), lambda i: (0, 0)), row, row, row],
        out_specs=[tile(0), stat, tile(0), tile(0), stat, tile(0),
                   pl.BlockSpec((d1, tm // d1, D), lambda i: (0, i, 0)),
                   pl.BlockSpec((d2, tm // d2, D), lambda i: (0, i, 0))],
        out_shape=[SDS((S, D), F32), SDS((S, 1), F32), SDS((S, D), BF16), SDS((S, D), F32), SDS((S, 1), F32),
                   SDS((S, D), BF16), SDS((d1, S // d1, D), BF16), SDS((d2, S // d2, D), BF16)],
        scratch_shapes=[pltpu.VMEM((HALO + tm, D), F32), pltpu.VMEM((tm, D), F32),
                        pltpu.VMEM((D // 128, tm, 128), F32), pltpu.VMEM((7, HALO + tm - 8, D), F32)],
        compiler_params=_params(("parallel",)),
    )(h, h, h, h, h, x, wdw, bdw, lng, lnb, wout, bout, pg, pb)


def _band(n, dilation):
    qi = lax.broadcasted_iota(jnp.int32, (BLOCK, 2 * BLOCK), 0)
    kj = lax.broadcasted_iota(jnp.int32, (BLOCK, 2 * BLOCK), 1)
    dist = qi + BLOCK - kj
    valid = (dist >= 0) & (dist <= BLOCK) & ((n > 0) | (kj >= BLOCK))
    return jnp.where(valid, dist.astype(F32) * float(-dilation), -jnp.inf)


def _attn_fwd(g, qkv, D):
    S = qkv.shape[0]
    d = DILATIONS[g]
    nb = S // (d * BLOCK)
    H = D // HEAD_DIM
    slopes = _slopes(H)

    def body(q_ref, kp_ref, kc_ref, vp_ref, vc_ref, o_ref, lse_ref):
        neg_dist = _band(pl.program_id(1), d)
        lane = lax.broadcasted_iota(jnp.int32, (BLOCK, LSE_LANES), 1)
        low = lane < HEAD_DIM
        lse = jnp.zeros((BLOCK, LSE_LANES), F32)
        for hp in range(H // 2):
            sl = slice(hp * 128, (hp + 1) * 128)
            q = q_ref[:, sl]
            k = jnp.concatenate([kp_ref[:, sl], kc_ref[:, sl]], axis=0)
            v = jnp.concatenate([vp_ref[:, sl], vc_ref[:, sl]], axis=0)
            o = []
            for a in range(2):
                h = 2 * hp + a
                s = _dot_t(jnp.where(low if a == 0 else ~low, q, jnp.zeros_like(q)), k)
                s = s + slopes[h] * neg_dist
                m = jnp.max(s, axis=1, keepdims=True)
                p = jnp.exp(s - m)
                l = jnp.sum(p, axis=1, keepdims=True)
                o.append(_dot(p.astype(BF16), v) * (1.0 / l))
                lse = jnp.where(lane == h, m + jnp.log(l), lse)
            o_ref[:, sl] = jnp.where(low, o[0], o[1])
        lse_ref[...] = lse

    cur = lambda c: pl.BlockSpec((BLOCK, D), lambda r, n, c=c: (r * nb + n, c))
    prev = lambda c: pl.BlockSpec((BLOCK, D), lambda r, n, c=c: (r * nb + jnp.maximum(n - 1, 0), c))
    return pl.pallas_call(
        body, name="attn_fwd_g%d" % g, grid=(d, nb),
        in_specs=[cur(0), prev(1), cur(1), prev(2), cur(2)],
        out_specs=[cur(0), pl.BlockSpec((BLOCK, LSE_LANES), lambda r, n: (r * nb + n, 0))],
        out_shape=[SDS((S, D), F32), SDS((S, LSE_LANES), F32)],
        compiler_params=_params(("parallel", "parallel")),
    )(qkv, qkv, qkv, qkv, qkv)


def _attn_bwd(g, qkv, do, lse, delta, D):
    S = qkv.shape[0]
    d = DILATIONS[g]
    nb = S // (d * BLOCK)
    H = D // HEAD_DIM
    slopes = _slopes(H)

    def body(q_ref, kp_ref, kc_ref, vp_ref, vc_ref, do_ref, lse_ref, dl_ref, dq_ref, dkv_ref, ck_ref, cv_ref):
        n = pl.program_id(1)

        @pl.when(n == 0)
        def _():
            ck_ref[...] = jnp.zeros_like(ck_ref)
            cv_ref[...] = jnp.zeros_like(cv_ref)

        @pl.when(n < nb)
        def _():
            neg_dist = _band(n, d)
            low = lax.broadcasted_iota(jnp.int32, (BLOCK, 128), 1) < HEAD_DIM
            low2 = lax.broadcasted_iota(jnp.int32, (2 * BLOCK, 128), 1) < HEAD_DIM
            for hp in range(H // 2):
                sl = slice(hp * 128, (hp + 1) * 128)
                q = q_ref[:, sl]
                do2 = do_ref[:, sl]
                k = jnp.concatenate([kp_ref[:, sl], kc_ref[:, sl]], axis=0)
                v = jnp.concatenate([vp_ref[:, sl], vc_ref[:, sl]], axis=0)
                zero = jnp.zeros_like(q)
                q_do = jnp.concatenate([jnp.concatenate([q, zero], axis=1),
                                        jnp.concatenate([zero, do2], axis=1)], axis=0)
                dq, dkv = [], []
                for a in range(2):
                    h = 2 * hp + a
                    keep = low if a == 0 else ~low
                    s = _dot_t(jnp.where(keep, q, zero), k)
                    s = s + slopes[h] * neg_dist
                    p = jnp.exp(s - lse_ref[:, h:h + 1])
                    dp = _dot_t(jnp.where(keep, do2, zero), v)
                    dsb = (p * (dp - dl_ref[:, h:h + 1])).astype(BF16)
                    dq.append(_dot(dsb, k))
                    dkv.append(_tdot(jnp.concatenate([dsb, p.astype(BF16)], axis=0), q_do))
                dq_ref[:, sl] = (jnp.where(low, dq[0], dq[1]) * (HEAD_DIM ** -0.5)).astype(BF16)
                dk2 = jnp.where(low2, dkv[0][:, :128], dkv[1][:, :128])
                dv2 = jnp.where(low2, dkv[0][:, 128:], dkv[1][:, 128:])
                dkv_ref[:, sl] = (ck_ref[:, sl] + dk2[:BLOCK]).astype(BF16)
                dkv_ref[:, D + hp * 128:D + (hp + 1) * 128] = (cv_ref[:, sl] + dv2[:BLOCK]).astype(BF16)
                ck_ref[:, sl] = dk2[BLOCK:]
                cv_ref[:, sl] = dv2[BLOCK:]

        @pl.when(n == nb)
        def _():
            dkv_ref[:, :D] = ck_ref[...].astype(BF16)
            dkv_ref[:, D:] = cv_ref[...].astype(BF16)

    nq = lambda n: jnp.minimum(n, nb - 1)
    cur = lambda c: pl.BlockSpec((BLOCK, D), lambda r, n, c=c: (r * nb + nq(n), c))
    prev = lambda c: pl.BlockSpec((BLOCK, D), lambda r, n, c=c: (r * nb + jnp.maximum(nq(n) - 1, 0), c))
    stat = pl.BlockSpec((BLOCK, LSE_LANES), lambda r, n: (r * nb + nq(n), 0))
    return pl.pallas_call(
        body, name="attn_bwd_g%d" % g, grid=(d, nb + 1),
        in_specs=[cur(0), prev(1), cur(1), prev(2), cur(2), cur(0), stat, stat],
        out_specs=[cur(0), pl.BlockSpec((BLOCK, 2 * D), lambda r, n: (r * nb + jnp.maximum(n - 1, 0), 0))],
        out_shape=[SDS((S, D), BF16), SDS((S, 2 * D), BF16)],
        scratch_shapes=[pltpu.VMEM((BLOCK, D), F32), pltpu.VMEM((BLOCK, D), F32)],
        compiler_params=_params(("parallel", "arbitrary")),
    )(qkv, qkv, qkv, qkv, qkv, do, lse, delta)


def _b_merge_out_loss(o0, o1, o2, l0, l1, l2, z2, xh1, target, wbo, bbo, pg0, pb0, pg1, pb1, tm):
    S, D = o0.shape
    H = D // HEAD_DIM
    d1, d2 = DILATIONS[1], DILATIONS[2]
    inv_d = 1.0 / D

    def body(o0_ref, o1_ref, o2_ref, l0_ref, l1_ref, l2_ref, z_ref, xh1_ref, t_ref, wbo_ref, bbo_ref,
             pg0_ref, pb0_ref, pg1_ref, pb1_ref,
             v2b_ref, ds2_ref, ds2b_ref, dz2b_ref, da0_ref, da1_ref, da2_ref, ls0_ref, ls1_ref, ls2_ref,
             dl0_ref, dl1_ref, dl2_ref, loss_ref, sums_ref,
             o1n_ref, o2n_ref, l1n_ref, l2n_ref, att_ref):
        i = pl.program_id(0)
        _interleave(o1_ref, o1n_ref, d1)
        _interleave(o2_ref, o2n_ref, d2)
        for r in range(d1):
            l1n_ref[pl.ds(r, tm // d1, stride=d1), :] = l1_ref[r]
        for r in range(d2):
            l2n_ref[pl.ds(r, tm // d2, stride=d2), :] = l2_ref[r]
        la, lb, lc = l0_ref[...], l1n_ref[...], l2n_ref[...]
        m = jnp.maximum(jnp.maximum(la, lb), lc)
        ea, eb, ec = jnp.exp(la - m), jnp.exp(lb - m), jnp.exp(lc - m)
        den = ea + eb + ec
        wa, wb, wc = ea / den, eb / den, ec / den
        ls0_ref[...] = m + jnp.log(den)
        for h in range(H):
            sl = slice(h * HEAD_DIM, (h + 1) * HEAD_DIM)
            cc, hl = divmod(h * HEAD_DIM, 128)
            att_ref[:, sl] = (wa[:, h:h + 1] * o0_ref[:, sl] + wb[:, h:h + 1] * o1n_ref[cc, :, hl:hl + HEAD_DIM]
                              + wc[:, h:h + 1] * o2n_ref[cc, :, hl:hl + HEAD_DIM])
        att = att_ref[...]
        z = z_ref[...]
        sz = _sigmoid(z)
        gate = z * sz
        v2b = (att * gate).astype(BF16)
        v2b_ref[...] = v2b
        x1 = xh1_ref[...] * pg0_ref[...] + pb0_ref[...]
        s2 = ALPHA * x1 + _dot(v2b, wbo_ref[...]) + bbo_ref[...]
        xh2, rs2 = _ln_fwd(s2)
        err = xh2 * pg1_ref[...] + pb1_ref[...] - t_ref[...]
        dy = err * inv_d
        ds2 = _ln_bwd(dy * pg1_ref[...], xh2, rs2)
        ds2b = ds2.astype(BF16)
        ds2_ref[...] = ds2
        ds2b_ref[...] = ds2b

        @pl.when(i == 0)
        def _():
            loss_ref[...] = jnp.zeros_like(loss_ref)
            sums_ref[...] = jnp.zeros_like(sums_ref)
        loss_ref[...] += 0.5 * inv_d * jnp.sum(err * err)
        sums_ref[0:1, :] += _colsum(dy * xh2)
        sums_ref[1:2, :] += _colsum(dy)
        sums_ref[2:3, :] += _colsum(ds2)

        dv2 = _dot_t(ds2b, wbo_ref[...])
        datt = dv2 * gate
        dz2b_ref[...] = (dv2 * att * _silu_grad(z, sz)).astype(BF16)
        prod = datt * att
        lane = lax.broadcasted_iota(jnp.int32, (tm, LSE_LANES), 1)
        dl = jnp.zeros((tm, LSE_LANES), F32)
        for h in range(H):
            sl = slice(h * HEAD_DIM, (h + 1) * HEAD_DIM)
            dl = jnp.where(lane == h, jnp.sum(prod[:, sl], axis=1, keepdims=True), dl)
        da0_ref[...] = datt.astype(BF16)
        dl0_ref[...] = dl
        _to_chunks(o1n_ref, datt)
        _deinterleave(o1n_ref, da1_ref, d1, BF16)
        _deinterleave(o1n_ref, da2_ref, d2, BF16)
        for r in range(d1):
            ls1_ref[r] = ls0_ref[pl.ds(r, tm // d1, stride=d1), :]
            dl1_ref[r] = dl0_ref[pl.ds(r, tm // d1, stride=d1), :]
        for r in range(d2):
            ls2_ref[r] = ls0_ref[pl.ds(r, tm // d2, stride=d2), :]
            dl2_ref[r] = dl0_ref[pl.ds(r, tm // d2, stride=d2), :]

    tile = pl.BlockSpec((tm, D), lambda i: (i, 0))
    stat = pl.BlockSpec((tm, LSE_LANES), lambda i: (i, 0))
    perm = lambda d, w: pl.BlockSpec((d, tm // d, w), lambda i: (0, i, 0))
    row = pl.BlockSpec((1, D), lambda i: (0, 0))
    acc = lambda w: pl.BlockSpec((8, w), lambda i: (0, 0))
    pshape = lambda d, w, dt: SDS((d, S // d, w), dt)
    return pl.pallas_call(
        body, name="b_merge_out_loss", grid=(S // tm,),
        in_specs=[tile, perm(d1, D), perm(d2, D), stat, perm(d1, LSE_LANES), perm(d2, LSE_LANES),
                  tile, tile, tile, pl.BlockSpec((D, D), lambda i: (0, 0)), row, row, row, row, row],
        out_specs=[tile, tile, tile, tile, tile, perm(d1, D), perm(d2, D),
                   stat, perm(d1, LSE_LANES), perm(d2, LSE_LANES),
                   stat, perm(d1, LSE_LANES), perm(d2, LSE_LANES), acc(LSE_LANES), acc(D)],
        out_shape=[SDS((S, D), BF16), SDS((S, D), F32), SDS((S, D), BF16), SDS((S, D), BF16),
                   SDS((S, D), BF16), pshape(d1, D, BF16), pshape(d2, D, BF16),
                   SDS((S, LSE_LANES), F32), pshape(d1, LSE_LANES, F32), pshape(d2, LSE_LANES, F32),
                   SDS((S, LSE_LANES), F32), pshape(d1, LSE_LANES, F32), pshape(d2, LSE_LANES, F32),
                   SDS((8, LSE_LANES), F32), SDS((8, D), F32)],
        scratch_shapes=[pltpu.VMEM((D // 128, tm, 128), F32), pltpu.VMEM((D // 128, tm, 128), F32),
                        pltpu.VMEM((tm, LSE_LANES), F32), pltpu.VMEM((tm, LSE_LANES), F32),
                        pltpu.VMEM((tm, D), F32)],
        compiler_params=_params(("arbitrary",)),
    )(o0, o1, o2, l0, l1, l2, z2, xh1, target, wbo, bbo, pg0, pb0, pg1, pb1)


def _b_dx1_ln1_bwd(ds2, dz2b, dq, dkv, xh1, rs1, wz, wg, pg0, tm):
    S, D = ds2.shape
    d1, d2 = DILATIONS[1], DILATIONS[2]

    def group_part(dq_blk, dkv_blk, w_ref):
        return (_dot_t(dq_blk, w_ref[:, 0:D]) + _dot_t(dkv_blk[:, 0:D], w_ref[:, D:2 * D])
                + _dot_t(dkv_blk[:, D:2 * D], w_ref[:, 2 * D:3 * D]))

    def body(ds2_ref, dz_ref, dq0_ref, dkv0_ref, dq1_ref, dkv1_ref, dq2_ref, dkv2_ref, xh1_ref, rs1_ref,
             wz_ref, w0_ref, w1_ref, w2_ref, pg0_ref, ds1_ref, ds1b_ref, sums_ref, acc_ref):
        i = pl.program_id(0)
        _to_chunks(acc_ref, ALPHA * ds2_ref[...] + _dot_t(dz_ref[...], wz_ref[...])
                   + group_part(dq0_ref[...], dkv0_ref[...], w0_ref))
        for d, dq_ref, dkv_ref, w_ref in ((d1, dq1_ref, dkv1_ref, w1_ref), (d2, dq2_ref, dkv2_ref, w2_ref)):
            rows = tm // d
            part = group_part(dq_ref[...].reshape(tm, D), dkv_ref[...].reshape(tm, 2 * D), w_ref)
            for r in range(d):
                idx = pl.ds(r, rows, stride=d)
                for cc in range(D // 128):
                    acc_ref[cc, idx, :] = acc_ref[cc, idx, :] + part[r * rows:(r + 1) * rows, cc * 128:(cc + 1) * 128]
        dx1 = jnp.concatenate([acc_ref[cc] for cc in range(D // 128)], axis=1)
        xh1 = xh1_ref[...]
        ds1 = _ln_bwd(dx1 * pg0_ref[...], xh1, rs1_ref[...])
        ds1_ref[...] = ds1
        ds1b_ref[...] = ds1.astype(BF16)

        @pl.when(i == 0)
        def _():
            sums_ref[...] = jnp.zeros_like(sums_ref)
        sums_ref[0:1, :] += _colsum(dx1 * xh1)
        sums_ref[1:2, :] += _colsum(dx1)
        sums_ref[2:3, :] += _colsum(ds1)

    tile = lambda w: pl.BlockSpec((tm, w), lambda i: (i, 0))
    perm = lambda d, w: pl.BlockSpec((d, tm // d, w), lambda i: (0, i, 0))
    whole = pl.BlockSpec(memory_space=pltpu.VMEM)
    return pl.pallas_call(
        body, name="b_dx1_ln1_bwd", grid=(S // tm,),
        in_specs=[tile(D), tile(D), tile(D), tile(2 * D), perm(d1, D), perm(d1, 2 * D), perm(d2, D),
                  perm(d2, 2 * D), tile(D), tile(1), whole, whole, whole, whole,
                  pl.BlockSpec((1, D), lambda i: (0, 0))],
        out_specs=[tile(D), tile(D), pl.BlockSpec((8, D), lambda i: (0, 0))],
        out_shape=[SDS((S, D), F32), SDS((S, D), BF16), SDS((8, D), F32)],
        scratch_shapes=[pltpu.VMEM((D // 128, tm, 128), F32)],
        compiler_params=_params(("arbitrary",)),
    )(ds2, dz2b, dq[0], dkv[0], dq[1].reshape(d1, S // d1, D), dkv[1].reshape(d1, S // d1, 2 * D),
      dq[2].reshape(d2, S // d2, D), dkv[2].reshape(d2, S // d2, 2 * D), xh1, rs1, wz, wg[0], wg[1], wg[2], pg0)


def _a_gate_bwd(ds1b, h, xhu, rsu, wout, lng, lnb, grads, tm):
    S, D = xhu.shape
    n = len(grads)
    steps = S // tm

    def body(ds_ref, z_ref, xhu_ref, rsu_ref, w_ref, lng_ref, lnb_ref, *refs):
        grad_refs, (du1_ref, dzb_ref, sums_ref), recvd_refs = refs[:n], refs[n:n + 3], refs[n + 3:2 * n + 3]
        exchange = (grad_refs, recvd_refs, *refs[2 * n + 3:])
        i = pl.program_id(0)
        pl.when(i == 0)(functools.partial(_exchange_start, _PairExchange, *exchange))
        dv = _dot_t(ds_ref[...], w_ref[...])
        xhu = xhu_ref[...]
        u2 = xhu * lng_ref[...] + lnb_ref[...]
        su = _sigmoid(u2)
        z = z_ref[...]
        sz = _sigmoid(z)
        dz = dv * (u2 * su) * _silu_grad(z, sz)
        du2 = dv * (z * sz) * _silu_grad(u2, su)
        du1 = _ln_bwd(du2 * lng_ref[...], xhu, rsu_ref[...])
        du1_ref[...] = du1
        dzb_ref[...] = dz.astype(BF16)

        @pl.when(i == 0)
        def _():
            sums_ref[...] = jnp.zeros_like(sums_ref)
        sums_ref[0:1, :] += _colsum(du2 * xhu)
        sums_ref[1:2, :] += _colsum(du2)
        sums_ref[2:3, :] += _colsum(du1)
        sums_ref[3:4, :] += _colsum(dz)
        pl.when(i == steps - 1)(functools.partial(_exchange_finish, _PairExchange, *exchange))

    tile = pl.BlockSpec((tm, D), lambda i: (i, 0))
    row = pl.BlockSpec((1, D), lambda i: (0, 0))
    outs = pl.pallas_call(
        body, name="a_gate_bwd", grid=(steps,),
        in_specs=[tile, pl.BlockSpec((tm, D), lambda i: (i, 2)), tile, pl.BlockSpec((tm, 1), lambda i: (i, 0)),
                  pl.BlockSpec((D, D), lambda i: (0, 0)), row, row] + _hbm_specs(n),
        out_specs=[tile, tile, pl.BlockSpec((8, D), lambda i: (0, 0))] + _hbm_specs(n),
        out_shape=[SDS((S, D), F32), SDS((S, D), BF16), SDS((8, D), F32)]
        + [_PairExchange.out_shape(g) for g in grads],
        scratch_shapes=_exchange_scratch(_PairExchange, n),
        compiler_params=_params(("arbitrary",)),
    )(ds1b, h, xhu, rsu, wout, lng, lnb, *grads)
    return outs[0], outs[1], outs[2], outs[3:]


def _a_conv_bwd(du1, h, wdw, parts, tm):
    S, D = du1.shape
    hb = tm // HALO
    last_halo = S // HALO - 1
    n_tiles = S // tm
    n = len(parts)

    def body(du_ref, dun_ref, a_ref, g_ref, ah_ref, gh_ref, wdw_ref, *refs):
        part_refs, (dag_ref, sums_ref, wsum_ref), landed_refs = refs[:n], refs[n:n + 3], refs[n + 3:2 * n + 3]
        dext_ref, ext_ref, dsh_ref, sh_ref, wacc_ref, send_sems, recv_sems = refs[2 * n + 3:]
        scatter = (part_refs, landed_refs, send_sems, recv_sems)
        i = pl.program_id(0)

        @pl.when(i == 0)
        def _():
            _exchange_start(_ChipScatter, *scatter)
            sums_ref[...] = jnp.zeros_like(sums_ref)
            wsum_ref[...] = jnp.zeros_like(wsum_ref)
            wacc_ref[...] = jnp.zeros_like(wacc_ref)
        dext_ref[0:tm, :] = du_ref[...]
        dext_ref[tm:, :] = jnp.where(i < n_tiles - 1, dun_ref[...], 0.0)
        _fill_shifts(dsh_ref, dext_ref)
        _fill_glu_ext(ext_ref, a_ref, g_ref, ah_ref, gh_ref, i > 0)
        _fill_shifts(sh_ref, ext_ref)
        back = [CONV_WIDTH - 1 - k for k in range(CONV_WIDTH)]
        fwd = [HALO - (CONV_WIDTH - 1) + k for k in range(CONV_WIDTH)]
        for cc in range(D // 128):
            lanes = slice(cc * 128, (cc + 1) * 128)
            hi_lanes = slice(D + cc * 128, D + (cc + 1) * 128)
            sa = jnp.zeros((1, 128), F32)
            sg = jnp.zeros((1, 128), F32)
            for r0 in range(0, tm, CONV_ROWS):
                acc = jnp.zeros((CONV_ROWS, 128), F32)
                for k, win in _tap_windows(dext_ref, dsh_ref, back, r0, CONV_ROWS, lanes):
                    acc = acc + wdw_ref[k:k + 1, lanes] * win
                a = a_ref[r0:r0 + CONV_ROWS, lanes]
                s = _sigmoid(g_ref[r0:r0 + CONV_ROWS, lanes])
                da = acc * s
                dg = acc * a * s * (1.0 - s)
                dag_ref[r0:r0 + CONV_ROWS, lanes] = da.astype(BF16)
                dag_ref[r0:r0 + CONV_ROWS, hi_lanes] = dg.astype(BF16)
                sa = sa + _colsum(da)
                sg = sg + _colsum(dg)
                du = du_ref[r0:r0 + CONV_ROWS, lanes]
                for k, win in _tap_windows(ext_ref, sh_ref, fwd, r0, CONV_ROWS, lanes):
                    p = du * win
                    fold = p[0:8]
                    for q in range(8, CONV_ROWS, 8):
                        fold = fold + p[q:q + 8]
                    wacc_ref[k, :, lanes] += fold
            sums_ref[0:1, lanes] += sa
            sums_ref[1:2, lanes] += sg

        @pl.when(i == n_tiles - 1)
        def _():
            for k in range(CONV_WIDTH):
                wsum_ref[k:k + 1, :] = _colsum(wacc_ref[k])
            _exchange_finish(_ChipScatter, *scatter)

    tile = lambda c: pl.BlockSpec((tm, D), lambda i, c=c: (i, c))
    halo = lambda c: pl.BlockSpec((HALO, D), lambda i, c=c: (jnp.maximum(i * hb - 1, 0), c))
    outs = pl.pallas_call(
        body, name="a_conv_bwd", grid=(n_tiles,),
        in_specs=[tile(0), pl.BlockSpec((HALO, D), lambda i: (jnp.minimum((i + 1) * hb, last_halo), 0)),
                  tile(0), tile(1), halo(0), halo(1), pl.BlockSpec((HALO, D), lambda i: (0, 0))] + _hbm_specs(n),
        out_specs=[pl.BlockSpec((tm, 2 * D), lambda i: (i, 0)), pl.BlockSpec((8, D), lambda i: (0, 0)),
                   pl.BlockSpec((HALO, D), lambda i: (0, 0))] + _hbm_specs(n),
        out_shape=[SDS((S, 2 * D), BF16), SDS((8, D), F32), SDS((HALO, D), F32)]
        + [SDS((3,) + p.shape[1:], p.dtype) for p in parts],
        scratch_shapes=[pltpu.VMEM((tm + HALO, D), F32), pltpu.VMEM((HALO + tm, D), F32),
                        pltpu.VMEM((7, HALO + tm - 8, D), F32), pltpu.VMEM((7, HALO + tm - 8, D), F32),
                        pltpu.VMEM((HALO, 8, D), F32)] + _exchange_scratch(_ChipScatter, n),
        compiler_params=_params(("arbitrary",)),
    )(du1, du1, h, h, h, h, wdw, *parts)
    return outs[0], outs[1], outs[2], outs[3:]


def _a_dx(ds1, dag, dzb, w_in, parts, tm):
    S, D = ds1.shape
    n = len(parts)
    steps = S // tm

    def body(ds_ref, dag_ref, dz_ref, w_ref, *refs):
        part_refs, o_ref, landed_refs = refs[:n], refs[n], refs[n + 1:2 * n + 1]
        scatter = (part_refs, landed_refs, *refs[2 * n + 1:])
        i = pl.program_id(0)
        pl.when(i == 0)(functools.partial(_exchange_start, _ChipScatter, *scatter))
        o_ref[...] = (ALPHA * ds_ref[...] + _dot_t(dag_ref[...], w_ref[:, 0:2 * D])
                      + _dot_t(dz_ref[...], w_ref[:, 2 * D:3 * D]))
        pl.when(i == steps - 1)(functools.partial(_exchange_finish, _ChipScatter, *scatter))

    tile = lambda w: pl.BlockSpec((tm, w), lambda i: (i, 0))
    outs = pl.pallas_call(
        body, name="a_dx", grid=(steps,),
        in_specs=[tile(D), tile(2 * D), tile(D), pl.BlockSpec(memory_space=pltpu.VMEM)] + _hbm_specs(n),
        out_specs=[tile(D)] + _hbm_specs(n),
        out_shape=[SDS((S, D), F32)] + [_ChipScatter.out_shape(p) for p in parts],
        scratch_shapes=_exchange_scratch(_ChipScatter, n),
        compiler_params=_params(("arbitrary",)),
    )(ds1, dag, dzb, w_in, *parts)
    return outs[0], outs[1:]


def _halves(w):
    return w.reshape(2, w.shape[0] // 2, w.shape[1])


def _unstack_cols(w4):
    return jnp.transpose(w4, (1, 0, 2)).reshape(w4.shape[1], N_CHIPS * w4.shape[2])


def _stack_cols(w):
    D, n = w.shape
    return jnp.transpose(w.reshape(D, N_CHIPS, n // N_CHIPS), (1, 0, 2))


def _pack_rows(rows, width):
    slab = jnp.concatenate([r.reshape(-1, width) for r in rows], axis=0)
    return jnp.pad(slab, ((0, SMALL_ROWS - slab.shape[0]), (0, 0)))


def kernel(x, a_w_in, a_b_in, a_w_dw, a_b_dw, a_ln_g, a_ln_b, a_w_out, a_b_out, kv_w, b_w_in, b_w_out, b_b_out, post_ln_g, post_ln_b, loss_target, m_a_w_in, m_a_b_in, m_a_w_dw, m_a_b_dw, m_a_ln_g, m_a_ln_b, m_a_w_out, m_a_b_out, m_kv_w, m_b_w_in, m_b_w_out, m_b_b_out, m_post_ln_g, m_post_ln_b, v_a_w_in, v_a_b_in, v_a_w_dw, v_a_b_dw, v_a_ln_g, v_a_ln_b, v_a_w_out, v_a_b_out, v_kv_w, v_b_w_in, v_b_w_out, v_b_b_out, v_post_ln_g, v_post_ln_b):
    S, D = x.shape[1], x.shape[2]
    dq4 = D // N_CHIPS
    tm = 256
    tm_mm = min(S, 1024)
    x2 = x.reshape(S, D)
    target = loss_target.reshape(S, D)
    jchip = 2 * lax.axis_index("x") + lax.axis_index("y")

    big_local = [a_w_in[0], kv_w, b_w_in[0], a_w_out[0], b_w_out[0]]
    small_local = _pack_rows([a_b_in.reshape(3, dq4), jnp.pad(a_w_dw[0], ((0, 1), (0, 0))), a_b_dw, a_ln_g,
                              a_ln_b, a_b_out], dq4)
    wire = [_halves(w.astype(BF16)) for w in big_local]
    whole = lambda g: g.reshape((N_CHIPS, 2 * g.shape[2], g.shape[3]))
    gathered = _all_gather_chips([wire[0], wire[3], _halves(small_local)])
    w_in4 = whole(gathered[0])
    w_in_a = _unstack_cols(w_in4)
    w_out_a = gathered[1].reshape(D, D)
    small = jnp.transpose(gathered[2].reshape(N_CHIPS, SMALL_ROWS, dq4), (1, 0, 2))
    b_in_full = jnp.transpose(small[0:3], (1, 0, 2)).reshape(1, 3 * D)
    wdw_full = small[3:3 + HALO].reshape(HALO, D)
    bdw_full, lng_full, lnb_full, bout_a_full = [small[35 + q].reshape(1, D) for q in range(4)]
    pg0, pg1 = post_ln_g[0:1], post_ln_g[1:2]
    pb0, pb1 = post_ln_b[0:1], post_ln_b[1:2]

    h, xb, later = _a_in_proj(x2, w_in4, b_in_full, [wire[1], wire[2], wire[4]], tm_mm)
    kv_full = _unstack_cols(whole(later[0]))
    b_in4 = whole(later[1])
    w_out_b = later[2].reshape(D, D)
    w_z = b_in4[3]
    w_g = [jnp.concatenate([b_in4[g], kv_full[:, g * D:(g + 1) * D], kv_full[:, (3 + g) * D:(4 + g) * D]], axis=1)
           for g in range(3)]
    xhu, rsu, vb, xh1, rs1, x1b, x1p1, x1p2 = _a_conv_out(
        h, x2, wdw_full, bdw_full, lng_full, lnb_full, w_out_a, bout_a_full, pg0, pb0, tm)
    x1g = [x1b, x1p1.reshape(S, D), x1p2.reshape(S, D)]
    qkv = [_mm_nn("b_qkv_g%d" % g, x1g[g], w_g[g], BF16, tm_mm, scale_first_tile=HEAD_DIM ** -0.5)
           for g in range(3)]
    z2 = _mm_nn("b_gate_proj", x1b, w_z, F32, tm_mm)
    og, lg = zip(*[_attn_fwd(g, qkv[g], D) for g in range(3)])
    d1, d2 = DILATIONS[1], DILATIONS[2]
    (v2b, ds2, ds2b, dz2b, da0, da1, da2, ls0, ls1, ls2, dl0, dl1, dl2, loss_acc, sums_b) = _b_merge_out_loss(
        og[0], og[1].reshape(d1, S // d1, D), og[2].reshape(d2, S // d2, D),
        lg[0], lg[1].reshape(d1, S // d1, LSE_LANES), lg[2].reshape(d2, S // d2, LSE_LANES),
        z2, xh1, target, w_out_b, b_b_out, pg0, pb0, pg1, pb1, tm)

    das = [da0, da1.reshape(S, D), da2.reshape(S, D)]
    lss = [ls0, ls1.reshape(S, LSE_LANES), ls2.reshape(S, LSE_LANES)]
    dls = [dl0, dl1.reshape(S, LSE_LANES), dl2.reshape(S, LSE_LANES)]
    dq, dkv = zip(*[_attn_bwd(g, qkv[g], das[g], lss[g], dls[g], D) for g in range(3)])
    ds1, ds1b, sums_1 = _b_dx1_ln1_bwd(ds2, dz2b, dq, dkv, xh1, rs1, w_z, w_g, pg0, tm)

    def by_chip_cols(gw):
        s4 = _stack_cols(gw)
        return s4.reshape(N_CHIPS, 2, D // 2, s4.shape[2])

    def by_chip_rows(gw):
        return gw.reshape(N_CHIPS, 2, D // 8, D)

    core = lax.axis_index("c").astype(jnp.int32).reshape(1)
    chip = jchip.astype(jnp.int32).reshape(1)

    g_w_out_b = _mm_tn("dw_b_out", v2b, ds2b, tm_mm)
    g_q = [_mm_tn("dw_b_q_g%d" % g, x1g[g], dq[g], tm_mm) for g in range(3)]
    g_z = _mm_tn("dw_b_z", x1b, dz2b, tm_mm)
    g_kvg = [_mm_tn("dw_kv_g%d" % g, x1g[g], dkv[g], tm_mm) for g in range(3)]
    g_kv = jnp.concatenate([t[:, :D] for t in g_kvg] + [t[:, D:] for t in g_kvg], axis=1)
    grads_b = [by_chip_cols(g_kv), jnp.stack(g_q + [g_z]).reshape(N_CHIPS, 2, D // 2, D), by_chip_rows(g_w_out_b)]
    du1, dzab, sums_a, recvd_b = _a_gate_bwd(ds1b, h, xhu, rsu, w_out_a, lng_full, lnb_full, grads_b, tm)
    parts_b, wire_b = _pair_sum(grads_b, recvd_b, core, "b")
    dag, sums_c, wsum, landed_b = _a_conv_bwd(du1, h, wdw_full, wire_b, tm)
    own_b = _chip_sum(parts_b, landed_b, chip, "b")
    other_b = _pair_share(own_b, "b")

    g_w_in = jnp.concatenate([_mm_tn("dw_a_in_ag", xb, dag, tm_mm), _mm_tn("dw_a_in_z", xb, dzab, tm_mm)], axis=1)
    g_w_out_a = _mm_tn("dw_a_out", vb, ds1b, tm_mm)
    grads_a = [by_chip_cols(g_w_in), by_chip_rows(g_w_out_a)]
    recvd_a = _exchange(_PairExchange, "grad_pair_exchange_a", grads_a)
    parts_a, wire_a = _pair_sum(grads_a, recvd_a, core, "a")
    grad_x, landed_a = _a_dx(ds1, dag, dzab, w_in_a, wire_a, 2 * tm)
    own_a = _chip_sum(parts_a, landed_a, chip, "a")
    other_a = _pair_share(own_a, "a")
    own_half = [own_a[0], own_b[0], own_b[1], own_a[1], own_b[2]]
    other_half = [other_a[0], other_b[0], other_b[1], other_a[1], other_b[2]]

    small_grads = _pack_rows([sums_c[0:1], sums_c[1:2], sums_a[3:4], wsum, sums_a[2:3], sums_a[0:1], sums_a[1:2],
                              sums_1[2:3], sums_b[2:3], sums_1[0:1], sums_b[0:1], sums_1[1:2], sums_b[1:2]], D)
    small_sum = _sum_devices(_gather_all_devices(small_grads))
    loss = lax.psum(loss_acc[0, 0], ("x", "y", "c"))

    big_m = [m_a_w_in[0], m_kv_w, m_b_w_in[0], m_a_w_out[0], m_b_w_out[0]]
    big_v = [v_a_w_in[0], v_kv_w, v_b_w_in[0], v_a_w_out[0], v_b_w_out[0]]
    shards, big_delta, big_new_m, big_new_v = [
        [a.reshape(2 * a.shape[1], a.shape[2]) for a in group] for group in _adamw_halves(
            [_halves(w) for w in big_local], own_half, other_half, [_halves(m) for m in big_m],
            [_halves(v) for v in big_v], core)]

    def chip_cols(rows):
        return lax.dynamic_slice_in_dim(rows, jchip * dq4, dq4, axis=1)

    g_b_in = lax.dynamic_slice_in_dim(small_sum[0:3].reshape(1, 3 * D), jchip * 3 * dq4, 3 * dq4, axis=1)
    small_g = [g_b_in, chip_cols(small_sum[3:3 + CONV_WIDTH]), chip_cols(small_sum[35:36]), chip_cols(small_sum[36:37]),
               chip_cols(small_sum[37:38]), chip_cols(small_sum[38:39]), small_sum[39:40], small_sum[40:42],
               small_sum[42:44]]
    small_w = [a_b_in, a_w_dw[0], a_b_dw, a_ln_g, a_ln_b, a_b_out, b_b_out, post_ln_g, post_ln_b]
    small_m = [m_a_b_in, m_a_w_dw[0], m_a_b_dw, m_a_ln_g, m_a_ln_b, m_a_b_out, m_b_b_out, m_post_ln_g, m_post_ln_b]
    small_v = [v_a_b_in, v_a_w_dw[0], v_a_b_dw, v_a_ln_g, v_a_ln_b, v_a_b_out, v_b_b_out, v_post_ln_g, v_post_ln_b]
    small_delta, small_new_m, small_new_v = _adamw("adamw_small", small_w, small_g, small_m, small_v, 1)

    def ordered(big, sm):
        return (big[0][None], sm[0], sm[1][None], sm[2], sm[3], sm[4], big[3][None], sm[5], big[1], big[2][None],
                big[4][None], sm[6], sm[7], sm[8])

    return (loss, grad_x.reshape(1, S, D), *ordered(shards, small_g), *ordered(big_delta, small_delta),
            *ordered(big_new_m, small_new_m), *ordered(big_new_v, small_new_v))
```

```python
import functools

import numpy as np
import jax
import jax.numpy as jnp
from jax import lax
from jax.experimental import pallas as pl
from jax.experimental.pallas import tpu as pltpu

F32 = jnp.float32
BF16 = jnp.bfloat16
MESH = pl.DeviceIdType.MESH
SDS = jax.ShapeDtypeStruct

HEAD_DIM = 64
BLOCK = 128
DILATIONS = (1, 4, 16)
ALIBI_MAX_EXP = 8.0
CONV_WIDTH = 31
HALO = 32
CONV_ROWS = 128
LSE_LANES = 128
DEPTH = 2
ALPHA = (2.0 * DEPTH) ** 0.25
LN_EPS = 1e-5
ADAM_LR = 0.001
ADAM_B1 = 0.9
ADAM_B2 = 0.999
ADAM_EPS = 1e-08
ADAM_WD = 0.01
ADAM_STEP = 10
N_CHIPS = 4
N_DEV = 8
VMEM_LIMIT = 56 * 2 ** 20
SMALL_ROWS = 48


def _params(sem=None):
    return pltpu.CompilerParams(dimension_semantics=sem, vmem_limit_bytes=VMEM_LIMIT)


def _sigmoid(x):
    return 1.0 / (1.0 + jnp.exp(-x))


def _silu_grad(x, s):
    return s * (1.0 + x * (1.0 - s))


def _ln_fwd(x):
    mu = jnp.mean(x, axis=-1, keepdims=True)
    xc = x - mu
    var = jnp.mean(xc * xc, axis=-1, keepdims=True)
    rstd = lax.rsqrt(var + LN_EPS)
    return xc * rstd, rstd


def _ln_bwd(dxhat, xhat, rstd):
    m1 = jnp.mean(dxhat, axis=-1, keepdims=True)
    m2 = jnp.mean(dxhat * xhat, axis=-1, keepdims=True)
    return rstd * (dxhat - m1 - xhat * m2)


def _dot(a, b):
    return jnp.dot(a, b, preferred_element_type=F32)


def _dot_t(a, b):
    return lax.dot_general(a, b, (((1,), (1,)), ((), ())), preferred_element_type=F32)


def _tdot(a, b):
    return lax.dot_general(a, b, (((0,), (0,)), ((), ())), preferred_element_type=F32)


def _colsum(x):
    return jnp.sum(x, axis=0, keepdims=True)


def _slopes(n_heads):
    return [float(np.float32(2.0 ** (-ALIBI_MAX_EXP * (h + 1) / n_heads))) for h in range(n_heads)]


def _to_chunks(chunks_ref, x):
    for cc in range(chunks_ref.shape[0]):
        chunks_ref[cc] = x[:, cc * 128:(cc + 1) * 128]


def _deinterleave(chunks_ref, out_ref, d, dtype):
    rows = chunks_ref.shape[1] // d
    for r in range(d):
        for cc in range(chunks_ref.shape[0]):
            out_ref[r, :, cc * 128:(cc + 1) * 128] = chunks_ref[cc, pl.ds(r, rows, stride=d), :].astype(dtype)


def _interleave(in_ref, chunks_ref, d):
    rows = chunks_ref.shape[1] // d
    for r in range(d):
        for cc in range(chunks_ref.shape[0]):
            chunks_ref[cc, pl.ds(r, rows, stride=d), :] = in_ref[r, :, cc * 128:(cc + 1) * 128]


def _hbm_specs(n):
    return [pl.BlockSpec(memory_space=pl.ANY)] * n


def _position():
    x, y, c = lax.axis_index("x"), lax.axis_index("y"), lax.axis_index("c")
    return x, y, c


def _gather_stages(ins, outs, send_sems, recv_sems, local_sems):
    n = len(ins)

    def plan():
        x, y, c = _position()
        j = 2 * x + y
        me, sibling = (x, y, c), (x, y, 1 - c)
        chips = [(1 - x, y), (x, 1 - y), (1 - x, 1 - y)]

        def copy(i, k, src, dst, to):
            return pltpu.make_async_remote_copy(
                src_ref=src, dst_ref=dst, send_sem=send_sems.at[i, k], recv_sem=recv_sems.at[i, k],
                device_id=to, device_id_type=MESH)

        local = [pltpu.make_async_copy(ins[i], outs[i].at[j], local_sems.at[i]) for i in range(n)]
        first, landing, passed, passed_landing = [], [], [], []
        for i in range(n):
            for k, chip in enumerate(chips):
                pj = 2 * chip[0] + chip[1]
                first.append(copy(i, k, ins[i].at[c], outs[i].at[j, c], (*chip, c)))
                landing.append(copy(i, k, ins[i].at[c], outs[i].at[pj, c], me))
                passed.append(copy(i, 3 + k, outs[i].at[pj, c], outs[i].at[pj, c], sibling))
                passed_landing.append(copy(i, 3 + k, ins[i].at[c], outs[i].at[pj, 1 - c], me))
        return local, first, landing, passed, passed_landing

    def start():
        local, first, _, _, _ = plan()
        for cp in local + first:
            cp.start()

    def forward():
        _, _, landing, passed, _ = plan()
        for arrived, cp in zip(landing, passed):
            arrived.wait_recv()
            cp.start()

    def finish():
        local, first, _, passed, passed_landing = plan()
        for cp in passed_landing:
            cp.wait_recv()
        for cp in first + passed:
            cp.wait_send()
        for cp in local:
            cp.wait()

    return start, forward, finish


def _gather_scratch(n):
    return [pltpu.SemaphoreType.DMA((n, 6)), pltpu.SemaphoreType.DMA((n, 6)), pltpu.SemaphoreType.DMA((n,))]


def _all_gather_chips(shards):
    n = len(shards)

    def body(*refs):
        for stage in _gather_stages(refs[:n], refs[n:2 * n], *refs[2 * n:]):
            stage()

    return pl.pallas_call(
        body, name="all_gather_chips",
        out_shape=[SDS((N_CHIPS,) + s.shape, s.dtype) for s in shards],
        in_specs=_hbm_specs(n), out_specs=_hbm_specs(n), scratch_shapes=_gather_scratch(n),
    )(*shards)


class _PairExchange:
    slots = N_CHIPS

    @staticmethod
    def out_shape(g):
        return SDS((N_CHIPS,) + g.shape[2:], g.dtype)

    @staticmethod
    def copies(ins, outs, send_sems, recv_sems):
        x, y, c = _position()
        return [pltpu.make_async_remote_copy(
            src_ref=ins[i].at[j, 1 - c], dst_ref=outs[i].at[j],
            send_sem=send_sems.at[i, j], recv_sem=recv_sems.at[i, j],
            device_id=(x, y, 1 - c), device_id_type=MESH) for i in range(len(ins)) for j in range(N_CHIPS)]


class _ChipScatter:
    slots = 3

    @staticmethod
    def out_shape(p):
        return SDS((3,) + p.shape[1:], p.dtype)

    @staticmethod
    def copies(ins, outs, send_sems, recv_sems):
        x, y, c = _position()
        chips = [(1 - x, y), (x, 1 - y), (1 - x, 1 - y)]
        return [pltpu.make_async_remote_copy(
            src_ref=ins[i].at[2 * chip[0] + chip[1]], dst_ref=outs[i].at[k],
            send_sem=send_sems.at[i, k], recv_sem=recv_sems.at[i, k],
            device_id=(*chip, c), device_id_type=MESH) for i in range(len(ins)) for k, chip in enumerate(chips)]


def _exchange_start(kind, *refs):
    for cp in kind.copies(*refs):
        cp.start()


def _exchange_finish(kind, *refs):
    copies = kind.copies(*refs)
    for cp in copies:
        cp.wait_recv()
    for cp in copies:
        cp.wait_send()


def _exchange_scratch(kind, n):
    return [pltpu.SemaphoreType.DMA((n, kind.slots)), pltpu.SemaphoreType.DMA((n, kind.slots))]


def _exchange(kind, name, arrays):
    n = len(arrays)

    def body(*refs):
        args = (refs[:n], refs[n:2 * n], *refs[2 * n:])
        _exchange_start(kind, *args)
        _exchange_finish(kind, *args)

    return pl.pallas_call(
        body, name=name, out_shape=[kind.out_shape(a) for a in arrays],
        in_specs=_hbm_specs(n), out_specs=_hbm_specs(n), scratch_shapes=_exchange_scratch(kind, n),
    )(*arrays)


def _pair_share(halves, tag):
    n = len(halves)

    def body(*refs):
        ins, outs = refs[:n], refs[n:2 * n]
        send_sems, recv_sems = refs[2 * n:]
        x, y, c = _position()
        remote = [pltpu.make_async_remote_copy(
            src_ref=ins[i], dst_ref=outs[i], send_sem=send_sems.at[i], recv_sem=recv_sems.at[i],
            device_id=(x, y, 1 - c), device_id_type=MESH) for i in range(n)]
        for cp in remote:
            cp.start()
        for cp in remote:
            cp.wait_recv()
        for cp in remote:
            cp.wait_send()

    return pl.pallas_call(
        body, name="grad_pair_share_" + tag,
        out_shape=[SDS(h.shape, h.dtype) for h in halves],
        in_specs=_hbm_specs(n), out_specs=_hbm_specs(n),
        scratch_shapes=[pltpu.SemaphoreType.DMA((n,)), pltpu.SemaphoreType.DMA((n,))],
    )(*halves)


def _gather_all_devices(slab):
    def body(in_ref, out_ref, send_sems, recv_sems, local_sem):
        x, y, c = _position()
        me = 4 * x + 2 * y + c
        local = pltpu.make_async_copy(in_ref, out_ref.at[me], local_sem)
        local.start()
        remote, landing = [], []
        for mask in range(1, N_DEV):
            px, py, pc = x ^ (mask >> 2), y ^ ((mask >> 1) & 1), c ^ (mask & 1)
            peer = 4 * px + 2 * py + pc
            remote.append(pltpu.make_async_remote_copy(
                src_ref=in_ref, dst_ref=out_ref.at[me], send_sem=send_sems.at[mask - 1],
                recv_sem=recv_sems.at[mask - 1], device_id=(px, py, pc), device_id_type=MESH))
            landing.append(pltpu.make_async_remote_copy(
                src_ref=in_ref, dst_ref=out_ref.at[peer], send_sem=send_sems.at[mask - 1],
                recv_sem=recv_sems.at[mask - 1], device_id=(px, py, pc), device_id_type=MESH))
        for cp in remote:
            cp.start()
        for cp in landing:
            cp.wait_recv()
        for cp in remote:
            cp.wait_send()
        local.wait()

    return pl.pallas_call(
        body, name="small_grad_gather",
        out_shape=SDS((N_DEV,) + slab.shape, slab.dtype),
        in_specs=_hbm_specs(1), out_specs=pl.BlockSpec(memory_space=pl.ANY),
        scratch_shapes=[pltpu.SemaphoreType.DMA((N_DEV - 1,)), pltpu.SemaphoreType.DMA((N_DEV - 1,)),
                        pltpu.SemaphoreType.DMA],
    )(slab)


def _row_splits(arrays):
    return min(a.shape[-2] for a in arrays) // 16


def _pair_sum(grads, recvd, core, tag):
    n = len(grads)
    splits = _row_splits(recvd)

    def body(core_ref, *refs):
        for i in range(n):
            s = refs[i][...] + refs[n + i][...]
            refs[2 * n + i][...] = s
            refs[3 * n + i][...] = s.astype(BF16)

    mine = [pl.BlockSpec((N_CHIPS, None, r.shape[1] // splits, r.shape[2]), lambda s, core: (0, core[0], s, 0))
            for r in recvd]
    block = [pl.BlockSpec((N_CHIPS, r.shape[1] // splits, r.shape[2]), lambda s, core: (0, s, 0)) for r in recvd]
    outs = pl.pallas_call(
        body, name="grad_pair_sum_" + tag,
        grid_spec=pltpu.PrefetchScalarGridSpec(
            num_scalar_prefetch=1, grid=(splits,), in_specs=mine + block, out_specs=block + block),
        out_shape=[SDS(r.shape, F32) for r in recvd] + [SDS(r.shape, BF16) for r in recvd],
        compiler_params=_params(("parallel",)),
    )(core, *grads, *recvd)
    return outs[:n], outs[n:]


def _chip_sum(parts, landed, chip, tag):
    n = len(parts)
    splits = _row_splits(landed)

    def body(chip_ref, *refs):
        for i in range(n):
            acc = refs[i][...]
            for k in range(3):
                acc = acc + refs[n + i][k].astype(F32)
            refs[2 * n + i][...] = acc

    rows = lambda p: p.shape[1] // splits
    return pl.pallas_call(
        body, name="grad_chip_sum_" + tag,
        grid_spec=pltpu.PrefetchScalarGridSpec(
            num_scalar_prefetch=1, grid=(splits,),
            in_specs=[pl.BlockSpec((None, rows(p), p.shape[2]), lambda s, chip: (chip[0], s, 0)) for p in parts]
            + [pl.BlockSpec((3, rows(p), p.shape[2]), lambda s, chip: (0, s, 0)) for p in parts],
            out_specs=[pl.BlockSpec((rows(p), p.shape[2]), lambda s, chip: (s, 0)) for p in parts]),
        out_shape=[SDS(p.shape[1:], F32) for p in parts],
        compiler_params=_params(("parallel",)),
    )(chip, *parts, *landed)


def _adamw_math(w, g, m, v):
    m = ADAM_B1 * m + (1.0 - ADAM_B1) * g
    v = ADAM_B2 * v + (1.0 - ADAM_B2) * (g * g)
    m_hat = m / (1.0 - ADAM_B1 ** ADAM_STEP)
    v_hat = v / (1.0 - ADAM_B2 ** ADAM_STEP)
    delta = -ADAM_LR * (m_hat / (jnp.sqrt(v_hat) + ADAM_EPS) + ADAM_WD * w)
    return delta, m, v


def _adamw(name, ws, gs, ms, vs, splits):
    n = len(ws)

    def body(*refs):
        for i in range(n):
            w, g, m, v = (refs[q * n + i][...] for q in range(4))
            delta, m, v = _adamw_math(w, g, m, v)
            refs[4 * n + i][...] = delta
            refs[5 * n + i][...] = m
            refs[6 * n + i][...] = v

    def spec(a):
        if splits == 1:
            return pl.BlockSpec(a.shape, lambda s: (0, 0))
        return pl.BlockSpec((a.shape[0] // splits, a.shape[1]), lambda s: (s, 0))

    specs = [spec(a) for a in ws]
    outs = pl.pallas_call(
        body, name=name, grid=(splits,),
        in_specs=specs * 4, out_specs=specs * 3,
        out_shape=[SDS(a.shape, F32) for a in ws] * 3,
        compiler_params=_params(("parallel",)),
    )(*ws, *gs, *ms, *vs)
    return outs[:n], outs[n:2 * n], outs[2 * n:]


def _adamw_halves(ws, own, other, ms, vs, core):
    n = len(ws)
    splits = _row_splits(own)

    def body(core_ref, *refs):
        mine = pl.program_id(0) == core_ref[0]
        for i in range(n):
            g = jnp.where(mine, refs[n + i][...], refs[2 * n + i][...])
            delta, m, v = _adamw_math(refs[i][...], g, refs[3 * n + i][...], refs[4 * n + i][...])
            refs[5 * n + i][...] = g
            refs[6 * n + i][...] = delta
            refs[7 * n + i][...] = m
            refs[8 * n + i][...] = v

    rows = lambda a: a.shape[0] // splits
    half = [pl.BlockSpec((None, rows(a), a.shape[1]), lambda hh, s, core: (hh, s, 0)) for a in own]
    flat = [pl.BlockSpec((rows(a), a.shape[1]), lambda hh, s, core: (s, 0)) for a in own]
    outs = pl.pallas_call(
        body, name="adamw_big",
        grid_spec=pltpu.PrefetchScalarGridSpec(
            num_scalar_prefetch=1, grid=(2, splits), in_specs=half + flat + flat + half + half, out_specs=half * 4),
        out_shape=[SDS(w.shape, F32) for w in ws] * 4,
        compiler_params=_params(("parallel", "parallel")),
    )(core, *ws, *own, *other, *ms, *vs)
    return outs[:n], outs[n:2 * n], outs[2 * n:3 * n], outs[3 * n:]


def _sum_devices(slabs):
    def body(in_ref, out_ref):
        acc = in_ref[0]
        for k in range(1, N_DEV):
            acc = acc + in_ref[k]
        out_ref[...] = acc

    return pl.pallas_call(
        body, name="small_grad_sum", out_shape=SDS(slabs.shape[1:], F32),
    )(slabs)


def _mm_nn(name, a, w, out_dtype, tm, scale_first_tile=None):
    S, K = a.shape
    N = w.shape[1]
    tn = K

    def body(a_ref, w_ref, o_ref):
        acc = _dot(a_ref[...], w_ref[...])
        if scale_first_tile is not None:
            acc = acc * jnp.where(pl.program_id(1) == 0, scale_first_tile, 1.0)
        o_ref[...] = acc.astype(out_dtype)

    return pl.pallas_call(
        body, name=name, grid=(S // tm, N // tn),
        in_specs=[pl.BlockSpec((tm, K), lambda i, t: (i, 0)), pl.BlockSpec((K, tn), lambda i, t: (0, t))],
        out_specs=pl.BlockSpec((tm, tn), lambda i, t: (i, t)),
        out_shape=SDS((S, N), out_dtype),
        compiler_params=_params(("parallel", "arbitrary")),
    )(a, w)


def _mm_tn(name, a, b, tk):
    S, M = a.shape
    N = b.shape[1]
    tn = M

    def body(a_ref, b_ref, o_ref):
        @pl.when(pl.program_id(1) == 0)
        def _():
            o_ref[...] = jnp.zeros_like(o_ref)
        o_ref[...] += _tdot(a_ref[...], b_ref[...])

    return pl.pallas_call(
        body, name=name, grid=(N // tn, S // tk),
        in_specs=[pl.BlockSpec((tk, M), lambda t, k: (k, 0)), pl.BlockSpec((tk, tn), lambda t, k: (k, t))],
        out_specs=pl.BlockSpec((M, tn), lambda t, k: (0, t)),
        out_shape=SDS((M, N), F32),
        compiler_params=_params(("parallel", "arbitrary")),
    )(a, b)


def _a_in_proj(x, w4, b_in, later_shards, tm):
    S, D = x.shape
    nj = w4.shape[2]
    n = len(later_shards)
    steps = S // tm

    def body(x_ref, w_ref, b_ref, *refs):
        shard_refs, (h_ref, xb_ref), gathered_refs = refs[:n], refs[n:n + 2], refs[n + 2:2 * n + 2]
        start, forward, finish = _gather_stages(shard_refs, gathered_refs, *refs[2 * n + 2:])
        i, t = pl.program_id(0), pl.program_id(1)
        pl.when((i == 0) & (t == 0))(start)
        pl.when((i == steps // 2) & (t == 0))(forward)
        xb = x_ref[...].astype(BF16)

        @pl.when(t == 0)
        def _():
            xb_ref[...] = xb
        h_ref[...] = _dot(xb, w_ref[...]) + b_ref[...]
        pl.when((i == steps - 1) & (t == N_CHIPS - 1))(finish)

    outs = pl.pallas_call(
        body, name="a_in_proj", grid=(steps, N_CHIPS),
        in_specs=[pl.BlockSpec((tm, D), lambda i, t: (i, 0)),
                  pl.BlockSpec((None, D, nj), lambda i, t: (t, 0, 0)),
                  pl.BlockSpec((1, nj), lambda i, t: (0, t))] + _hbm_specs(n),
        out_specs=[pl.BlockSpec((tm, nj), lambda i, t: (i, t)), pl.BlockSpec((tm, D), lambda i, t: (i, 0))]
        + _hbm_specs(n),
        out_shape=[SDS((S, N_CHIPS * nj), F32), SDS((S, D), BF16)]
        + [SDS((N_CHIPS,) + s.shape, s.dtype) for s in later_shards],
        scratch_shapes=_gather_scratch(n),
        compiler_params=_params(("arbitrary", "arbitrary")),
    )(x, w4, b_in, *later_shards)
    return outs[0], outs[1], outs[2:]


def _fill_glu_ext(ext_ref, a_ref, g_ref, ah_ref, gh_ref, has_prev):
    u0h = ah_ref[...] * _sigmoid(gh_ref[...])
    ext_ref[0:HALO, :] = jnp.where(has_prev, u0h, 0.0)
    ext_ref[HALO:, :] = a_ref[...] * _sigmoid(g_ref[...])


def _fill_shifts(shift_ref, ext_ref):
    for s in range(1, 8):
        shift_ref[s - 1] = ext_ref[s:s + shift_ref.shape[1], :]


def _tap_windows(ext_ref, shift_ref, starts, r0, rows, lanes):
    for s in range(8):
        taps = [(k, st) for k, st in enumerate(starts) if st % 8 == s]
        if not taps:
            continue
        lo = min(st for _, st in taps)
        hi = max(st for _, st in taps)
        if s == 0:
            win = ext_ref[r0 + lo:r0 + hi + rows, lanes]
        else:
            win = shift_ref[s - 1, r0 + lo - s:r0 + hi - s + rows, lanes]
        for k, st in taps:
            yield k, win[st - lo:st - lo + rows]


def _a_conv_out(h, x, wdw, bdw, lng, lnb, wout, bout, pg, pb, tm):
    S, D = x.shape
    hb = tm // HALO
    d1, d2 = DILATIONS[1], DILATIONS[2]

    def body(a_ref, g_ref, z_ref, ah_ref, gh_ref, x_ref, wdw_ref, bdw_ref, lng_ref, lnb_ref, wout_ref,
             bout_ref, pg_ref, pb_ref, xhu_ref, rsu_ref, vb_ref, xh1_ref, rs1_ref, x1b_ref, x1p1_ref,
             x1p2_ref, ext_ref, u1_ref, x1_ref, sh_ref):
        i = pl.program_id(0)
        _fill_glu_ext(ext_ref, a_ref, g_ref, ah_ref, gh_ref, i > 0)
        _fill_shifts(sh_ref, ext_ref)
        starts = [HALO - (CONV_WIDTH - 1) + k for k in range(CONV_WIDTH)]
        for cc in range(D // 128):
            lanes = slice(cc * 128, (cc + 1) * 128)
            for r0 in range(0, tm, CONV_ROWS):
                acc = jnp.broadcast_to(bdw_ref[:, lanes], (CONV_ROWS, 128))
                for k, win in _tap_windows(ext_ref, sh_ref, starts, r0, CONV_ROWS, lanes):
                    acc = acc + wdw_ref[k:k + 1, lanes] * win
                u1_ref[r0:r0 + CONV_ROWS, lanes] = acc
        xhu, rsu = _ln_fwd(u1_ref[...])
        xhu_ref[...] = xhu
        rsu_ref[...] = rsu
        u2 = xhu * lng_ref[...] + lnb_ref[...]
        z = z_ref[...]
        v = (u2 * _sigmoid(u2)) * (z * _sigmoid(z))
        vb = v.astype(BF16)
        vb_ref[...] = vb
        s1 = ALPHA * x_ref[...] + _dot(vb, wout_ref[...]) + bout_ref[...]
        xh1, rs1 = _ln_fwd(s1)
        xh1_ref[...] = xh1
        rs1_ref[...] = rs1
        x1 = xh1 * pg_ref[...] + pb_ref[...]
        x1b_ref[...] = x1.astype(BF16)
        _to_chunks(x1_ref, x1)
        _deinterleave(x1_ref, x1p1_ref, d1, BF16)
        _deinterleave(x1_ref, x1p2_ref, d2, BF16)

    tile = lambda c: pl.BlockSpec((tm, D), lambda i, c=c: (i, c))
    halo = lambda c: pl.BlockSpec((HALO, D), lambda i, c=c: (jnp.maximum(i * hb - 1, 0), c))
    row = pl.BlockSpec((1, D), lambda i: (0, 0))
    stat = pl.BlockSpec((tm, 1), lambda i: (i, 0))
    return pl.pallas_call(
        body, name="a_conv_out", grid=(S // tm,),
        in_specs=[tile(0), tile(1), tile(2), halo(0), halo(1), tile(0),
                  pl.BlockSpec((HALO, D), lambda i: (0, 0)), row, row, row,
                  pl.BlockSpec((D, D), lambda i: (0, 0)), row, row, row],
        out_specs=[tile(0), stat, tile(0), tile(0), stat, tile(0),
                   pl.BlockSpec((d1, tm // d1, D), lambda i: (0, i, 0)),
                   pl.BlockSpec((d2, tm // d2, D), lambda i: (0, i, 0))],
        out_shape=[SDS((S, D), F32), SDS((S, 1), F32), SDS((S, D), BF16), SDS((S, D), F32), SDS((S, 1), F32),
                   SDS((S, D), BF16), SDS((d1, S // d1, D), BF16), SDS((d2, S // d2, D), BF16)],
        scratch_shapes=[pltpu.VMEM((HALO + tm, D), F32), pltpu.VMEM((tm, D), F32),
                        pltpu.VMEM((D // 128, tm, 128), F32), pltpu.VMEM((7, HALO + tm - 8, D), F32)],
        compiler_params=_params(("parallel",)),
    )(h, h, h, h, h, x, wdw, bdw, lng, lnb, wout, bout, pg, pb)


def _band(n, dilation):
    qi = lax.broadcasted_iota(jnp.int32, (BLOCK, 2 * BLOCK), 0)
    kj = lax.broadcasted_iota(jnp.int32, (BLOCK, 2 * BLOCK), 1)
    dist = qi + BLOCK - kj
    valid = (dist >= 0) & (dist <= BLOCK) & ((n > 0) | (kj >= BLOCK))
    return jnp.where(valid, dist.astype(F32) * float(-dilation), -jnp.inf)


def _attn_fwd(g, qkv, D):
    S = qkv.shape[0]
    d = DILATIONS[g]
    nb = S // (d * BLOCK)
    H = D // HEAD_DIM
    slopes = _slopes(H)

    def body(q_ref, kp_ref, kc_ref, vp_ref, vc_ref, o_ref, lse_ref):
        neg_dist = _band(pl.program_id(1), d)
        lane = lax.broadcasted_iota(jnp.int32, (BLOCK, LSE_LANES), 1)
        low = lane < HEAD_DIM
        lse = jnp.zeros((BLOCK, LSE_LANES), F32)
        for hp in range(H // 2):
            sl = slice(hp * 128, (hp + 1) * 128)
            q = q_ref[:, sl]
            k = jnp.concatenate([kp_ref[:, sl], kc_ref[:, sl]], axis=0)
            v = jnp.concatenate([vp_ref[:, sl], vc_ref[:, sl]], axis=0)
            o = []
            for a in range(2):
                h = 2 * hp + a
                s = _dot_t(jnp.where(low if a == 0 else ~low, q, jnp.zeros_like(q)), k)
                s = s + slopes[h] * neg_dist
                m = jnp.max(s, axis=1, keepdims=True)
                p = jnp.exp(s - m)
                l = jnp.sum(p, axis=1, keepdims=True)
                o.append(_dot(p.astype(BF16), v) * (1.0 / l))
                lse = jnp.where(lane == h, m + jnp.log(l), lse)
            o_ref[:, sl] = jnp.where(low, o[0], o[1])
        lse_ref[...] = lse

    cur = lambda c: pl.BlockSpec((BLOCK, D), lambda r, n, c=c: (r * nb + n, c))
    prev = lambda c: pl.BlockSpec((BLOCK, D), lambda r, n, c=c: (r * nb + jnp.maximum(n - 1, 0), c))
    return pl.pallas_call(
        body, name="attn_fwd_g%d" % g, grid=(d, nb),
        in_specs=[cur(0), prev(1), cur(1), prev(2), cur(2)],
        out_specs=[cur(0), pl.BlockSpec((BLOCK, LSE_LANES), lambda r, n: (r * nb + n, 0))],
        out_shape=[SDS((S, D), F32), SDS((S, LSE_LANES), F32)],
        compiler_params=_params(("parallel", "parallel")),
    )(qkv, qkv, qkv, qkv, qkv)


def _attn_bwd(g, qkv, do, lse, delta, D):
    S = qkv.shape[0]
    d = DILATIONS[g]
    nb = S // (d * BLOCK)
    H = D // HEAD_DIM
    slopes = _slopes(H)

    def body(q_ref, kp_ref, kc_ref, vp_ref, vc_ref, do_ref, lse_ref, dl_ref, dq_ref, dkv_ref, ck_ref, cv_ref):
        n = pl.program_id(1)

        @pl.when(n == 0)
        def _():
            ck_ref[...] = jnp.zeros_like(ck_ref)
            cv_ref[...] = jnp.zeros_like(cv_ref)

        @pl.when(n < nb)
        def _():
            neg_dist = _band(n, d)
            low = lax.broadcasted_iota(jnp.int32, (BLOCK, 128), 1) < HEAD_DIM
            low2 = lax.broadcasted_iota(jnp.int32, (2 * BLOCK, 128), 1) < HEAD_DIM
            for hp in range(H // 2):
                sl = slice(hp * 128, (hp + 1) * 128)
                q = q_ref[:, sl]
                do2 = do_ref[:, sl]
                k = jnp.concatenate([kp_ref[:, sl], kc_ref[:, sl]], axis=0)
                v = jnp.concatenate([vp_ref[:, sl], vc_ref[:, sl]], axis=0)
                zero = jnp.zeros_like(q)
                q_do = jnp.concatenate([jnp.concatenate([q, zero], axis=1),
                                        jnp.concatenate([zero, do2], axis=1)], axis=0)
                dq, dkv = [], []
                for a in range(2):
                    h = 2 * hp + a
                    keep = low if a == 0 else ~low
                    s = _dot_t(jnp.where(keep, q, zero), k)
                    s = s + slopes[h] * neg_dist
                    p = jnp.exp(s - lse_ref[:, h:h + 1])
                    dp = _dot_t(jnp.where(keep, do2, zero), v)
                    dsb = (p * (dp - dl_ref[:, h:h + 1])).astype(BF16)
                    dq.append(_dot(dsb, k))
                    dkv.append(_tdot(jnp.concatenate([dsb, p.astype(BF16)], axis=0), q_do))
                dq_ref[:, sl] = (jnp.where(low, dq[0], dq[1]) * (HEAD_DIM ** -0.5)).astype(BF16)
                dk2 = jnp.where(low2, dkv[0][:, :128], dkv[1][:, :128])
                dv2 = jnp.where(low2, dkv[0][:, 128:], dkv[1][:, 128:])
                dkv_ref[:, sl] = (ck_ref[:, sl] + dk2[:BLOCK]).astype(BF16)
                dkv_ref[:, D + hp * 128:D + (hp + 1) * 128] = (cv_ref[:, sl] + dv2[:BLOCK]).astype(BF16)
                ck_ref[:, sl] = dk2[BLOCK:]
                cv_ref[:, sl] = dv2[BLOCK:]

        @pl.when(n == nb)
        def _():
            dkv_ref[:, :D] = ck_ref[...].astype(BF16)
            dkv_ref[:, D:] = cv_ref[...].astype(BF16)

    nq = lambda n: jnp.minimum(n, nb - 1)
    cur = lambda c: pl.BlockSpec((BLOCK, D), lambda r, n, c=c: (r * nb + nq(n), c))
    prev = lambda c: pl.BlockSpec((BLOCK, D), lambda r, n, c=c: (r * nb + jnp.maximum(nq(n) - 1, 0), c))
    stat = pl.BlockSpec((BLOCK, LSE_LANES), lambda r, n: (r * nb + nq(n), 0))
    return pl.pallas_call(
        body, name="attn_bwd_g%d" % g, grid=(d, nb + 1),
        in_specs=[cur(0), prev(1), cur(1), prev(2), cur(2), cur(0), stat, stat],
        out_specs=[cur(0), pl.BlockSpec((BLOCK, 2 * D), lambda r, n: (r * nb + jnp.maximum(n - 1, 0), 0))],
        out_shape=[SDS((S, D), BF16), SDS((S, 2 * D), BF16)],
        scratch_shapes=[pltpu.VMEM((BLOCK, D), F32), pltpu.VMEM((BLOCK, D), F32)],
        compiler_params=_params(("parallel", "arbitrary")),
    )(qkv, qkv, qkv, qkv, qkv, do, lse, delta)


def _b_merge_out_loss(o0, o1, o2, l0, l1, l2, z2, xh1, target, wbo, bbo, pg0, pb0, pg1, pb1, tm):
    S, D = o0.shape
    H = D // HEAD_DIM
    d1, d2 = DILATIONS[1], DILATIONS[2]
    inv_d = 1.0 / D

    def body(o0_ref, o1_ref, o2_ref, l0_ref, l1_ref, l2_ref, z_ref, xh1_ref, t_ref, wbo_ref, bbo_ref,
             pg0_ref, pb0_ref, pg1_ref, pb1_ref,
             v2b_ref, ds2_ref, ds2b_ref, dz2b_ref, da0_ref, da1_ref, da2_ref, ls0_ref, ls1_ref, ls2_ref,
             dl0_ref, dl1_ref, dl2_ref, loss_ref, sums_ref,
             o1n_ref, o2n_ref, l1n_ref, l2n_ref, att_ref):
        i = pl.program_id(0)
        _interleave(o1_ref, o1n_ref, d1)
        _interleave(o2_ref, o2n_ref, d2)
        for r in range(d1):
            l1n_ref[pl.ds(r, tm // d1, stride=d1), :] = l1_ref[r]
        for r in range(d2):
            l2n_ref[pl.ds(r, tm // d2, stride=d2), :] = l2_ref[r]
        la, lb, lc = l0_ref[...], l1n_ref[...], l2n_ref[...]
        m = jnp.maximum(jnp.maximum(la, lb), lc)
        ea, eb, ec = jnp.exp(la - m), jnp.exp(lb - m), jnp.exp(lc - m)
        den = ea + eb + ec
        wa, wb, wc = ea / den, eb / den, ec / den
        ls0_ref[...] = m + jnp.log(den)
        for h in range(H):
            sl = slice(h * HEAD_DIM, (h + 1) * HEAD_DIM)
            cc, hl = divmod(h * HEAD_DIM, 128)
            att_ref[:, sl] = (wa[:, h:h + 1] * o0_ref[:, sl] + wb[:, h:h + 1] * o1n_ref[cc, :, hl:hl + HEAD_DIM]
                              + wc[:, h:h + 1] * o2n_ref[cc, :, hl:hl + HEAD_DIM])
        att = att_ref[...]
        z = z_ref[...]
        sz = _sigmoid(z)
        gate = z * sz
        v2b = (att * gate).astype(BF16)
        v2b_ref[...] = v2b
        x1 = xh1_ref[...] * pg0_ref[...] + pb0_ref[...]
        s2 = ALPHA * x1 + _dot(v2b, wbo_ref[...]) + bbo_ref[...]
        xh2, rs2 = _ln_fwd(s2)
        err = xh2 * pg1_ref[...] + pb1_ref[...] - t_ref[...]
        dy = err * inv_d
        ds2 = _ln_bwd(dy * pg1_ref[...], xh2, rs2)
        ds2b = ds2.astype(BF16)
        ds2_ref[...] = ds2
        ds2b_ref[...] = ds2b

        @pl.when(i == 0)
        def _():
            loss_ref[...] = jnp.zeros_like(loss_ref)
            sums_ref[...] = jnp.zeros_like(sums_ref)
        loss_ref[...] += 0.5 * inv_d * jnp.sum(err * err)
        sums_ref[0:1, :] += _colsum(dy * xh2)
        sums_ref[1:2, :] += _colsum(dy)
        sums_ref[2:3, :] += _colsum(ds2)

        dv2 = _dot_t(ds2b, wbo_ref[...])
        datt = dv2 * gate
        dz2b_ref[...] = (dv2 * att * _silu_grad(z, sz)).astype(BF16)
        prod = datt * att
        lane = lax.broadcasted_iota(jnp.int32, (tm, LSE_LANES), 1)
        dl = jnp.zeros((tm, LSE_LANES), F32)
        for h in range(H):
            sl = slice(h * HEAD_DIM, (h + 1) * HEAD_DIM)
            dl = jnp.where(lane == h, jnp.sum(prod[:, sl], axis=1, keepdims=True), dl)
        da0_ref[...] = datt.astype(BF16)
        dl0_ref[...] = dl
        _to_chunks(o1n_ref, datt)
        _deinterleave(o1n_ref, da1_ref, d1, BF16)
        _deinterleave(o1n_ref, da2_ref, d2, BF16)
        for r in range(d1):
            ls1_ref[r] = ls0_ref[pl.ds(r, tm // d1, stride=d1), :]
            dl1_ref[r] = dl0_ref[pl.ds(r, tm // d1, stride=d1), :]
        for r in range(d2):
            ls2_ref[r] = ls0_ref[pl.ds(r, tm // d2, stride=d2), :]
            dl2_ref[r] = dl0_ref[pl.ds(r, tm // d2, stride=d2), :]

    tile = pl.BlockSpec((tm, D), lambda i: (i, 0))
    stat = pl.BlockSpec((tm, LSE_LANES), lambda i: (i, 0))
    perm = lambda d, w: pl.BlockSpec((d, tm // d, w), lambda i: (0, i, 0))
    row = pl.BlockSpec((1, D), lambda i: (0, 0))
    acc = lambda w: pl.BlockSpec((8, w), lambda i: (0, 0))
    pshape = lambda d, w, dt: SDS((d, S // d, w), dt)
    return pl.pallas_call(
        body, name="b_merge_out_loss", grid=(S // tm,),
        in_specs=[tile, perm(d1, D), perm(d2, D), stat, perm(d1, LSE_LANES), perm(d2, LSE_LANES),
                  tile, tile, tile, pl.BlockSpec((D, D), lambda i: (0, 0)), row, row, row, row, row],
        out_specs=[tile, tile, tile, tile, tile, perm(d1, D), perm(d2, D),
                   stat, perm(d1, LSE_LANES), perm(d2, LSE_LANES),
                   stat, perm(d1, LSE_LANES), perm(d2, LSE_LANES), acc(LSE_LANES), acc(D)],
        out_shape=[SDS((S, D), BF16), SDS((S, D), F32), SDS((S, D), BF16), SDS((S, D), BF16),
                   SDS((S, D), BF16), pshape(d1, D, BF16), pshape(d2, D, BF16),
                   SDS((S, LSE_LANES), F32), pshape(d1, LSE_LANES, F32), pshape(d2, LSE_LANES, F32),
                   SDS((S, LSE_LANES), F32), pshape(d1, LSE_LANES, F32), pshape(d2, LSE_LANES, F32),
                   SDS((8, LSE_LANES), F32), SDS((8, D), F32)],
        scratch_shapes=[pltpu.VMEM((D // 128, tm, 128), F32), pltpu.VMEM((D // 128, tm, 128), F32),
                        pltpu.VMEM((tm, LSE_LANES), F32), pltpu.VMEM((tm, LSE_LANES), F32),
                        pltpu.VMEM((tm, D), F32)],
        compiler_params=_params(("arbitrary",)),
    )(o0, o1, o2, l0, l1, l2, z2, xh1, target, wbo, bbo, pg0, pb0, pg1, pb1)


def _b_dx1_ln1_bwd(ds2, dz2b, dq, dkv, xh1, rs1, wz, wg, pg0, tm):
    S, D = ds2.shape
    d1, d2 = DILATIONS[1], DILATIONS[2]

    def group_part(dq_blk, dkv_blk, w_ref):
        return (_dot_t(dq_blk, w_ref[:, 0:D]) + _dot_t(dkv_blk[:, 0:D], w_ref[:, D:2 * D])
                + _dot_t(dkv_blk[:, D:2 * D], w_ref[:, 2 * D:3 * D]))

    def body(ds2_ref, dz_ref, dq0_ref, dkv0_ref, dq1_ref, dkv1_ref, dq2_ref, dkv2_ref, xh1_ref, rs1_ref,
             wz_ref, w0_ref, w1_ref, w2_ref, pg0_ref, ds1_ref, ds1b_ref, sums_ref, acc_ref):
        i = pl.program_id(0)
        _to_chunks(acc_ref, ALPHA * ds2_ref[...] + _dot_t(dz_ref[...], wz_ref[...])
                   + group_part(dq0_ref[...], dkv0_ref[...], w0_ref))
        for d, dq_ref, dkv_ref, w_ref in ((d1, dq1_ref, dkv1_ref, w1_ref), (d2, dq2_ref, dkv2_ref, w2_ref)):
            rows = tm // d
            part = group_part(dq_ref[...].reshape(tm, D), dkv_ref[...].reshape(tm, 2 * D), w_ref)
            for r in range(d):
                idx = pl.ds(r, rows, stride=d)
                for cc in range(D // 128):
                    acc_ref[cc, idx, :] = acc_ref[cc, idx, :] + part[r * rows:(r + 1) * rows, cc * 128:(cc + 1) * 128]
        dx1 = jnp.concatenate([acc_ref[cc] for cc in range(D // 128)], axis=1)
        xh1 = xh1_ref[...]
        ds1 = _ln_bwd(dx1 * pg0_ref[...], xh1, rs1_ref[...])
        ds1_ref[...] = ds1
        ds1b_ref[...] = ds1.astype(BF16)

        @pl.when(i == 0)
        def _():
            sums_ref[...] = jnp.zeros_like(sums_ref)
        sums_ref[0:1, :] += _colsum(dx1 * xh1)
        sums_ref[1:2, :] += _colsum(dx1)
        sums_ref[2:3, :] += _colsum(ds1)

    tile = lambda w: pl.BlockSpec((tm, w), lambda i: (i, 0))
    perm = lambda d, w: pl.BlockSpec((d, tm // d, w), lambda i: (0, i, 0))
    whole = pl.BlockSpec(memory_space=pltpu.VMEM)
    return pl.pallas_call(
        body, name="b_dx1_ln1_bwd", grid=(S // tm,),
        in_specs=[tile(D), tile(D), tile(D), tile(2 * D), perm(d1, D), perm(d1, 2 * D), perm(d2, D),
                  perm(d2, 2 * D), tile(D), tile(1), whole, whole, whole, whole,
                  pl.BlockSpec((1, D), lambda i: (0, 0))],
        out_specs=[tile(D), tile(D), pl.BlockSpec((8, D), lambda i: (0, 0))],
        out_shape=[SDS((S, D), F32), SDS((S, D), BF16), SDS((8, D), F32)],
        scratch_shapes=[pltpu.VMEM((D // 128, tm, 128), F32)],
        compiler_params=_params(("arbitrary",)),
    )(ds2, dz2b, dq[0], dkv[0], dq[1].reshape(d1, S // d1, D), dkv[1].reshape(d1, S // d1, 2 * D),
      dq[2].reshape(d2, S // d2, D), dkv[2].reshape(d2, S // d2, 2 * D), xh1, rs1, wz, wg[0], wg[1], wg[2], pg0)


def _a_gate_bwd(ds1b, h, xhu, rsu, wout, lng, lnb, grads, tm):
    S, D = xhu.shape
    n = len(grads)
    steps = S // tm

    def body(ds_ref, z_ref, xhu_ref, rsu_ref, w_ref, lng_ref, lnb_ref, *refs):
        grad_refs, (du1_ref, dzb_ref, sums_ref), recvd_refs = refs[:n], refs[n:n + 3], refs[n + 3:2 * n + 3]
        exchange = (grad_refs, recvd_refs, *refs[2 * n + 3:])
        i = pl.program_id(0)
        pl.when(i == 0)(functools.partial(_exchange_start, _PairExchange, *exchange))
        dv = _dot_t(ds_ref[...], w_ref[...])
        xhu = xhu_ref[...]
        u2 = xhu * lng_ref[...] + lnb_ref[...]
        su = _sigmoid(u2)
        z = z_ref[...]
        sz = _sigmoid(z)
        dz = dv * (u2 * su) * _silu_grad(z, sz)
        du2 = dv * (z * sz) * _silu_grad(u2, su)
        du1 = _ln_bwd(du2 * lng_ref[...], xhu, rsu_ref[...])
        du1_ref[...] = du1
        dzb_ref[...] = dz.astype(BF16)

        @pl.when(i == 0)
        def _():
            sums_ref[...] = jnp.zeros_like(sums_ref)
        sums_ref[0:1, :] += _colsum(du2 * xhu)
        sums_ref[1:2, :] += _colsum(du2)
        sums_ref[2:3, :] += _colsum(du1)
        sums_ref[3:4, :] += _colsum(dz)
        pl.when(i == steps - 1)(functools.partial(_exchange_finish, _PairExchange, *exchange))

    tile = pl.BlockSpec((tm, D), lambda i: (i, 0))
    row = pl.BlockSpec((1, D), lambda i: (0, 0))
    outs = pl.pallas_call(
        body, name="a_gate_bwd", grid=(steps,),
        in_specs=[tile, pl.BlockSpec((tm, D), lambda i: (i, 2)), tile, pl.BlockSpec((tm, 1), lambda i: (i, 0)),
                  pl.BlockSpec((D, D), lambda i: (0, 0)), row, row] + _hbm_specs(n),
        out_specs=[tile, tile, pl.BlockSpec((8, D), lambda i: (0, 0))] + _hbm_specs(n),
        out_shape=[SDS((S, D), F32), SDS((S, D), BF16), SDS((8, D), F32)]
        + [_PairExchange.out_shape(g) for g in grads],
        scratch_shapes=_exchange_scratch(_PairExchange, n),
        compiler_params=_params(("arbitrary",)),
    )(ds1b, h, xhu, rsu, wout, lng, lnb, *grads)
    return outs[0], outs[1], outs[2], outs[3:]


def _a_conv_bwd(du1, h, wdw, parts, tm):
    S, D = du1.shape
    hb = tm // HALO
    last_halo = S // HALO - 1
    n_tiles = S // tm
    n = len(parts)

    def body(du_ref, dun_ref, a_ref, g_ref, ah_ref, gh_ref, wdw_ref, *refs):
        part_refs, (dag_ref, sums_ref, wsum_ref), landed_refs = refs[:n], refs[n:n + 3], refs[n + 3:2 * n + 3]
        dext_ref, ext_ref, dsh_ref, sh_ref, wacc_ref, send_sems, recv_sems = refs[2 * n + 3:]
        scatter = (part_refs, landed_refs, send_sems, recv_sems)
        i = pl.program_id(0)

        @pl.when(i == 0)
        def _():
            _exchange_start(_ChipScatter, *scatter)
            sums_ref[...] = jnp.zeros_like(sums_ref)
            wsum_ref[...] = jnp.zeros_like(wsum_ref)
            wacc_ref[...] = jnp.zeros_like(wacc_ref)
        dext_ref[0:tm, :] = du_ref[...]
        dext_ref[tm:, :] = jnp.where(i < n_tiles - 1, dun_ref[...], 0.0)
        _fill_shifts(dsh_ref, dext_ref)
        _fill_glu_ext(ext_ref, a_ref, g_ref, ah_ref, gh_ref, i > 0)
        _fill_shifts(sh_ref, ext_ref)
        back = [CONV_WIDTH - 1 - k for k in range(CONV_WIDTH)]
        fwd = [HALO - (CONV_WIDTH - 1) + k for k in range(CONV_WIDTH)]
        for cc in range(D // 128):
            lanes = slice(cc * 128, (cc + 1) * 128)
            hi_lanes = slice(D + cc * 128, D + (cc + 1) * 128)
            sa = jnp.zeros((1, 128), F32)
            sg = jnp.zeros((1, 128), F32)
            for r0 in range(0, tm, CONV_ROWS):
                acc = jnp.zeros((CONV_ROWS, 128), F32)
                for k, win in _tap_windows(dext_ref, dsh_ref, back, r0, CONV_ROWS, lanes):
                    acc = acc + wdw_ref[k:k + 1, lanes] * win
                a = a_ref[r0:r0 + CONV_ROWS, lanes]
                s = _sigmoid(g_ref[r0:r0 + CONV_ROWS, lanes])
                da = acc * s
                dg = acc * a * s * (1.0 - s)
                dag_ref[r0:r0 + CONV_ROWS, lanes] = da.astype(BF16)
                dag_ref[r0:r0 + CONV_ROWS, hi_lanes] = dg.astype(BF16)
                sa = sa + _colsum(da)
                sg = sg + _colsum(dg)
                du = du_ref[r0:r0 + CONV_ROWS, lanes]
                for k, win in _tap_windows(ext_ref, sh_ref, fwd, r0, CONV_ROWS, lanes):
                    p = du * win
                    fold = p[0:8]
                    for q in range(8, CONV_ROWS, 8):
                        fold = fold + p[q:q + 8]
                    wacc_ref[k, :, lanes] += fold
            sums_ref[0:1, lanes] += sa
            sums_ref[1:2, lanes] += sg

        @pl.when(i == n_tiles - 1)
        def _():
            for k in range(CONV_WIDTH):
                wsum_ref[k:k + 1, :] = _colsum(wacc_ref[k])
            _exchange_finish(_ChipScatter, *scatter)

    tile = lambda c: pl.BlockSpec((tm, D), lambda i, c=c: (i, c))
    halo = lambda c: pl.BlockSpec((HALO, D), lambda i, c=c: (jnp.maximum(i * hb - 1, 0), c))
    outs = pl.pallas_call(
        body, name="a_conv_bwd", grid=(n_tiles,),
        in_specs=[tile(0), pl.BlockSpec((HALO, D), lambda i: (jnp.minimum((i + 1) * hb, last_halo), 0)),
                  tile(0), tile(1), halo(0), halo(1), pl.BlockSpec((HALO, D), lambda i: (0, 0))] + _hbm_specs(n),
        out_specs=[pl.BlockSpec((tm, 2 * D), lambda i: (i, 0)), pl.BlockSpec((8, D), lambda i: (0, 0)),
                   pl.BlockSpec((HALO, D), lambda i: (0, 0))] + _hbm_specs(n),
        out_shape=[SDS((S, 2 * D), BF16), SDS((8, D), F32), SDS((HALO, D), F32)]
        + [SDS((3,) + p.shape[1:], p.dtype) for p in parts],
        scratch_shapes=[pltpu.VMEM((tm + HALO, D), F32), pltpu.VMEM((HALO + tm, D), F32),
                        pltpu.VMEM((7, HALO + tm - 8, D), F32), pltpu.VMEM((7, HALO + tm - 8, D), F32),
                        pltpu.VMEM((HALO, 8, D), F32)] + _exchange_scratch(_ChipScatter, n),
        compiler_params=_params(("arbitrary",)),
    )(du1, du1, h, h, h, h, wdw, *parts)
    return outs[0], outs[1], outs[2], outs[3:]


def _a_dx(ds1, dag, dzb, w_in, parts, tm):
    S, D = ds1.shape
    n = len(parts)
    steps = S // tm

    def body(ds_ref, dag_ref, dz_ref, w_ref, *refs):
        part_refs, o_ref, landed_refs = refs[:n], refs[n], refs[n + 1:2 * n + 1]
        scatter = (part_refs, landed_refs, *refs[2 * n + 1:])
        i = pl.program_id(0)
        pl.when(i == 0)(functools.partial(_exchange_start, _ChipScatter, *scatter))
        o_ref[...] = (ALPHA * ds_ref[...] + _dot_t(dag_ref[...], w_ref[:, 0:2 * D])
                      + _dot_t(dz_ref[...], w_ref[:, 2 * D:3 * D]))
        pl.when(i == steps - 1)(functools.partial(_exchange_finish, _ChipScatter, *scatter))

    tile = lambda w: pl.BlockSpec((tm, w), lambda i: (i, 0))
    outs = pl.pallas_call(
        body, name="a_dx", grid=(steps,),
        in_specs=[tile(D), tile(2 * D), tile(D), pl.BlockSpec(memory_space=pltpu.VMEM)] + _hbm_specs(n),
        out_specs=[tile(D)] + _hbm_specs(n),
        out_shape=[SDS((S, D), F32)] + [_ChipScatter.out_shape(p) for p in parts],
        scratch_shapes=_exchange_scratch(_ChipScatter, n),
        compiler_params=_params(("arbitrary",)),
    )(ds1, dag, dzb, w_in, *parts)
    return outs[0], outs[1:]


def _halves(w):
    return w.reshape(2, w.shape[0] // 2, w.shape[1])


def _unstack_cols(w4):
    return jnp.transpose(w4, (1, 0, 2)).reshape(w4.shape[1], N_CHIPS * w4.shape[2])


def _stack_cols(w):
    D, n = w.shape
    return jnp.transpose(w.reshape(D, N_CHIPS, n // N_CHIPS), (1, 0, 2))


def _pack_rows(rows, width, total=SMALL_ROWS):
    slab = jnp.concatenate([r.reshape(-1, width) for r in rows], axis=0)
    return jnp.pad(slab, ((0, total - slab.shape[0]), (0, 0)))


def _by_row(gathered, rows, dq4):
    return jnp.transpose(gathered.reshape(N_CHIPS, rows, dq4), (1, 0, 2))


def kernel(x, a_w_in, a_b_in, a_w_dw, a_b_dw, a_ln_g, a_ln_b, a_w_out, a_b_out, kv_w, b_w_in, b_w_out, b_b_out, post_ln_g, post_ln_b, loss_target, m_a_w_in, m_a_b_in, m_a_w_dw, m_a_b_dw, m_a_ln_g, m_a_ln_b, m_a_w_out, m_a_b_out, m_kv_w, m_b_w_in, m_b_w_out, m_b_b_out, m_post_ln_g, m_post_ln_b, v_a_w_in, v_a_b_in, v_a_w_dw, v_a_b_dw, v_a_ln_g, v_a_ln_b, v_a_w_out, v_a_b_out, v_kv_w, v_b_w_in, v_b_w_out, v_b_b_out, v_post_ln_g, v_post_ln_b):
    S, D = x.shape[1], x.shape[2]
    dq4 = D // N_CHIPS
    tm = 256
    tm_mm = min(S, 1024)
    x2 = x.reshape(S, D)
    target = loss_target.reshape(S, D)
    jchip = 2 * lax.axis_index("x") + lax.axis_index("y")

    big_local = [a_w_in[0], kv_w, b_w_in[0], a_w_out[0], b_w_out[0]]
    wire = [_halves(w.astype(BF16)) for w in big_local]
    whole = lambda g: g.reshape((N_CHIPS, 2 * g.shape[2], g.shape[3]))
    first_rows = 16
    small_first = _pack_rows([a_b_in.reshape(3, dq4)], dq4, first_rows)
    small_rest = _pack_rows([jnp.pad(a_w_dw[0], ((0, 1), (0, 0))), a_b_dw, a_ln_g, a_ln_b, a_b_out], dq4)
    gathered = _all_gather_chips([wire[0], _halves(small_first)])
    w_in4 = whole(gathered[0])
    w_in_a = _unstack_cols(w_in4)
    b_in_full = jnp.transpose(_by_row(gathered[1], first_rows, dq4)[0:3], (1, 0, 2)).reshape(1, 3 * D)
    pg0, pg1 = post_ln_g[0:1], post_ln_g[1:2]
    pb0, pb1 = post_ln_b[0:1], post_ln_b[1:2]

    h, xb, later = _a_in_proj(x2, w_in4, b_in_full, [wire[1], wire[2], wire[4], wire[3], _halves(small_rest)], tm_mm)
    kv_full = _unstack_cols(whole(later[0]))
    b_in4 = whole(later[1])
    w_out_b = later[2].reshape(D, D)
    w_out_a = later[3].reshape(D, D)
    small = _by_row(later[4], SMALL_ROWS, dq4)
    wdw_full = small[0:HALO].reshape(HALO, D)
    bdw_full, lng_full, lnb_full, bout_a_full = [small[HALO + q].reshape(1, D) for q in range(4)]
    w_z = b_in4[3]
    w_g = [jnp.concatenate([b_in4[g], kv_full[:, g * D:(g + 1) * D], kv_full[:, (3 + g) * D:(4 + g) * D]], axis=1)
           for g in range(3)]
    xhu, rsu, vb, xh1, rs1, x1b, x1p1, x1p2 = _a_conv_out(
        h, x2, wdw_full, bdw_full, lng_full, lnb_full, w_out_a, bout_a_full, pg0, pb0, tm)
    x1g = [x1b, x1p1.reshape(S, D), x1p2.reshape(S, D)]
    qkv = [_mm_nn("b_qkv_g%d" % g, x1g[g], w_g[g], BF16, tm_mm, scale_first_tile=HEAD_DIM ** -0.5)
           for g in range(3)]
    z2 = _mm_nn("b_gate_proj", x1b, w_z, F32, tm_mm)
    og, lg = zip(*[_attn_fwd(g, qkv[g], D) for g in range(3)])
    d1, d2 = DILATIONS[1], DILATIONS[2]
    (v2b, ds2, ds2b, dz2b, da0, da1, da2, ls0, ls1, ls2, dl0, dl1, dl2, loss_acc, sums_b) = _b_merge_out_loss(
        og[0], og[1].reshape(d1, S // d1, D), og[2].reshape(d2, S // d2, D),
        lg[0], lg[1].reshape(d1, S // d1, LSE_LANES), lg[2].reshape(d2, S // d2, LSE_LANES),
        z2, xh1, target, w_out_b, b_b_out, pg0, pb0, pg1, pb1, tm)

    das = [da0, da1.reshape(S, D), da2.reshape(S, D)]
    lss = [ls0, ls1.reshape(S, LSE_LANES), ls2.reshape(S, LSE_LANES)]
    dls = [dl0, dl1.reshape(S, LSE_LANES), dl2.reshape(S, LSE_LANES)]
    dq, dkv = zip(*[_attn_bwd(g, qkv[g], das[g], lss[g], dls[g], D) for g in range(3)])
    ds1, ds1b, sums_1 = _b_dx1_ln1_bwd(ds2, dz2b, dq, dkv, xh1, rs1, w_z, w_g, pg0, tm)

    def by_chip_cols(gw):
        s4 = _stack_cols(gw)
        return s4.reshape(N_CHIPS, 2, D // 2, s4.shape[2])

    def by_chip_rows(gw):
        return gw.reshape(N_CHIPS, 2, D // 8, D)

    core = lax.axis_index("c").astype(jnp.int32).reshape(1)
    chip = jchip.astype(jnp.int32).reshape(1)

    g_w_out_b = _mm_tn("dw_b_out", v2b, ds2b, tm_mm)
    g_q = [_mm_tn("dw_b_q_g%d" % g, x1g[g], dq[g], tm_mm) for g in range(3)]
    g_z = _mm_tn("dw_b_z", x1b, dz2b, tm_mm)
    g_kvg = [_mm_tn("dw_kv_g%d" % g, x1g[g], dkv[g], tm_mm) for g in range(3)]
    g_kv = jnp.concatenate([t[:, :D] for t in g_kvg] + [t[:, D:] for t in g_kvg], axis=1)
    grads_b = [by_chip_cols(g_kv), jnp.stack(g_q + [g_z]).reshape(N_CHIPS, 2, D // 2, D), by_chip_rows(g_w_out_b)]
    du1, dzab, sums_a, recvd_b = _a_gate_bwd(ds1b, h, xhu, rsu, w_out_a, lng_full, lnb_full, grads_b, tm)
    parts_b, wire_b = _pair_sum(grads_b, recvd_b, core, "b")
    dag, sums_c, wsum, landed_b = _a_conv_bwd(du1, h, wdw_full, wire_b, tm)
    own_b = _chip_sum(parts_b, landed_b, chip, "b")
    other_b = _pair_share(own_b, "b")

    g_w_in = jnp.concatenate([_mm_tn("dw_a_in_ag", xb, dag, tm_mm), _mm_tn("dw_a_in_z", xb, dzab, tm_mm)], axis=1)
    g_w_out_a = _mm_tn("dw_a_out", vb, ds1b, tm_mm)
    grads_a = [by_chip_cols(g_w_in), by_chip_rows(g_w_out_a)]
    recvd_a = _exchange(_PairExchange, "grad_pair_exchange_a", grads_a)
    parts_a, wire_a = _pair_sum(grads_a, recvd_a, core, "a")
    grad_x, landed_a = _a_dx(ds1, dag, dzab, w_in_a, wire_a, 2 * tm)
    own_a = _chip_sum(parts_a, landed_a, chip, "a")
    other_a = _pair_share(own_a, "a")
    own_half = [own_a[0], own_b[0], own_b[1], own_a[1], own_b[2]]
    other_half = [other_a[0], other_b[0], other_b[1], other_a[1], other_b[2]]

    small_grads = _pack_rows([sums_c[0:1], sums_c[1:2], sums_a[3:4], wsum, sums_a[2:3], sums_a[0:1], sums_a[1:2],
                              sums_1[2:3], sums_b[2:3], sums_1[0:1], sums_b[0:1], sums_1[1:2], sums_b[1:2]], D)
    small_sum = _sum_devices(_gather_all_devices(small_grads))
    loss = lax.psum(loss_acc[0, 0], ("x", "y", "c"))

    big_m = [m_a_w_in[0], m_kv_w, m_b_w_in[0], m_a_w_out[0], m_b_w_out[0]]
    big_v = [v_a_w_in[0], v_kv_w, v_b_w_in[0], v_a_w_out[0], v_b_w_out[0]]
    shards, big_delta, big_new_m, big_new_v = [
        [a.reshape(2 * a.shape[1], a.shape[2]) for a in group] for group in _adamw_halves(
            [_halves(w) for w in big_local], own_half, other_half, [_halves(m) for m in big_m],
            [_halves(v) for v in big_v], core)]

    def chip_cols(rows):
        return lax.dynamic_slice_in_dim(rows, jchip * dq4, dq4, axis=1)

    g_b_in = lax.dynamic_slice_in_dim(small_sum[0:3].reshape(1, 3 * D), jchip * 3 * dq4, 3 * dq4, axis=1)
    small_g = [g_b_in, chip_cols(small_sum[3:3 + CONV_WIDTH]), chip_cols(small_sum[35:36]), chip_cols(small_sum[36:37]),
               chip_cols(small_sum[37:38]), chip_cols(small_sum[38:39]), small_sum[39:40], small_sum[40:42],
               small_sum[42:44]]
    small_w = [a_b_in, a_w_dw[0], a_b_dw, a_ln_g, a_ln_b, a_b_out, b_b_out, post_ln_g, post_ln_b]
    small_m = [m_a_b_in, m_a_w_dw[0], m_a_b_dw, m_a_ln_g, m_a_ln_b, m_a_b_out, m_b_b_out, m_post_ln_g, m_post_ln_b]
    small_v = [v_a_b_in, v_a_w_dw[0], v_a_b_dw, v_a_ln_g, v_a_ln_b, v_a_b_out, v_b_b_out, v_post_ln_g, v_post_ln_b]
    small_delta, small_new_m, small_new_v = _adamw("adamw_small", small_w, small_g, small_m, small_v, 1)

    def ordered(big, sm):
        return (big[0][None], sm[0], sm[1][None], sm[2], sm[3], sm[4], big[3][None], sm[5], big[1], big[2][None],
                big[4][None], sm[6], sm[7], sm[8])

    return (loss, grad_x.reshape(1, S, D), *ordered(shards, small_g), *ordered(big_delta, small_delta),
            *ordered(big_new_m, small_new_m), *ordered(big_new_v, small_new_v))
```

```python
import functools

import numpy as np
import jax
import jax.numpy as jnp
from jax import lax
from jax.experimental import pallas as pl
from jax.experimental.pallas import tpu as pltpu

F32 = jnp.float32
BF16 = jnp.bfloat16
MESH = pl.DeviceIdType.MESH
SDS = jax.ShapeDtypeStruct

HEAD_DIM = 64
BLOCK = 128
DILATIONS = (1, 4, 16)
ALIBI_MAX_EXP = 8.0
CONV_WIDTH = 31
HALO = 32
CONV_ROWS = 128
LSE_LANES = 128
DEPTH = 2
ALPHA = (2.0 * DEPTH) ** 0.25
LN_EPS = 1e-5
ADAM_LR = 0.001
ADAM_B1 = 0.9
ADAM_B2 = 0.999
ADAM_EPS = 1e-08
ADAM_WD = 0.01
ADAM_STEP = 10
N_CHIPS = 4
N_DEV = 8
VMEM_LIMIT = 56 * 2 ** 20
SMALL_ROWS = 48


def _params(sem=None):
    return pltpu.CompilerParams(dimension_semantics=sem, vmem_limit_bytes=VMEM_LIMIT)


def _sigmoid(x):
    return 1.0 / (1.0 + jnp.exp(-x))


def _silu_grad(x, s):
    return s * (1.0 + x * (1.0 - s))


def _ln_fwd(x):
    mu = jnp.mean(x, axis=-1, keepdims=True)
    xc = x - mu
    var = jnp.mean(xc * xc, axis=-1, keepdims=True)
    rstd = lax.rsqrt(var + LN_EPS)
    return xc * rstd, rstd


def _ln_bwd(dxhat, xhat, rstd):
    m1 = jnp.mean(dxhat, axis=-1, keepdims=True)
    m2 = jnp.mean(dxhat * xhat, axis=-1, keepdims=True)
    return rstd * (dxhat - m1 - xhat * m2)


def _dot(a, b):
    return jnp.dot(a, b, preferred_element_type=F32)


def _dot_t(a, b):
    return lax.dot_general(a, b, (((1,), (1,)), ((), ())), preferred_element_type=F32)


def _tdot(a, b):
    return lax.dot_general(a, b, (((0,), (0,)), ((), ())), preferred_element_type=F32)


def _colsum(x):
    return jnp.sum(x, axis=0, keepdims=True)


def _slopes(n_heads):
    return [float(np.float32(2.0 ** (-ALIBI_MAX_EXP * (h + 1) / n_heads))) for h in range(n_heads)]


def _to_chunks(chunks_ref, x):
    for cc in range(chunks_ref.shape[0]):
        chunks_ref[cc] = x[:, cc * 128:(cc + 1) * 128]


def _deinterleave(chunks_ref, out_ref, d, dtype):
    rows = chunks_ref.shape[1] // d
    for r in range(d):
        for cc in range(chunks_ref.shape[0]):
            out_ref[r, :, cc * 128:(cc + 1) * 128] = chunks_ref[cc, pl.ds(r, rows, stride=d), :].astype(dtype)


def _interleave(in_ref, chunks_ref, d):
    rows = chunks_ref.shape[1] // d
    for r in range(d):
        for cc in range(chunks_ref.shape[0]):
            chunks_ref[cc, pl.ds(r, rows, stride=d), :] = in_ref[r, :, cc * 128:(cc + 1) * 128]


def _hbm_specs(n):
    return [pl.BlockSpec(memory_space=pl.ANY)] * n


def _position():
    x, y, c = lax.axis_index("x"), lax.axis_index("y"), lax.axis_index("c")
    return x, y, c


def _gather_stages(ins, outs, send_sems, recv_sems, local_sems):
    n = len(ins)

    def plan():
        x, y, c = _position()
        j = 2 * x + y
        me, sibling = (x, y, c), (x, y, 1 - c)
        chips = [(1 - x, y), (x, 1 - y), (1 - x, 1 - y)]

        def copy(i, k, src, dst, to):
            return pltpu.make_async_remote_copy(
                src_ref=src, dst_ref=dst, send_sem=send_sems.at[i, k], recv_sem=recv_sems.at[i, k],
                device_id=to, device_id_type=MESH)

        local = [pltpu.make_async_copy(ins[i], outs[i].at[j], local_sems.at[i]) for i in range(n)]
        first, landing, passed, passed_landing = [], [], [], []
        for i in range(n):
            for k, chip in enumerate(chips):
                pj = 2 * chip[0] + chip[1]
                first.append(copy(i, k, ins[i].at[c], outs[i].at[j, c], (*chip, c)))
                landing.append(copy(i, k, ins[i].at[c], outs[i].at[pj, c], me))
                passed.append(copy(i, 3 + k, outs[i].at[pj, c], outs[i].at[pj, c], sibling))
                passed_landing.append(copy(i, 3 + k, ins[i].at[c], outs[i].at[pj, 1 - c], me))
        return local, first, landing, passed, passed_landing

    def start():
        local, first, _, _, _ = plan()
        for cp in local + first:
            cp.start()

    def forward():
        _, _, landing, passed, _ = plan()
        for arrived, cp in zip(landing, passed):
            arrived.wait_recv()
            cp.start()

    def finish():
        local, first, _, passed, passed_landing = plan()
        for cp in passed_landing:
            cp.wait_recv()
        for cp in first + passed:
            cp.wait_send()
        for cp in local:
            cp.wait()

    return start, forward, finish


def _gather_scratch(n):
    return [pltpu.SemaphoreType.DMA((n, 6)), pltpu.SemaphoreType.DMA((n, 6)), pltpu.SemaphoreType.DMA((n,))]


def _all_gather_chips(shards):
    n = len(shards)

    def body(*refs):
        for stage in _gather_stages(refs[:n], refs[n:2 * n], *refs[2 * n:]):
            stage()

    return pl.pallas_call(
        body, name="all_gather_chips",
        out_shape=[SDS((N_CHIPS,) + s.shape, s.dtype) for s in shards],
        in_specs=_hbm_specs(n), out_specs=_hbm_specs(n), scratch_shapes=_gather_scratch(n),
    )(*shards)


class _PairExchange:
    slots = N_CHIPS

    @staticmethod
    def out_shape(g):
        return SDS((N_CHIPS,) + g.shape[2:], g.dtype)

    @staticmethod
    def copies(ins, outs, send_sems, recv_sems):
        x, y, c = _position()
        return [pltpu.make_async_remote_copy(
            src_ref=ins[i].at[j, 1 - c], dst_ref=outs[i].at[j],
            send_sem=send_sems.at[i, j], recv_sem=recv_sems.at[i, j],
            device_id=(x, y, 1 - c), device_id_type=MESH) for i in range(len(ins)) for j in range(N_CHIPS)]


class _ChipScatter:
    slots = 3

    @staticmethod
    def out_shape(p):
        return SDS((3,) + p.shape[1:], p.dtype)

    @staticmethod
    def copies(ins, outs, send_sems, recv_sems):
        x, y, c = _position()
        chips = [(1 - x, y), (x, 1 - y), (1 - x, 1 - y)]
        return [pltpu.make_async_remote_copy(
            src_ref=ins[i].at[2 * chip[0] + chip[1]], dst_ref=outs[i].at[k],
            send_sem=send_sems.at[i, k], recv_sem=recv_sems.at[i, k],
            device_id=(*chip, c), device_id_type=MESH) for i in range(len(ins)) for k, chip in enumerate(chips)]


def _exchange_start(kind, *refs):
    for cp in kind.copies(*refs):
        cp.start()


def _exchange_finish(kind, *refs):
    copies = kind.copies(*refs)
    for cp in copies:
        cp.wait_recv()
    for cp in copies:
        cp.wait_send()


def _exchange_scratch(kind, n):
    return [pltpu.SemaphoreType.DMA((n, kind.slots)), pltpu.SemaphoreType.DMA((n, kind.slots))]


def _exchange(kind, name, arrays):
    n = len(arrays)

    def body(*refs):
        args = (refs[:n], refs[n:2 * n], *refs[2 * n:])
        _exchange_start(kind, *args)
        _exchange_finish(kind, *args)

    return pl.pallas_call(
        body, name=name, out_shape=[kind.out_shape(a) for a in arrays],
        in_specs=_hbm_specs(n), out_specs=_hbm_specs(n), scratch_shapes=_exchange_scratch(kind, n),
    )(*arrays)


def _pair_share(halves, tag):
    n = len(halves)

    def body(*refs):
        ins, outs = refs[:n], refs[n:2 * n]
        send_sems, recv_sems = refs[2 * n:]
        x, y, c = _position()
        remote = [pltpu.make_async_remote_copy(
            src_ref=ins[i], dst_ref=outs[i], send_sem=send_sems.at[i], recv_sem=recv_sems.at[i],
            device_id=(x, y, 1 - c), device_id_type=MESH) for i in range(n)]
        for cp in remote:
            cp.start()
        for cp in remote:
            cp.wait_recv()
        for cp in remote:
            cp.wait_send()

    return pl.pallas_call(
        body, name="grad_pair_share_" + tag,
        out_shape=[SDS(h.shape, h.dtype) for h in halves],
        in_specs=_hbm_specs(n), out_specs=_hbm_specs(n),
        scratch_shapes=[pltpu.SemaphoreType.DMA((n,)), pltpu.SemaphoreType.DMA((n,))],
    )(*halves)


def _gather_all_devices(slab):
    def body(in_ref, out_ref, send_sems, recv_sems, local_sem):
        x, y, c = _position()
        me = 4 * x + 2 * y + c
        local = pltpu.make_async_copy(in_ref, out_ref.at[me], local_sem)
        local.start()
        remote, landing = [], []
        for mask in range(1, N_DEV):
            px, py, pc = x ^ (mask >> 2), y ^ ((mask >> 1) & 1), c ^ (mask & 1)
            peer = 4 * px + 2 * py + pc
            remote.append(pltpu.make_async_remote_copy(
                src_ref=in_ref, dst_ref=out_ref.at[me], send_sem=send_sems.at[mask - 1],
                recv_sem=recv_sems.at[mask - 1], device_id=(px, py, pc), device_id_type=MESH))
            landing.append(pltpu.make_async_remote_copy(
                src_ref=in_ref, dst_ref=out_ref.at[peer], send_sem=send_sems.at[mask - 1],
                recv_sem=recv_sems.at[mask - 1], device_id=(px, py, pc), device_id_type=MESH))
        for cp in remote:
            cp.start()
        for cp in landing:
            cp.wait_recv()
        for cp in remote:
            cp.wait_send()
        local.wait()

    return pl.pallas_call(
        body, name="small_grad_gather",
        out_shape=SDS((N_DEV,) + slab.shape, slab.dtype),
        in_specs=_hbm_specs(1), out_specs=pl.BlockSpec(memory_space=pl.ANY),
        scratch_shapes=[pltpu.SemaphoreType.DMA((N_DEV - 1,)), pltpu.SemaphoreType.DMA((N_DEV - 1,)),
                        pltpu.SemaphoreType.DMA],
    )(slab)


def _row_splits(arrays):
    return min(a.shape[-2] for a in arrays) // 16


def _pair_sum(grads, recvd, core, tag):
    n = len(grads)
    splits = _row_splits(recvd)

    def body(core_ref, *refs):
        for i in range(n):
            s = refs[i][...] + refs[n + i][...]
            refs[2 * n + i][...] = s
            refs[3 * n + i][...] = s.astype(BF16)

    mine = [pl.BlockSpec((N_CHIPS, None, r.shape[1] // splits, r.shape[2]), lambda s, core: (0, core[0], s, 0))
            for r in recvd]
    block = [pl.BlockSpec((N_CHIPS, r.shape[1] // splits, r.shape[2]), lambda s, core: (0, s, 0)) for r in recvd]
    outs = pl.pallas_call(
        body, name="grad_pair_sum_" + tag,
        grid_spec=pltpu.PrefetchScalarGridSpec(
            num_scalar_prefetch=1, grid=(splits,), in_specs=mine + block, out_specs=block + block),
        out_shape=[SDS(r.shape, F32) for r in recvd] + [SDS(r.shape, BF16) for r in recvd],
        compiler_params=_params(("parallel",)),
    )(core, *grads, *recvd)
    return outs[:n], outs[n:]


def _chip_sum(parts, landed, chip, tag):
    n = len(parts)
    splits = _row_splits(landed)

    def body(chip_ref, *refs):
        for i in range(n):
            acc = refs[i][...]
            for k in range(3):
                acc = acc + refs[n + i][k].astype(F32)
            refs[2 * n + i][...] = acc

    rows = lambda p: p.shape[1] // splits
    return pl.pallas_call(
        body, name="grad_chip_sum_" + tag,
        grid_spec=pltpu.PrefetchScalarGridSpec(
            num_scalar_prefetch=1, grid=(splits,),
            in_specs=[pl.BlockSpec((None, rows(p), p.shape[2]), lambda s, chip: (chip[0], s, 0)) for p in parts]
            + [pl.BlockSpec((3, rows(p), p.shape[2]), lambda s, chip: (0, s, 0)) for p in parts],
            out_specs=[pl.BlockSpec((rows(p), p.shape[2]), lambda s, chip: (s, 0)) for p in parts]),
        out_shape=[SDS(p.shape[1:], F32) for p in parts],
        compiler_params=_params(("parallel",)),
    )(chip, *parts, *landed)


def _adamw_math(w, g, m, v):
    m = ADAM_B1 * m + (1.0 - ADAM_B1) * g
    v = ADAM_B2 * v + (1.0 - ADAM_B2) * (g * g)
    m_hat = m / (1.0 - ADAM_B1 ** ADAM_STEP)
    v_hat = v / (1.0 - ADAM_B2 ** ADAM_STEP)
    delta = -ADAM_LR * (m_hat / (jnp.sqrt(v_hat) + ADAM_EPS) + ADAM_WD * w)
    return delta, m, v


def _adamw(name, ws, gs, ms, vs, splits):
    n = len(ws)

    def body(*refs):
        for i in range(n):
            w, g, m, v = (refs[q * n + i][...] for q in range(4))
            delta, m, v = _adamw_math(w, g, m, v)
            refs[4 * n + i][...] = delta
            refs[5 * n + i][...] = m
            refs[6 * n + i][...] = v

    def spec(a):
        if splits == 1:
            return pl.BlockSpec(a.shape, lambda s: (0, 0))
        return pl.BlockSpec((a.shape[0] // splits, a.shape[1]), lambda s: (s, 0))

    specs = [spec(a) for a in ws]
    outs = pl.pallas_call(
        body, name=name, grid=(splits,),
        in_specs=specs * 4, out_specs=specs * 3,
        out_shape=[SDS(a.shape, F32) for a in ws] * 3,
        compiler_params=_params(("parallel",)),
    )(*ws, *gs, *ms, *vs)
    return outs[:n], outs[n:2 * n], outs[2 * n:]


def _adamw_halves(ws, own, other, ms, vs, core):
    n = len(ws)
    splits = _row_splits(own)

    def body(core_ref, *refs):
        mine = pl.program_id(0) == core_ref[0]
        for i in range(n):
            g = jnp.where(mine, refs[n + i][...], refs[2 * n + i][...])
            delta, m, v = _adamw_math(refs[i][...], g, refs[3 * n + i][...], refs[4 * n + i][...])
            refs[5 * n + i][...] = g
            refs[6 * n + i][...] = delta
            refs[7 * n + i][...] = m
            refs[8 * n + i][...] = v

    rows = lambda a: a.shape[0] // splits
    half = [pl.BlockSpec((None, rows(a), a.shape[1]), lambda hh, s, core: (hh, s, 0)) for a in own]
    flat = [pl.BlockSpec((rows(a), a.shape[1]), lambda hh, s, core: (s, 0)) for a in own]
    outs = pl.pallas_call(
        body, name="adamw_big",
        grid_spec=pltpu.PrefetchScalarGridSpec(
            num_scalar_prefetch=1, grid=(2, splits), in_specs=half + flat + flat + half + half, out_specs=half * 4),
        out_shape=[SDS(w.shape, F32) for w in ws] * 4,
        compiler_params=_params(("parallel", "parallel")),
    )(core, *ws, *own, *other, *ms, *vs)
    return outs[:n], outs[n:2 * n], outs[2 * n:3 * n], outs[3 * n:]


def _sum_devices(slabs):
    def body(in_ref, out_ref):
        acc = in_ref[0]
        for k in range(1, N_DEV):
            acc = acc + in_ref[k]
        out_ref[...] = acc

    return pl.pallas_call(
        body, name="small_grad_sum", out_shape=SDS(slabs.shape[1:], F32),
    )(slabs)


def _mm_nn(name, a, w, out_dtype, tm, scale_first_tile=None):
    S, K = a.shape
    N = w.shape[1]
    tn = K

    def body(a_ref, w_ref, o_ref):
        acc = _dot(a_ref[...], w_ref[...])
        if scale_first_tile is not None:
            acc = acc * jnp.where(pl.program_id(1) == 0, scale_first_tile, 1.0)
        o_ref[...] = acc.astype(out_dtype)

    return pl.pallas_call(
        body, name=name, grid=(S // tm, N // tn),
        in_specs=[pl.BlockSpec((tm, K), lambda i, t: (i, 0)), pl.BlockSpec((K, tn), lambda i, t: (0, t))],
        out_specs=pl.BlockSpec((tm, tn), lambda i, t: (i, t)),
        out_shape=SDS((S, N), out_dtype),
        compiler_params=_params(("parallel", "arbitrary")),
    )(a, w)


def _mm_tn(name, a, b, tk):
    S, M = a.shape
    N = b.shape[1]
    tn = M

    def body(a_ref, b_ref, o_ref):
        @pl.when(pl.program_id(1) == 0)
        def _():
            o_ref[...] = jnp.zeros_like(o_ref)
        o_ref[...] += _tdot(a_ref[...], b_ref[...])

    return pl.pallas_call(
        body, name=name, grid=(N // tn, S // tk),
        in_specs=[pl.BlockSpec((tk, M), lambda t, k: (k, 0)), pl.BlockSpec((tk, tn), lambda t, k: (k, t))],
        out_specs=pl.BlockSpec((M, tn), lambda t, k: (0, t)),
        out_shape=SDS((M, N), F32),
        compiler_params=_params(("parallel", "arbitrary")),
    )(a, b)


def _a_in_proj(x, w4, b_in, later_shards, tm):
    S, D = x.shape
    nj = w4.shape[2]
    n = len(later_shards)
    steps = S // tm

    def body(x_ref, w_ref, b_ref, *refs):
        shard_refs, (h_ref, xb_ref), gathered_refs = refs[:n], refs[n:n + 2], refs[n + 2:2 * n + 2]
        start, forward, finish = _gather_stages(shard_refs, gathered_refs, *refs[2 * n + 2:])
        i, t = pl.program_id(0), pl.program_id(1)
        pl.when((i == 0) & (t == 0))(start)
        pl.when((i == steps // 2) & (t == 0))(forward)
        xb = x_ref[...].astype(BF16)

        @pl.when(t == 0)
        def _():
            xb_ref[...] = xb
        h_ref[...] = _dot(xb, w_ref[...]) + b_ref[...]
        pl.when((i == steps - 1) & (t == N_CHIPS - 1))(finish)

    outs = pl.pallas_call(
        body, name="a_in_proj", grid=(steps, N_CHIPS),
        in_specs=[pl.BlockSpec((tm, D), lambda i, t: (i, 0)),
                  pl.BlockSpec((None, D, nj), lambda i, t: (t, 0, 0)),
                  pl.BlockSpec((1, nj), lambda i, t: (0, t))] + _hbm_specs(n),
        out_specs=[pl.BlockSpec((tm, nj), lambda i, t: (i, t)), pl.BlockSpec((tm, D), lambda i, t: (i, 0))]
        + _hbm_specs(n),
        out_shape=[SDS((S, N_CHIPS * nj), F32), SDS((S, D), BF16)]
        + [SDS((N_CHIPS,) + s.shape, s.dtype) for s in later_shards],
        scratch_shapes=_gather_scratch(n),
        compiler_params=_params(("arbitrary", "arbitrary")),
    )(x, w4, b_in, *later_shards)
    return outs[0], outs[1], outs[2:]


def _fill_glu_ext(ext_ref, a_ref, g_ref, ah_ref, gh_ref, has_prev):
    u0h = ah_ref[...] * _sigmoid(gh_ref[...])
    ext_ref[0:HALO, :] = jnp.where(has_prev, u0h, 0.0)
    ext_ref[HALO:, :] = a_ref[...] * _sigmoid(g_ref[...])


def _fill_shifts(shift_ref, ext_ref):
    for s in range(1, 8):
        shift_ref[s - 1] = ext_ref[s:s + shift_ref.shape[1], :]


def _tap_windows(ext_ref, shift_ref, starts, r0, rows, lanes):
    for s in range(8):
        taps = [(k, st) for k, st in enumerate(starts) if st % 8 == s]
        if not taps:
            continue
        lo = min(st for _, st in taps)
        hi = max(st for _, st in taps)
        if s == 0:
            win = ext_ref[r0 + lo:r0 + hi + rows, lanes]
        else:
            win = shift_ref[s - 1, r0 + lo - s:r0 + hi - s + rows, lanes]
        for k, st in taps:
            yield k, win[st - lo:st - lo + rows]


def _a_conv_out(h, x, wdw, bdw, lng, lnb, wout, bout, pg, pb, tm):
    S, D = x.shape
    hb = tm // HALO
    d1, d2 = DILATIONS[1], DILATIONS[2]

    def body(a_ref, g_ref, z_ref, ah_ref, gh_ref, x_ref, wdw_ref, bdw_ref, lng_ref, lnb_ref, wout_ref,
             bout_ref, pg_ref, pb_ref, xhu_ref, rsu_ref, vb_ref, xh1_ref, rs1_ref, x1b_ref, x1p1_ref,
             x1p2_ref, ext_ref, u1_ref, x1_ref, sh_ref):
        i = pl.program_id(0)
        _fill_glu_ext(ext_ref, a_ref, g_ref, ah_ref, gh_ref, i > 0)
        _fill_shifts(sh_ref, ext_ref)
        starts = [HALO - (CONV_WIDTH - 1) + k for k in range(CONV_WIDTH)]
        for cc in range(D // 128):
            lanes = slice(cc * 128, (cc + 1) * 128)
            for r0 in range(0, tm, CONV_ROWS):
                acc = jnp.broadcast_to(bdw_ref[:, lanes], (CONV_ROWS, 128))
                for k, win in _tap_windows(ext_ref, sh_ref, starts, r0, CONV_ROWS, lanes):
                    acc = acc + wdw_ref[k:k + 1, lanes] * win
                u1_ref[r0:r0 + CONV_ROWS, lanes] = acc
        xhu, rsu = _ln_fwd(u1_ref[...])
        xhu_ref[...] = xhu
        rsu_ref[...] = rsu
        u2 = xhu * lng_ref[...] + lnb_ref[...]
        z = z_ref[...]
        v = (u2 * _sigmoid(u2)) * (z * _sigmoid(z))
        vb = v.astype(BF16)
        vb_ref[...] = vb
        s1 = ALPHA * x_ref[...] + _dot(vb, wout_ref[...]) + bout_ref[...]
        xh1, rs1 = _ln_fwd(s1)
        xh1_ref[...] = xh1
        rs1_ref[...] = rs1
        x1 = xh1 * pg_ref[...] + pb_ref[...]
        x1b_ref[...] = x1.astype(BF16)
        _to_chunks(x1_ref, x1)
        _deinterleave(x1_ref, x1p1_ref, d1, BF16)
        _deinterleave(x1_ref, x1p2_ref, d2, BF16)

    tile = lambda c: pl.BlockSpec((tm, D), lambda i, c=c: (i, c))
    halo = lambda c: pl.BlockSpec((HALO, D), lambda i, c=c: (jnp.maximum(i * hb - 1, 0), c))
    row = pl.BlockSpec((1, D), lambda i: (0, 0))
    stat = pl.BlockSpec((tm, 1), lambda i: (i, 0))
    return pl.pallas_call(
        body, name="a_conv_out", grid=(S // tm,),
        in_specs=[tile(0), tile(1), tile(2), halo(0), halo(1), tile(0),
                  pl.BlockSpec((HALO, D), lambda i: (0, 0)), row, row, row,
                  pl.BlockSpec((D, D), lambda i: (0, 0)), row, row, row],
        out_specs=[tile(0), stat, tile(0), tile(0), stat, tile(0),
                   pl.BlockSpec((d1, tm // d1, D), lambda i: (0, i, 0)),
                   pl.BlockSpec((d2, tm // d2, D), lambda i: (0, i, 0))],
        out_shape=[SDS((S, D), F32), SDS((S, 1), F32), SDS((S, D), BF16), SDS((S, D), F32), SDS((S, 1), F32),
                   SDS((S, D), BF16), SDS((d1, S // d1, D), BF16), SDS((d2, S // d2, D), BF16)],
        scratch_shapes=[pltpu.VMEM((HALO + tm, D), F32), pltpu.VMEM((tm, D), F32),
                        pltpu.VMEM((D // 128, tm, 128), F32), pltpu.VMEM((7, HALO + tm - 8, D), F32)],
        compiler_params=_params(("parallel",)),
    )(h, h, h, h, h, x, wdw, bdw, lng, lnb, wout, bout, pg, pb)


def _band(n, dilation):
    qi = lax.broadcasted_iota(jnp.int32, (BLOCK, 2 * BLOCK), 0)
    kj = lax.broadcasted_iota(jnp.int32, (BLOCK, 2 * BLOCK), 1)
    dist = qi + BLOCK - kj
    valid = (dist >= 0) & (dist <= BLOCK) & ((n > 0) | (kj >= BLOCK))
    return jnp.where(valid, dist.astype(F32) * float(-dilation), -jnp.inf)


def _attn_fwd(g, qkv, D):
    S = qkv.shape[0]
    d = DILATIONS[g]
    nb = S // (d * BLOCK)
    H = D // HEAD_DIM
    slopes = _slopes(H)

    def body(q_ref, kp_ref, kc_ref, vp_ref, vc_ref, o_ref, lse_ref):
        neg_dist = _band(pl.program_id(1), d)
        lane = lax.broadcasted_iota(jnp.int32, (BLOCK, LSE_LANES), 1)
        low = lane < HEAD_DIM
        lse = jnp.zeros((BLOCK, LSE_LANES), F32)
        for hp in range(H // 2):
            sl = slice(hp * 128, (hp + 1) * 128)
            q = q_ref[:, sl]
            k = jnp.concatenate([kp_ref[:, sl], kc_ref[:, sl]], axis=0)
            v = jnp.concatenate([vp_ref[:, sl], vc_ref[:, sl]], axis=0)
            o = []
            for a in range(2):
                h = 2 * hp + a
                s = _dot_t(jnp.where(low if a == 0 else ~low, q, jnp.zeros_like(q)), k)
                s = s + slopes[h] * neg_dist
                m = jnp.max(s, axis=1, keepdims=True)
                p = jnp.exp(s - m)
                l = jnp.sum(p, axis=1, keepdims=True)
                o.append(_dot(p.astype(BF16), v) * (1.0 / l))
                lse = jnp.where(lane == h, m + jnp.log(l), lse)
            o_ref[:, sl] = jnp.where(low, o[0], o[1])
        lse_ref[...] = lse

    cur = lambda c: pl.BlockSpec((BLOCK, D), lambda r, n, c=c: (r * nb + n, c))
    prev = lambda c: pl.BlockSpec((BLOCK, D), lambda r, n, c=c: (r * nb + jnp.maximum(n - 1, 0), c))
    return pl.pallas_call(
        body, name="attn_fwd_g%d" % g, grid=(d, nb),
        in_specs=[cur(0), prev(1), cur(1), prev(2), cur(2)],
        out_specs=[cur(0), pl.BlockSpec((BLOCK, LSE_LANES), lambda r, n: (r * nb + n, 0))],
        out_shape=[SDS((S, D), F32), SDS((S, LSE_LANES), F32)],
        compiler_params=_params(("parallel", "parallel")),
    )(qkv, qkv, qkv, qkv, qkv)


def _attn_bwd(g, qkv, do, lse, delta, D):
    S = qkv.shape[0]
    d = DILATIONS[g]
    nb = S // (d * BLOCK)
    H = D // HEAD_DIM
    slopes = _slopes(H)

    def body(q_ref, kp_ref, kc_ref, vp_ref, vc_ref, do_ref, lse_ref, dl_ref, dq_ref, dkv_ref, ck_ref, cv_ref):
        n = pl.program_id(1)

        @pl.when(n == 0)
        def _():
            ck_ref[...] = jnp.zeros_like(ck_ref)
            cv_ref[...] = jnp.zeros_like(cv_ref)

        @pl.when(n < nb)
        def _():
            neg_dist = _band(n, d)
            low = lax.broadcasted_iota(jnp.int32, (BLOCK, 128), 1) < HEAD_DIM
            low2 = lax.broadcasted_iota(jnp.int32, (2 * BLOCK, 128), 1) < HEAD_DIM
            for hp in range(H // 2):
                sl = slice(hp * 128, (hp + 1) * 128)
                q = q_ref[:, sl]
                do2 = do_ref[:, sl]
                k = jnp.concatenate([kp_ref[:, sl], kc_ref[:, sl]], axis=0)
                v = jnp.concatenate([vp_ref[:, sl], vc_ref[:, sl]], axis=0)
                zero = jnp.zeros_like(q)
                q_do = jnp.concatenate([jnp.concatenate([q, zero], axis=1),
                                        jnp.concatenate([zero, do2], axis=1)], axis=0)
                dq, dkv = [], []
                for a in range(2):
                    h = 2 * hp + a
                    keep = low if a == 0 else ~low
                    s = _dot_t(jnp.where(keep, q, zero), k)
                    s = s + slopes[h] * neg_dist
                    p = jnp.exp(s - lse_ref[:, h:h + 1])
                    dp = _dot_t(jnp.where(keep, do2, zero), v)
                    dsb = (p * (dp - dl_ref[:, h:h + 1])).astype(BF16)
                    dq.append(_dot(dsb, k))
                    dkv.append(_tdot(jnp.concatenate([dsb, p.astype(BF16)], axis=0), q_do))
                dq_ref[:, sl] = (jnp.where(low, dq[0], dq[1]) * (HEAD_DIM ** -0.5)).astype(BF16)
                dk2 = jnp.where(low2, dkv[0][:, :128], dkv[1][:, :128])
                dv2 = jnp.where(low2, dkv[0][:, 128:], dkv[1][:, 128:])
                dkv_ref[:, sl] = (ck_ref[:, sl] + dk2[:BLOCK]).astype(BF16)
                dkv_ref[:, D + hp * 128:D + (hp + 1) * 128] = (cv_ref[:, sl] + dv2[:BLOCK]).astype(BF16)
                ck_ref[:, sl] = dk2[BLOCK:]
                cv_ref[:, sl] = dv2[BLOCK:]

        @pl.when(n == nb)
        def _():
            dkv_ref[:, :D] = ck_ref[...].astype(BF16)
            dkv_ref[:, D:] = cv_ref[...].astype(BF16)

    nq = lambda n: jnp.minimum(n, nb - 1)
    cur = lambda c: pl.BlockSpec((BLOCK, D), lambda r, n, c=c: (r * nb + nq(n), c))
    prev = lambda c: pl.BlockSpec((BLOCK, D), lambda r, n, c=c: (r * nb + jnp.maximum(nq(n) - 1, 0), c))
    stat = pl.BlockSpec((BLOCK, LSE_LANES), lambda r, n: (r * nb + nq(n), 0))
    return pl.pallas_call(
        body, name="attn_bwd_g%d" % g, grid=(d, nb + 1),
        in_specs=[cur(0), prev(1), cur(1), prev(2), cur(2), cur(0), stat, stat],
        out_specs=[cur(0), pl.BlockSpec((BLOCK, 2 * D), lambda r, n: (r * nb + jnp.maximum(n - 1, 0), 0))],
        out_shape=[SDS((S, D), BF16), SDS((S, 2 * D), BF16)],
        scratch_shapes=[pltpu.VMEM((BLOCK, D), F32), pltpu.VMEM((BLOCK, D), F32)],
        compiler_params=_params(("parallel", "arbitrary")),
    )(qkv, qkv, qkv, qkv, qkv, do, lse, delta)


def _b_merge_out_loss(o0, o1, o2, l0, l1, l2, z2, xh1, target, wbo, bbo, pg0, pb0, pg1, pb1, tm):
    S, D = o0.shape
    H = D // HEAD_DIM
    d1, d2 = DILATIONS[1], DILATIONS[2]
    inv_d = 1.0 / D

    def body(o0_ref, o1_ref, o2_ref, l0_ref, l1_ref, l2_ref, z_ref, xh1_ref, t_ref, wbo_ref, bbo_ref,
             pg0_ref, pb0_ref, pg1_ref, pb1_ref,
             v2b_ref, ds2_ref, ds2b_ref, dz2b_ref, da0_ref, da1_ref, da2_ref, ls0_ref, ls1_ref, ls2_ref,
             dl0_ref, dl1_ref, dl2_ref, loss_ref, sums_ref,
             o1n_ref, o2n_ref, l1n_ref, l2n_ref, att_ref):
        i = pl.program_id(0)
        _interleave(o1_ref, o1n_ref, d1)
        _interleave(o2_ref, o2n_ref, d2)
        for r in range(d1):
            l1n_ref[pl.ds(r, tm // d1, stride=d1), :] = l1_ref[r]
        for r in range(d2):
            l2n_ref[pl.ds(r, tm // d2, stride=d2), :] = l2_ref[r]
        la, lb, lc = l0_ref[...], l1n_ref[...], l2n_ref[...]
        m = jnp.maximum(jnp.maximum(la, lb), lc)
        ea, eb, ec = jnp.exp(la - m), jnp.exp(lb - m), jnp.exp(lc - m)
        den = ea + eb + ec
        wa, wb, wc = ea / den, eb / den, ec / den
        ls0_ref[...] = m + jnp.log(den)
        for h in range(H):
            sl = slice(h * HEAD_DIM, (h + 1) * HEAD_DIM)
            cc, hl = divmod(h * HEAD_DIM, 128)
            att_ref[:, sl] = (wa[:, h:h + 1] * o0_ref[:, sl] + wb[:, h:h + 1] * o1n_ref[cc, :, hl:hl + HEAD_DIM]
                              + wc[:, h:h + 1] * o2n_ref[cc, :, hl:hl + HEAD_DIM])
        att = att_ref[...]
        z = z_ref[...]
        sz = _sigmoid(z)
        gate = z * sz
        v2b = (att * gate).astype(BF16)
        v2b_ref[...] = v2b
        x1 = xh1_ref[...] * pg0_ref[...] + pb0_ref[...]
        s2 = ALPHA * x1 + _dot(v2b, wbo_ref[...]) + bbo_ref[...]
        xh2, rs2 = _ln_fwd(s2)
        err = xh2 * pg1_ref[...] + pb1_ref[...] - t_ref[...]
        dy = err * inv_d
        ds2 = _ln_bwd(dy * pg1_ref[...], xh2, rs2)
        ds2b = ds2.astype(BF16)
        ds2_ref[...] = ds2
        ds2b_ref[...] = ds2b

        @pl.when(i == 0)
        def _():
            loss_ref[...] = jnp.zeros_like(loss_ref)
            sums_ref[...] = jnp.zeros_like(sums_ref)
        loss_ref[...] += 0.5 * inv_d * jnp.sum(err * err)
        sums_ref[0:1, :] += _colsum(dy * xh2)
        sums_ref[1:2, :] += _colsum(dy)
        sums_ref[2:3, :] += _colsum(ds2)

        dv2 = _dot_t(ds2b, wbo_ref[...])
        datt = dv2 * gate
        dz2b_ref[...] = (dv2 * att * _silu_grad(z, sz)).astype(BF16)
        prod = datt * att
        lane = lax.broadcasted_iota(jnp.int32, (tm, LSE_LANES), 1)
        dl = jnp.zeros((tm, LSE_LANES), F32)
        for h in range(H):
            sl = slice(h * HEAD_DIM, (h + 1) * HEAD_DIM)
            dl = jnp.where(lane == h, jnp.sum(prod[:, sl], axis=1, keepdims=True), dl)
        da0_ref[...] = datt.astype(BF16)
        dl0_ref[...] = dl
        _to_chunks(o1n_ref, datt)
        _deinterleave(o1n_ref, da1_ref, d1, BF16)
        _deinterleave(o1n_ref, da2_ref, d2, BF16)
        for r in range(d1):
            ls1_ref[r] = ls0_ref[pl.ds(r, tm // d1, stride=d1), :]
            dl1_ref[r] = dl0_ref[pl.ds(r, tm // d1, stride=d1), :]
        for r in range(d2):
            ls2_ref[r] = ls0_ref[pl.ds(r, tm // d2, stride=d2), :]
            dl2_ref[r] = dl0_ref[pl.ds(r, tm // d2, stride=d2), :]

    tile = pl.BlockSpec((tm, D), lambda i: (i, 0))
    stat = pl.BlockSpec((tm, LSE_LANES), lambda i: (i, 0))
    perm = lambda d, w: pl.BlockSpec((d, tm // d, w), lambda i: (0, i, 0))
    row = pl.BlockSpec((1, D), lambda i: (0, 0))
    acc = lambda w: pl.BlockSpec((8, w), lambda i: (0, 0))
    pshape = lambda d, w, dt: SDS((d, S // d, w), dt)
    return pl.pallas_call(
        body, name="b_merge_out_loss", grid=(S // tm,),
        in_specs=[tile, perm(d1, D), perm(d2, D), stat, perm(d1, LSE_LANES), perm(d2, LSE_LANES),
                  tile, tile, tile, pl.BlockSpec((D, D), lambda i: (0, 0)), row, row, row, row, row],
        out_specs=[tile, tile, tile, tile, tile, perm(d1, D), perm(d2, D),
                   stat, perm(d1, LSE_LANES), perm(d2, LSE_LANES),
                   stat, perm(d1, LSE_LANES), perm(d2, LSE_LANES), acc(LSE_LANES), acc(D)],
        out_shape=[SDS((S, D), BF16), SDS((S, D), F32), SDS((S, D), BF16), SDS((S, D), BF16),
                   SDS((S, D), BF16), pshape(d1, D, BF16), pshape(d2, D, BF16),
                   SDS((S, LSE_LANES), F32), pshape(d1, LSE_LANES, F32), pshape(d2, LSE_LANES, F32),
                   SDS((S, LSE_LANES), F32), pshape(d1, LSE_LANES, F32), pshape(d2, LSE_LANES, F32),
                   SDS((8, LSE_LANES), F32), SDS((8, D), F32)],
        scratch_shapes=[pltpu.VMEM((D // 128, tm, 128), F32), pltpu.VMEM((D // 128, tm, 128), F32),
                        pltpu.VMEM((tm, LSE_LANES), F32), pltpu.VMEM((tm, LSE_LANES), F32),
                        pltpu.VMEM((tm, D), F32)],
        compiler_params=_params(("arbitrary",)),
    )(o0, o1, o2, l0, l1, l2, z2, xh1, target, wbo, bbo, pg0, pb0, pg1, pb1)


def _b_dx1_ln1_bwd(ds2, dz2b, dq, dkv, xh1, rs1, wz, wg, pg0, tm):
    S, D = ds2.shape
    d1, d2 = DILATIONS[1], DILATIONS[2]

    def group_part(dq_blk, dkv_blk, w_ref):
        return (_dot_t(dq_blk, w_ref[:, 0:D]) + _dot_t(dkv_blk[:, 0:D], w_ref[:, D:2 * D])
                + _dot_t(dkv_blk[:, D:2 * D], w_ref[:, 2 * D:3 * D]))

    def body(ds2_ref, dz_ref, dq0_ref, dkv0_ref, dq1_ref, dkv1_ref, dq2_ref, dkv2_ref, xh1_ref, rs1_ref,
             wz_ref, w0_ref, w1_ref, w2_ref, pg0_ref, ds1_ref, ds1b_ref, sums_ref, acc_ref):
        i = pl.program_id(0)
        _to_chunks(acc_ref, ALPHA * ds2_ref[...] + _dot_t(dz_ref[...], wz_ref[...])
                   + group_part(dq0_ref[...], dkv0_ref[...], w0_ref))
        for d, dq_ref, dkv_ref, w_ref in ((d1, dq1_ref, dkv1_ref, w1_ref), (d2, dq2_ref, dkv2_ref, w2_ref)):
            rows = tm // d
            part = group_part(dq_ref[...].reshape(tm, D), dkv_ref[...].reshape(tm, 2 * D), w_ref)
            for r in range(d):
                idx = pl.ds(r, rows, stride=d)
                for cc in range(D // 128):
                    acc_ref[cc, idx, :] = acc_ref[cc, idx, :] + part[r * rows:(r + 1) * rows, cc * 128:(cc + 1) * 128]
        dx1 = jnp.concatenate([acc_ref[cc] for cc in range(D // 128)], axis=1)
        xh1 = xh1_ref[...]
        ds1 = _ln_bwd(dx1 * pg0_ref[...], xh1, rs1_ref[...])
        ds1_ref[...] = ds1
        ds1b_ref[...] = ds1.astype(BF16)

        @pl.when(i == 0)
        def _():
            sums_ref[...] = jnp.zeros_like(sums_ref)
        sums_ref[0:1, :] += _colsum(dx1 * xh1)
        sums_ref[1:2, :] += _colsum(dx1)
        sums_ref[2:3, :] += _colsum(ds1)

    tile = lambda w: pl.BlockSpec((tm, w), lambda i: (i, 0))
    perm = lambda d, w: pl.BlockSpec((d, tm // d, w), lambda i: (0, i, 0))
    whole = pl.BlockSpec(memory_space=pltpu.VMEM)
    return pl.pallas_call(
        body, name="b_dx1_ln1_bwd", grid=(S // tm,),
        in_specs=[tile(D), tile(D), tile(D), tile(2 * D), perm(d1, D), perm(d1, 2 * D), perm(d2, D),
                  perm(d2, 2 * D), tile(D), tile(1), whole, whole, whole, whole,
                  pl.BlockSpec((1, D), lambda i: (0, 0))],
        out_specs=[tile(D), tile(D), pl.BlockSpec((8, D), lambda i: (0, 0))],
        out_shape=[SDS((S, D), F32), SDS((S, D), BF16), SDS((8, D), F32)],
        scratch_shapes=[pltpu.VMEM((D // 128, tm, 128), F32)],
        compiler_params=_params(("arbitrary",)),
    )(ds2, dz2b, dq[0], dkv[0], dq[1].reshape(d1, S // d1, D), dkv[1].reshape(d1, S // d1, 2 * D),
      dq[2].reshape(d2, S // d2, D), dkv[2].reshape(d2, S // d2, 2 * D), xh1, rs1, wz, wg[0], wg[1], wg[2], pg0)


def _a_gate_bwd(ds1b, h, xhu, rsu, wout, lng, lnb, grads, tm):
    S, D = xhu.shape
    n = len(grads)
    steps = S // tm

    def body(ds_ref, z_ref, xhu_ref, rsu_ref, w_ref, lng_ref, lnb_ref, *refs):
        grad_refs, (du1_ref, dzb_ref, sums_ref), recvd_refs = refs[:n], refs[n:n + 3], refs[n + 3:2 * n + 3]
        exchange = (grad_refs, recvd_refs, *refs[2 * n + 3:])
        i = pl.program_id(0)
        pl.when(i == 0)(functools.partial(_exchange_start, _PairExchange, *exchange))
        dv = _dot_t(ds_ref[...], w_ref[...])
        xhu = xhu_ref[...]
        u2 = xhu * lng_ref[...] + lnb_ref[...]
        su = _sigmoid(u2)
        z = z_ref[...]
        sz = _sigmoid(z)
        dz = dv * (u2 * su) * _silu_grad(z, sz)
        du2 = dv * (z * sz) * _silu_grad(u2, su)
        du1 = _ln_bwd(du2 * lng_ref[...], xhu, rsu_ref[...])
        du1_ref[...] = du1
        dzb_ref[...] = dz.astype(BF16)

        @pl.when(i == 0)
        def _():
            sums_ref[...] = jnp.zeros_like(sums_ref)
        sums_ref[0:1, :] += _colsum(du2 * xhu)
        sums_ref[1:2, :] += _colsum(du2)
        sums_ref[2:3, :] += _colsum(du1)
        sums_ref[3:4, :] += _colsum(dz)
        pl.when(i == steps - 1)(functools.partial(_exchange_finish, _PairExchange, *exchange))

    tile = pl.BlockSpec((tm, D), lambda i: (i, 0))
    row = pl.BlockSpec((1, D), lambda i: (0, 0))
    outs = pl.pallas_call(
        body, name="a_gate_bwd", grid=(steps,),
        in_specs=[tile, pl.BlockSpec((tm, D), lambda i: (i, 2)), tile, pl.BlockSpec((tm, 1), lambda i: (i, 0)),
                  pl.BlockSpec((D, D), lambda i: (0, 0)), row, row] + _hbm_specs(n),
        out_specs=[tile, tile, pl.BlockSpec((8, D), lambda i: (0, 0))] + _hbm_specs(n),
        out_shape=[SDS((S, D), F32), SDS((S, D), BF16), SDS((8, D), F32)]
        + [_PairExchange.out_shape(g) for g in grads],
        scratch_shapes=_exchange_scratch(_PairExchange, n),
        compiler_params=_params(("arbitrary",)),
    )(ds1b, h, xhu, rsu, wout, lng, lnb, *grads)
    return outs[0], outs[1], outs[2], outs[3:]


def _a_conv_bwd(du1, h, wdw, parts, tm):
    S, D = du1.shape
    hb = tm // HALO
    last_halo = S // HALO - 1
    n_tiles = S // tm
    n = len(parts)

    def body(du_ref, dun_ref, a_ref, g_ref, ah_ref, gh_ref, wdw_ref, *refs):
        part_refs, (dag_ref, sums_ref, wsum_ref), landed_refs = refs[:n], refs[n:n + 3], refs[n + 3:2 * n + 3]
        dext_ref, ext_ref, dsh_ref, sh_ref, wacc_ref, send_sems, recv_sems = refs[2 * n + 3:]
        scatter = (part_refs, landed_refs, send_sems, recv_sems)
        i = pl.program_id(0)

        @pl.when(i == 0)
        def _():
            _exchange_start(_ChipScatter, *scatter)
            sums_ref[...] = jnp.zeros_like(sums_ref)
            wsum_ref[...] = jnp.zeros_like(wsum_ref)
            wacc_ref[...] = jnp.zeros_like(wacc_ref)
        dext_ref[0:tm, :] = du_ref[...]
        dext_ref[tm:, :] = jnp.where(i < n_tiles - 1, dun_ref[...], 0.0)
        _fill_shifts(dsh_ref, dext_ref)
        _fill_glu_ext(ext_ref, a_ref, g_ref, ah_ref, gh_ref, i > 0)
        _fill_shifts(sh_ref, ext_ref)
        back = [CONV_WIDTH - 1 - k for k in range(CONV_WIDTH)]
        fwd = [HALO - (CONV_WIDTH - 1) + k for k in range(CONV_WIDTH)]
        for cc in range(D // 128):
            lanes = slice(cc * 128, (cc + 1) * 128)
            hi_lanes = slice(D + cc * 128, D + (cc + 1) * 128)
            sa = jnp.zeros((1, 128), F32)
            sg = jnp.zeros((1, 128), F32)
            for r0 in range(0, tm, CONV_ROWS):
                acc = jnp.zeros((CONV_ROWS, 128), F32)
                for k, win in _tap_windows(dext_ref, dsh_ref, back, r0, CONV_ROWS, lanes):
                    acc = acc + wdw_ref[k:k + 1, lanes] * win
                a = a_ref[r0:r0 + CONV_ROWS, lanes]
                s = _sigmoid(g_ref[r0:r0 + CONV_ROWS, lanes])
                da = acc * s
                dg = acc * a * s * (1.0 - s)
                dag_ref[r0:r0 + CONV_ROWS, lanes] = da.astype(BF16)
                dag_ref[r0:r0 + CONV_ROWS, hi_lanes] = dg.astype(BF16)
                sa = sa + _colsum(da)
                sg = sg + _colsum(dg)
                du = du_ref[r0:r0 + CONV_ROWS, lanes]
                for k, win in _tap_windows(ext_ref, sh_ref, fwd, r0, CONV_ROWS, lanes):
                    p = du * win
                    fold = p[0:8]
                    for q in range(8, CONV_ROWS, 8):
                        fold = fold + p[q:q + 8]
                    wacc_ref[k, :, lanes] += fold
            sums_ref[0:1, lanes] += sa
            sums_ref[1:2, lanes] += sg

        @pl.when(i == n_tiles - 1)
        def _():
            for k in range(CONV_WIDTH):
                wsum_ref[k:k + 1, :] = _colsum(wacc_ref[k])
            _exchange_finish(_ChipScatter, *scatter)

    tile = lambda c: pl.BlockSpec((tm, D), lambda i, c=c: (i, c))
    halo = lambda c: pl.BlockSpec((HALO, D), lambda i, c=c: (jnp.maximum(i * hb - 1, 0), c))
    outs = pl.pallas_call(
        body, name="a_conv_bwd", grid=(n_tiles,),
        in_specs=[tile(0), pl.BlockSpec((HALO, D), lambda i: (jnp.minimum((i + 1) * hb, last_halo), 0)),
                  tile(0), tile(1), halo(0), halo(1), pl.BlockSpec((HALO, D), lambda i: (0, 0))] + _hbm_specs(n),
        out_specs=[pl.BlockSpec((tm, 2 * D), lambda i: (i, 0)), pl.BlockSpec((8, D), lambda i: (0, 0)),
                   pl.BlockSpec((HALO, D), lambda i: (0, 0))] + _hbm_specs(n),
        out_shape=[SDS((S, 2 * D), BF16), SDS((8, D), F32), SDS((HALO, D), F32)]
        + [SDS((3,) + p.shape[1:], p.dtype) for p in parts],
        scratch_shapes=[pltpu.VMEM((tm + HALO, D), F32), pltpu.VMEM((HALO + tm, D), F32),
                        pltpu.VMEM((7, HALO + tm - 8, D), F32), pltpu.VMEM((7, HALO + tm - 8, D), F32),
                        pltpu.VMEM((HALO, 8, D), F32)] + _exchange_scratch(_ChipScatter, n),
        compiler_params=_params(("arbitrary",)),
    )(du1, du1, h, h, h, h, wdw, *parts)
    return outs[0], outs[1], outs[2], outs[3:]


def _a_dx(ds1, dag, dzb, w_in, parts, tm):
    S, D = ds1.shape
    n = len(parts)
    steps = S // tm

    def body(ds_ref, dag_ref, dz_ref, w_ref, *refs):
        part_refs, o_ref, landed_refs = refs[:n], refs[n], refs[n + 1:2 * n + 1]
        scatter = (part_refs, landed_refs, *refs[2 * n + 1:])
        i = pl.program_id(0)
        pl.when(i == 0)(functools.partial(_exchange_start, _ChipScatter, *scatter))
        o_ref[...] = (ALPHA * ds_ref[...] + _dot_t(dag_ref[...], w_ref[:, 0:2 * D])
                      + _dot_t(dz_ref[...], w_ref[:, 2 * D:3 * D]))
        pl.when(i == steps - 1)(functools.partial(_exchange_finish, _ChipScatter, *scatter))

    tile = lambda w: pl.BlockSpec((tm, w), lambda i: (i, 0))
    outs = pl.pallas_call(
        body, name="a_dx", grid=(steps,),
        in_specs=[tile(D), tile(2 * D), tile(D), pl.BlockSpec(memory_space=pltpu.VMEM)] + _hbm_specs(n),
        out_specs=[tile(D)] + _hbm_specs(n),
        out_shape=[SDS((S, D), F32)] + [_ChipScatter.out_shape(p) for p in parts],
        scratch_shapes=_exchange_scratch(_ChipScatter, n),
        compiler_params=_params(("arbitrary",)),
    )(ds1, dag, dzb, w_in, *parts)
    return outs[0], outs[1:]


def _halves(w):
    return w.reshape(2, w.shape[0] // 2, w.shape[1])


def _unstack_cols(w4):
    return jnp.transpose(w4, (1, 0, 2)).reshape(w4.shape[1], N_CHIPS * w4.shape[2])


def _stack_cols(w):
    D, n = w.shape
    return jnp.transpose(w.reshape(D, N_CHIPS, n // N_CHIPS), (1, 0, 2))


def _pack_rows(rows, width, total=SMALL_ROWS):
    slab = jnp.concatenate([r.reshape(-1, width) for r in rows], axis=0)
    return jnp.pad(slab, ((0, total - slab.shape[0]), (0, 0)))


def _by_row(gathered, rows, dq4):
    return jnp.transpose(gathered.reshape(N_CHIPS, rows, dq4), (1, 0, 2))


def kernel(x, a_w_in, a_b_in, a_w_dw, a_b_dw, a_ln_g, a_ln_b, a_w_out, a_b_out, kv_w, b_w_in, b_w_out, b_b_out, post_ln_g, post_ln_b, loss_target, m_a_w_in, m_a_b_in, m_a_w_dw, m_a_b_dw, m_a_ln_g, m_a_ln_b, m_a_w_out, m_a_b_out, m_kv_w, m_b_w_in, m_b_w_out, m_b_b_out, m_post_ln_g, m_post_ln_b, v_a_w_in, v_a_b_in, v_a_w_dw, v_a_b_dw, v_a_ln_g, v_a_ln_b, v_a_w_out, v_a_b_out, v_kv_w, v_b_w_in, v_b_w_out, v_b_b_out, v_post_ln_g, v_post_ln_b):
    S, D = x.shape[1], x.shape[2]
    dq4 = D // N_CHIPS
    tm = 256
    tm_mm = min(S, 2048)
    x2 = x.reshape(S, D)
    target = loss_target.reshape(S, D)
    jchip = 2 * lax.axis_index("x") + lax.axis_index("y")

    big_local = [a_w_in[0], kv_w, b_w_in[0], a_w_out[0], b_w_out[0]]
    wire = [_halves(w.astype(BF16)) for w in big_local]
    whole = lambda g: g.reshape((N_CHIPS, 2 * g.shape[2], g.shape[3]))
    first_rows = 16
    small_first = _pack_rows([a_b_in.reshape(3, dq4)], dq4, first_rows)
    small_rest = _pack_rows([jnp.pad(a_w_dw[0], ((0, 1), (0, 0))), a_b_dw, a_ln_g, a_ln_b, a_b_out], dq4)
    gathered = _all_gather_chips([wire[0], _halves(small_first)])
    w_in4 = whole(gathered[0])
    w_in_a = _unstack_cols(w_in4)
    b_in_full = jnp.transpose(_by_row(gathered[1], first_rows, dq4)[0:3], (1, 0, 2)).reshape(1, 3 * D)
    pg0, pg1 = post_ln_g[0:1], post_ln_g[1:2]
    pb0, pb1 = post_ln_b[0:1], post_ln_b[1:2]

    h, xb, later = _a_in_proj(x2, w_in4, b_in_full, [wire[1], wire[2], wire[4], wire[3], _halves(small_rest)], tm_mm)
    kv_full = _unstack_cols(whole(later[0]))
    b_in4 = whole(later[1])
    w_out_b = later[2].reshape(D, D)
    w_out_a = later[3].reshape(D, D)
    small = _by_row(later[4], SMALL_ROWS, dq4)
    wdw_full = small[0:HALO].reshape(HALO, D)
    bdw_full, lng_full, lnb_full, bout_a_full = [small[HALO + q].reshape(1, D) for q in range(4)]
    w_z = b_in4[3]
    w_g = [jnp.concatenate([b_in4[g], kv_full[:, g * D:(g + 1) * D], kv_full[:, (3 + g) * D:(4 + g) * D]], axis=1)
           for g in range(3)]
    xhu, rsu, vb, xh1, rs1, x1b, x1p1, x1p2 = _a_conv_out(
        h, x2, wdw_full, bdw_full, lng_full, lnb_full, w_out_a, bout_a_full, pg0, pb0, tm)
    x1g = [x1b, x1p1.reshape(S, D), x1p2.reshape(S, D)]
    qkv = [_mm_nn("b_qkv_g%d" % g, x1g[g], w_g[g], BF16, tm_mm, scale_first_tile=HEAD_DIM ** -0.5)
           for g in range(3)]
    z2 = _mm_nn("b_gate_proj", x1b, w_z, F32, tm_mm)
    og, lg = zip(*[_attn_fwd(g, qkv[g], D) for g in range(3)])
    d1, d2 = DILATIONS[1], DILATIONS[2]
    (v2b, ds2, ds2b, dz2b, da0, da1, da2, ls0, ls1, ls2, dl0, dl1, dl2, loss_acc, sums_b) = _b_merge_out_loss(
        og[0], og[1].reshape(d1, S // d1, D), og[2].reshape(d2, S // d2, D),
        lg[0], lg[1].reshape(d1, S // d1, LSE_LANES), lg[2].reshape(d2, S // d2, LSE_LANES),
        z2, xh1, target, w_out_b, b_b_out, pg0, pb0, pg1, pb1, tm)

    das = [da0, da1.reshape(S, D), da2.reshape(S, D)]
    lss = [ls0, ls1.reshape(S, LSE_LANES), ls2.reshape(S, LSE_LANES)]
    dls = [dl0, dl1.reshape(S, LSE_LANES), dl2.reshape(S, LSE_LANES)]
    dq, dkv = zip(*[_attn_bwd(g, qkv[g], das[g], lss[g], dls[g], D) for g in range(3)])
    ds1, ds1b, sums_1 = _b_dx1_ln1_bwd(ds2, dz2b, dq, dkv, xh1, rs1, w_z, w_g, pg0, tm)

    def by_chip_cols(gw):
        s4 = _stack_cols(gw)
        return s4.reshape(N_CHIPS, 2, D // 2, s4.shape[2])

    def by_chip_rows(gw):
        return gw.reshape(N_CHIPS, 2, D // 8, D)

    core = lax.axis_index("c").astype(jnp.int32).reshape(1)
    chip = jchip.astype(jnp.int32).reshape(1)

    g_w_out_b = _mm_tn("dw_b_out", v2b, ds2b, tm_mm)
    g_q = [_mm_tn("dw_b_q_g%d" % g, x1g[g], dq[g], tm_mm) for g in range(3)]
    g_z = _mm_tn("dw_b_z", x1b, dz2b, tm_mm)
    g_kvg = [_mm_tn("dw_kv_g%d" % g, x1g[g], dkv[g], tm_mm) for g in range(3)]
    g_kv = jnp.concatenate([t[:, :D] for t in g_kvg] + [t[:, D:] for t in g_kvg], axis=1)
    grads_b = [by_chip_cols(g_kv), jnp.stack(g_q + [g_z]).reshape(N_CHIPS, 2, D // 2, D), by_chip_rows(g_w_out_b)]
    du1, dzab, sums_a, recvd_b = _a_gate_bwd(ds1b, h, xhu, rsu, w_out_a, lng_full, lnb_full, grads_b, tm)
    parts_b, wire_b = _pair_sum(grads_b, recvd_b, core, "b")
    dag, sums_c, wsum, landed_b = _a_conv_bwd(du1, h, wdw_full, wire_b, tm)
    own_b = _chip_sum(parts_b, landed_b, chip, "b")
    other_b = _pair_share(own_b, "b")

    g_w_in = jnp.concatenate([_mm_tn("dw_a_in_ag", xb, dag, tm_mm), _mm_tn("dw_a_in_z", xb, dzab, tm_mm)], axis=1)
    g_w_out_a = _mm_tn("dw_a_out", vb, ds1b, tm_mm)
    grads_a = [by_chip_cols(g_w_in), by_chip_rows(g_w_out_a)]
    recvd_a = _exchange(_PairExchange, "grad_pair_exchange_a", grads_a)
    parts_a, wire_a = _pair_sum(grads_a, recvd_a, core, "a")
    grad_x, landed_a = _a_dx(ds1, dag, dzab, w_in_a, wire_a, 2 * tm)
    own_a = _chip_sum(parts_a, landed_a, chip, "a")
    other_a = _pair_share(own_a, "a")
    own_half = [own_a[0], own_b[0], own_b[1], own_a[1], own_b[2]]
    other_half = [other_a[0], other_b[0], other_b[1], other_a[1], other_b[2]]

    small_grads = _pack_rows([sums_c[0:1], sums_c[1:2], sums_a[3:4], wsum, sums_a[2:3], sums_a[0:1], sums_a[1:2],
                              sums_1[2:3], sums_b[2:3], sums_1[0:1], sums_b[0:1], sums_1[1:2], sums_b[1:2]], D)
    small_sum = _sum_devices(_gather_all_devices(small_grads))
    loss = lax.psum(loss_acc[0, 0], ("x", "y", "c"))

    big_m = [m_a_w_in[0], m_kv_w, m_b_w_in[0], m_a_w_out[0], m_b_w_out[0]]
    big_v = [v_a_w_in[0], v_kv_w, v_b_w_in[0], v_a_w_out[0], v_b_w_out[0]]
    shards, big_delta, big_new_m, big_new_v = [
        [a.reshape(2 * a.shape[1], a.shape[2]) for a in group] for group in _adamw_halves(
            [_halves(w) for w in big_local], own_half, other_half, [_halves(m) for m in big_m],
            [_halves(v) for v in big_v], core)]

    def chip_cols(rows):
        return lax.dynamic_slice_in_dim(rows, jchip * dq4, dq4, axis=1)

    g_b_in = lax.dynamic_slice_in_dim(small_sum[0:3].reshape(1, 3 * D), jchip * 3 * dq4, 3 * dq4, axis=1)
    small_g = [g_b_in, chip_cols(small_sum[3:3 + CONV_WIDTH]), chip_cols(small_sum[35:36]), chip_cols(small_sum[36:37]),
               chip_cols(small_sum[37:38]), chip_cols(small_sum[38:39]), small_sum[39:40], small_sum[40:42],
               small_sum[42:44]]
    small_w = [a_b_in, a_w_dw[0], a_b_dw, a_ln_g, a_ln_b, a_b_out, b_b_out, post_ln_g, post_ln_b]
    small_m = [m_a_b_in, m_a_w_dw[0], m_a_b_dw, m_a_ln_g, m_a_ln_b, m_a_b_out, m_b_b_out, m_post_ln_g, m_post_ln_b]
    small_v = [v_a_b_in, v_a_w_dw[0], v_a_b_dw, v_a_ln_g, v_a_ln_b, v_a_b_out, v_b_b_out, v_post_ln_g, v_post_ln_b]
    small_delta, small_new_m, small_new_v = _adamw("adamw_small", small_w, small_g, small_m, small_v, 1)

    def ordered(big, sm):
        return (big[0][None], sm[0], sm[1][None], sm[2], sm[3], sm[4], big[3][None], sm[5], big[1], big[2][None],
                big[4][None], sm[6], sm[7], sm[8])

    return (loss, grad_x.reshape(1, S, D), *ordered(shards, small_g), *ordered(big_delta, small_delta),
            *ordered(big_new_m, small_new_m), *ordered(big_new_v, small_new_v))
```

```python
import functools

import numpy as np
import jax
import jax.numpy as jnp
from jax import lax
from jax.experimental import pallas as pl
from jax.experimental.pallas import tpu as pltpu

F32 = jnp.float32
BF16 = jnp.bfloat16
MESH = pl.DeviceIdType.MESH
SDS = jax.ShapeDtypeStruct

HEAD_DIM = 64
BLOCK = 128
DILATIONS = (1, 4, 16)
ALIBI_MAX_EXP = 8.0
CONV_WIDTH = 31
HALO = 32
CONV_ROWS = 128
LSE_LANES = 128
DEPTH = 2
ALPHA = (2.0 * DEPTH) ** 0.25
LN_EPS = 1e-5
ADAM_LR = 0.001
ADAM_B1 = 0.9
ADAM_B2 = 0.999
ADAM_EPS = 1e-08
ADAM_WD = 0.01
ADAM_STEP = 10
N_CHIPS = 4
N_DEV = 8
VMEM_LIMIT = 56 * 2 ** 20
SMALL_ROWS = 48


def _params(sem=None):
    return pltpu.CompilerParams(dimension_semantics=sem, vmem_limit_bytes=VMEM_LIMIT)


def _sigmoid(x):
    return 1.0 / (1.0 + jnp.exp(-x))


def _silu_grad(x, s):
    return s * (1.0 + x * (1.0 - s))


def _ln_fwd(x):
    mu = jnp.mean(x, axis=-1, keepdims=True)
    xc = x - mu
    var = jnp.mean(xc * xc, axis=-1, keepdims=True)
    rstd = lax.rsqrt(var + LN_EPS)
    return xc * rstd, rstd


def _ln_bwd(dxhat, xhat, rstd):
    m1 = jnp.mean(dxhat, axis=-1, keepdims=True)
    m2 = jnp.mean(dxhat * xhat, axis=-1, keepdims=True)
    return rstd * (dxhat - m1 - xhat * m2)


def _dot(a, b):
    return jnp.dot(a, b, preferred_element_type=F32)


def _dot_t(a, b):
    return lax.dot_general(a, b, (((1,), (1,)), ((), ())), preferred_element_type=F32)


def _tdot(a, b):
    return lax.dot_general(a, b, (((0,), (0,)), ((), ())), preferred_element_type=F32)


def _colsum(x):
    return jnp.sum(x, axis=0, keepdims=True)


def _slopes(n_heads):
    return [float(np.float32(2.0 ** (-ALIBI_MAX_EXP * (h + 1) / n_heads))) for h in range(n_heads)]


def _to_chunks(chunks_ref, x):
    for cc in range(chunks_ref.shape[0]):
        chunks_ref[cc] = x[:, cc * 128:(cc + 1) * 128]


def _deinterleave(chunks_ref, out_ref, d, dtype):
    rows = chunks_ref.shape[1] // d
    for r in range(d):
        for cc in range(chunks_ref.shape[0]):
            out_ref[r, :, cc * 128:(cc + 1) * 128] = chunks_ref[cc, pl.ds(r, rows, stride=d), :].astype(dtype)


def _interleave(in_ref, chunks_ref, d):
    rows = chunks_ref.shape[1] // d
    for r in range(d):
        for cc in range(chunks_ref.shape[0]):
            chunks_ref[cc, pl.ds(r, rows, stride=d), :] = in_ref[r, :, cc * 128:(cc + 1) * 128]


def _hbm_specs(n):
    return [pl.BlockSpec(memory_space=pl.ANY)] * n


def _position():
    x, y, c = lax.axis_index("x"), lax.axis_index("y"), lax.axis_index("c")
    return x, y, c


def _gather_stages(ins, outs, send_sems, recv_sems, local_sems):
    n = len(ins)

    def plan():
        x, y, c = _position()
        j = 2 * x + y
        me, sibling = (x, y, c), (x, y, 1 - c)
        chips = [(1 - x, y), (x, 1 - y), (1 - x, 1 - y)]

        def copy(i, k, src, dst, to):
            return pltpu.make_async_remote_copy(
                src_ref=src, dst_ref=dst, send_sem=send_sems.at[i, k], recv_sem=recv_sems.at[i, k],
                device_id=to, device_id_type=MESH)

        local = [pltpu.make_async_copy(ins[i], outs[i].at[j], local_sems.at[i]) for i in range(n)]
        first, landing, passed, passed_landing = [], [], [], []
        for i in range(n):
            for k, chip in enumerate(chips):
                pj = 2 * chip[0] + chip[1]
                first.append(copy(i, k, ins[i].at[c], outs[i].at[j, c], (*chip, c)))
                landing.append(copy(i, k, ins[i].at[c], outs[i].at[pj, c], me))
                passed.append(copy(i, 3 + k, outs[i].at[pj, c], outs[i].at[pj, c], sibling))
                passed_landing.append(copy(i, 3 + k, ins[i].at[c], outs[i].at[pj, 1 - c], me))
        return local, first, landing, passed, passed_landing

    def start():
        local, first, _, _, _ = plan()
        for cp in local + first:
            cp.start()

    def forward():
        _, _, landing, passed, _ = plan()
        for arrived, cp in zip(landing, passed):
            arrived.wait_recv()
            cp.start()

    def finish():
        local, first, _, passed, passed_landing = plan()
        for cp in passed_landing:
            cp.wait_recv()
        for cp in first + passed:
            cp.wait_send()
        for cp in local:
            cp.wait()

    return start, forward, finish


def _gather_scratch(n):
    return [pltpu.SemaphoreType.DMA((n, 6)), pltpu.SemaphoreType.DMA((n, 6)), pltpu.SemaphoreType.DMA((n,))]


def _all_gather_chips(shards):
    n = len(shards)

    def body(*refs):
        for stage in _gather_stages(refs[:n], refs[n:2 * n], *refs[2 * n:]):
            stage()

    return pl.pallas_call(
        body, name="all_gather_chips",
        out_shape=[SDS((N_CHIPS,) + s.shape, s.dtype) for s in shards],
        in_specs=_hbm_specs(n), out_specs=_hbm_specs(n), scratch_shapes=_gather_scratch(n),
    )(*shards)


class _PairExchange:
    slots = N_CHIPS

    @staticmethod
    def out_shape(g):
        return SDS((N_CHIPS,) + g.shape[2:], g.dtype)

    @staticmethod
    def copies(ins, outs, send_sems, recv_sems):
        x, y, c = _position()
        return [pltpu.make_async_remote_copy(
            src_ref=ins[i].at[j, 1 - c], dst_ref=outs[i].at[j],
            send_sem=send_sems.at[i, j], recv_sem=recv_sems.at[i, j],
            device_id=(x, y, 1 - c), device_id_type=MESH) for i in range(len(ins)) for j in range(N_CHIPS)]


class _ChipScatter:
    slots = 3

    @staticmethod
    def out_shape(p):
        return SDS((3,) + p.shape[1:], p.dtype)

    @staticmethod
    def copies(ins, outs, send_sems, recv_sems):
        x, y, c = _position()
        chips = [(1 - x, y), (x, 1 - y), (1 - x, 1 - y)]
        return [pltpu.make_async_remote_copy(
            src_ref=ins[i].at[2 * chip[0] + chip[1]], dst_ref=outs[i].at[k],
            send_sem=send_sems.at[i, k], recv_sem=recv_sems.at[i, k],
            device_id=(*chip, c), device_id_type=MESH) for i in range(len(ins)) for k, chip in enumerate(chips)]


def _exchange_start(kind, *refs):
    for cp in kind.copies(*refs):
        cp.start()


def _exchange_finish(kind, *refs):
    copies = kind.copies(*refs)
    for cp in copies:
        cp.wait_recv()
    for cp in copies:
        cp.wait_send()


def _exchange_scratch(kind, n):
    return [pltpu.SemaphoreType.DMA((n, kind.slots)), pltpu.SemaphoreType.DMA((n, kind.slots))]


def _exchange(kind, name, arrays):
    n = len(arrays)

    def body(*refs):
        args = (refs[:n], refs[n:2 * n], *refs[2 * n:])
        _exchange_start(kind, *args)
        _exchange_finish(kind, *args)

    return pl.pallas_call(
        body, name=name, out_shape=[kind.out_shape(a) for a in arrays],
        in_specs=_hbm_specs(n), out_specs=_hbm_specs(n), scratch_shapes=_exchange_scratch(kind, n),
    )(*arrays)


def _pair_share(halves, tag):
    n = len(halves)

    def body(*refs):
        ins, outs = refs[:n], refs[n:2 * n]
        send_sems, recv_sems = refs[2 * n:]
        x, y, c = _position()
        remote = [pltpu.make_async_remote_copy(
            src_ref=ins[i], dst_ref=outs[i], send_sem=send_sems.at[i], recv_sem=recv_sems.at[i],
            device_id=(x, y, 1 - c), device_id_type=MESH) for i in range(n)]
        for cp in remote:
            cp.start()
        for cp in remote:
            cp.wait_recv()
        for cp in remote:
            cp.wait_send()

    return pl.pallas_call(
        body, name="grad_pair_share_" + tag,
        out_shape=[SDS(h.shape, h.dtype) for h in halves],
        in_specs=_hbm_specs(n), out_specs=_hbm_specs(n),
        scratch_shapes=[pltpu.SemaphoreType.DMA((n,)), pltpu.SemaphoreType.DMA((n,))],
    )(*halves)


def _gather_all_plan(in_ref, out_ref, send_sems, recv_sems, local_sem):
    x, y, c = _position()
    me = 4 * x + 2 * y + c
    local = pltpu.make_async_copy(in_ref, out_ref.at[me], local_sem)
    remote, landing = [], []
    for mask in range(1, N_DEV):
        px, py, pc = x ^ (mask >> 2), y ^ ((mask >> 1) & 1), c ^ (mask & 1)
        peer = 4 * px + 2 * py + pc
        remote.append(pltpu.make_async_remote_copy(
            src_ref=in_ref, dst_ref=out_ref.at[me], send_sem=send_sems.at[mask - 1],
            recv_sem=recv_sems.at[mask - 1], device_id=(px, py, pc), device_id_type=MESH))
        landing.append(pltpu.make_async_remote_copy(
            src_ref=in_ref, dst_ref=out_ref.at[peer], send_sem=send_sems.at[mask - 1],
            recv_sem=recv_sems.at[mask - 1], device_id=(px, py, pc), device_id_type=MESH))
    return local, remote, landing


def _gather_all_start(*refs):
    local, remote, _ = _gather_all_plan(*refs)
    local.start()
    for cp in remote:
        cp.start()


def _gather_all_finish(*refs):
    local, remote, landing = _gather_all_plan(*refs)
    for cp in landing:
        cp.wait_recv()
    for cp in remote:
        cp.wait_send()
    local.wait()


def _gather_all_scratch():
    return [pltpu.SemaphoreType.DMA((N_DEV - 1,)), pltpu.SemaphoreType.DMA((N_DEV - 1,)), pltpu.SemaphoreType.DMA]


def _row_splits(arrays):
    return min(a.shape[-2] for a in arrays) // 16


def _pair_sum(grads, recvd, core, tag):
    n = len(grads)
    splits = _row_splits(recvd)

    def body(core_ref, *refs):
        for i in range(n):
            s = refs[i][...] + refs[n + i][...]
            refs[2 * n + i][...] = s
            refs[3 * n + i][...] = s.astype(BF16)

    mine = [pl.BlockSpec((N_CHIPS, None, r.shape[1] // splits, r.shape[2]), lambda s, core: (0, core[0], s, 0))
            for r in recvd]
    block = [pl.BlockSpec((N_CHIPS, r.shape[1] // splits, r.shape[2]), lambda s, core: (0, s, 0)) for r in recvd]
    outs = pl.pallas_call(
        body, name="grad_pair_sum_" + tag,
        grid_spec=pltpu.PrefetchScalarGridSpec(
            num_scalar_prefetch=1, grid=(splits,), in_specs=mine + block, out_specs=block + block),
        out_shape=[SDS(r.shape, F32) for r in recvd] + [SDS(r.shape, BF16) for r in recvd],
        compiler_params=_params(("parallel",)),
    )(core, *grads, *recvd)
    return outs[:n], outs[n:]


def _chip_sum(parts, landed, chip, tag):
    n = len(parts)
    splits = _row_splits(landed)

    def body(chip_ref, *refs):
        for i in range(n):
            acc = refs[i][...]
            for k in range(3):
                acc = acc + refs[n + i][k].astype(F32)
            refs[2 * n + i][...] = acc

    rows = lambda p: p.shape[1] // splits
    return pl.pallas_call(
        body, name="grad_chip_sum_" + tag,
        grid_spec=pltpu.PrefetchScalarGridSpec(
            num_scalar_prefetch=1, grid=(splits,),
            in_specs=[pl.BlockSpec((None, rows(p), p.shape[2]), lambda s, chip: (chip[0], s, 0)) for p in parts]
            + [pl.BlockSpec((3, rows(p), p.shape[2]), lambda s, chip: (0, s, 0)) for p in parts],
            out_specs=[pl.BlockSpec((rows(p), p.shape[2]), lambda s, chip: (s, 0)) for p in parts]),
        out_shape=[SDS(p.shape[1:], F32) for p in parts],
        compiler_params=_params(("parallel",)),
    )(chip, *parts, *landed)


def _adamw_math(w, g, m, v):
    m = ADAM_B1 * m + (1.0 - ADAM_B1) * g
    v = ADAM_B2 * v + (1.0 - ADAM_B2) * (g * g)
    m_hat = m / (1.0 - ADAM_B1 ** ADAM_STEP)
    v_hat = v / (1.0 - ADAM_B2 ** ADAM_STEP)
    delta = -ADAM_LR * (m_hat / (jnp.sqrt(v_hat) + ADAM_EPS) + ADAM_WD * w)
    return delta, m, v


def _adamw(name, ws, gs, ms, vs, splits):
    n = len(ws)

    def body(*refs):
        for i in range(n):
            w, g, m, v = (refs[q * n + i][...] for q in range(4))
            delta, m, v = _adamw_math(w, g, m, v)
            refs[4 * n + i][...] = delta
            refs[5 * n + i][...] = m
            refs[6 * n + i][...] = v

    def spec(a):
        if splits == 1:
            return pl.BlockSpec(a.shape, lambda s: (0, 0))
        return pl.BlockSpec((a.shape[0] // splits, a.shape[1]), lambda s: (s, 0))

    specs = [spec(a) for a in ws]
    outs = pl.pallas_call(
        body, name=name, grid=(splits,),
        in_specs=specs * 4, out_specs=specs * 3,
        out_shape=[SDS(a.shape, F32) for a in ws] * 3,
        compiler_params=_params(("parallel",)),
    )(*ws, *gs, *ms, *vs)
    return outs[:n], outs[n:2 * n], outs[2 * n:]


def _adamw_halves(ws, own, other, ms, vs, core):
    n = len(ws)
    splits = _row_splits(own)

    def body(core_ref, *refs):
        mine = pl.program_id(0) == core_ref[0]
        for i in range(n):
            g = jnp.where(mine, refs[n + i][...], refs[2 * n + i][...])
            delta, m, v = _adamw_math(refs[i][...], g, refs[3 * n + i][...], refs[4 * n + i][...])
            refs[5 * n + i][...] = g
            refs[6 * n + i][...] = delta
            refs[7 * n + i][...] = m
            refs[8 * n + i][...] = v

    rows = lambda a: a.shape[0] // splits
    half = [pl.BlockSpec((None, rows(a), a.shape[1]), lambda hh, s, core: (hh, s, 0)) for a in own]
    flat = [pl.BlockSpec((rows(a), a.shape[1]), lambda hh, s, core: (s, 0)) for a in own]
    outs = pl.pallas_call(
        body, name="adamw_big",
        grid_spec=pltpu.PrefetchScalarGridSpec(
            num_scalar_prefetch=1, grid=(2, splits), in_specs=half + flat + flat + half + half, out_specs=half * 4),
        out_shape=[SDS(w.shape, F32) for w in ws] * 4,
        compiler_params=_params(("parallel", "parallel")),
    )(core, *ws, *own, *other, *ms, *vs)
    return outs[:n], outs[n:2 * n], outs[2 * n:3 * n], outs[3 * n:]


def _sum_devices(slabs):
    def body(in_ref, out_ref):
        acc = in_ref[0]
        for k in range(1, N_DEV):
            acc = acc + in_ref[k]
        out_ref[...] = acc

    return pl.pallas_call(
        body, name="small_grad_sum", out_shape=SDS(slabs.shape[1:], F32),
    )(slabs)


def _mm_nn(name, a, w, out_dtype, tm, scale_first_tile=None):
    S, K = a.shape
    N = w.shape[1]
    tn = K

    def body(a_ref, w_ref, o_ref):
        acc = _dot(a_ref[...], w_ref[...])
        if scale_first_tile is not None:
            acc = acc * jnp.where(pl.program_id(1) == 0, scale_first_tile, 1.0)
        o_ref[...] = acc.astype(out_dtype)

    return pl.pallas_call(
        body, name=name, grid=(S // tm, N // tn),
        in_specs=[pl.BlockSpec((tm, K), lambda i, t: (i, 0)), pl.BlockSpec((K, tn), lambda i, t: (0, t))],
        out_specs=pl.BlockSpec((tm, tn), lambda i, t: (i, t)),
        out_shape=SDS((S, N), out_dtype),
        compiler_params=_params(("parallel", "arbitrary")),
    )(a, w)


def _mm_tn(name, a, b, tk, slab=None):
    S, M = a.shape
    N = b.shape[1]
    tn = M
    nt, nk = N // tn, S // tk

    def body(a_ref, b_ref, *refs):
        o_ref = refs[1] if slab is not None else refs[0]
        gather = (refs[0], *refs[2:]) if slab is not None else None
        t, k = pl.program_id(0), pl.program_id(1)
        if gather is not None:
            pl.when((t == 0) & (k == 0))(functools.partial(_gather_all_start, *gather))

        @pl.when(k == 0)
        def _():
            o_ref[...] = jnp.zeros_like(o_ref)
        o_ref[...] += _tdot(a_ref[...], b_ref[...])
        if gather is not None:
            pl.when((t == nt - 1) & (k == nk - 1))(functools.partial(_gather_all_finish, *gather))

    tiles = [pl.BlockSpec((tk, M), lambda t, k: (k, 0)), pl.BlockSpec((tk, tn), lambda t, k: (k, t))]
    out_tile = pl.BlockSpec((M, tn), lambda t, k: (0, t))
    if slab is None:
        return pl.pallas_call(
            body, name=name, grid=(nt, nk), in_specs=tiles, out_specs=out_tile, out_shape=SDS((M, N), F32),
            compiler_params=_params(("parallel", "arbitrary")),
        )(a, b)
    return pl.pallas_call(
        body, name=name, grid=(nt, nk), in_specs=tiles + _hbm_specs(1),
        out_specs=[out_tile, pl.BlockSpec(memory_space=pl.ANY)],
        out_shape=[SDS((M, N), F32), SDS((N_DEV,) + slab.shape, slab.dtype)],
        scratch_shapes=_gather_all_scratch(),
        compiler_params=_params(("arbitrary", "arbitrary")),
    )(a, b, slab)


def _a_in_proj(x, w4, b_in, later_shards, tm):
    S, D = x.shape
    nj = w4.shape[2]
    n = len(later_shards)
    steps = S // tm

    def body(x_ref, w_ref, b_ref, *refs):
        shard_refs, (h_ref, xb_ref), gathered_refs = refs[:n], refs[n:n + 2], refs[n + 2:2 * n + 2]
        start, forward, finish = _gather_stages(shard_refs, gathered_refs, *refs[2 * n + 2:])
        i, t = pl.program_id(0), pl.program_id(1)
        pl.when((i == 0) & (t == 0))(start)
        pl.when((i == steps // 2) & (t == 0))(forward)
        xb = x_ref[...].astype(BF16)

        @pl.when(t == 0)
        def _():
            xb_ref[...] = xb
        h_ref[...] = _dot(xb, w_ref[...]) + b_ref[...]
        pl.when((i == steps - 1) & (t == N_CHIPS - 1))(finish)

    outs = pl.pallas_call(
        body, name="a_in_proj", grid=(steps, N_CHIPS),
        in_specs=[pl.BlockSpec((tm, D), lambda i, t: (i, 0)),
                  pl.BlockSpec((None, D, nj), lambda i, t: (t, 0, 0)),
                  pl.BlockSpec((1, nj), lambda i, t: (0, t))] + _hbm_specs(n),
        out_specs=[pl.BlockSpec((tm, nj), lambda i, t: (i, t)), pl.BlockSpec((tm, D), lambda i, t: (i, 0))]
        + _hbm_specs(n),
        out_shape=[SDS((S, N_CHIPS * nj), F32), SDS((S, D), BF16)]
        + [SDS((N_CHIPS,) + s.shape, s.dtype) for s in later_shards],
        scratch_shapes=_gather_scratch(n),
        compiler_params=_params(("arbitrary", "arbitrary")),
    )(x, w4, b_in, *later_shards)
    return outs[0], outs[1], outs[2:]


def _fill_glu_ext(ext_ref, a_ref, g_ref, ah_ref, gh_ref, has_prev):
    u0h = ah_ref[...] * _sigmoid(gh_ref[...])
    ext_ref[0:HALO, :] = jnp.where(has_prev, u0h, 0.0)
    ext_ref[HALO:, :] = a_ref[...] * _sigmoid(g_ref[...])


def _fill_shifts(shift_ref, ext_ref):
    for s in range(1, 8):
        shift_ref[s - 1] = ext_ref[s:s + shift_ref.shape[1], :]


def _tap_windows(ext_ref, shift_ref, starts, r0, rows, lanes):
    for s in range(8):
        taps = [(k, st) for k, st in enumerate(starts) if st % 8 == s]
        if not taps:
            continue
        lo = min(st for _, st in taps)
        hi = max(st for _, st in taps)
        if s == 0:
            win = ext_ref[r0 + lo:r0 + hi + rows, lanes]
        else:
            win = shift_ref[s - 1, r0 + lo - s:r0 + hi - s + rows, lanes]
        for k, st in taps:
            yield k, win[st - lo:st - lo + rows]


def _a_conv_out(h, x, wdw, bdw, lng, lnb, wout, bout, pg, pb, tm):
    S, D = x.shape
    hb = tm // HALO
    d1, d2 = DILATIONS[1], DILATIONS[2]

    def body(a_ref, g_ref, z_ref, ah_ref, gh_ref, x_ref, wdw_ref, bdw_ref, lng_ref, lnb_ref, wout_ref,
             bout_ref, pg_ref, pb_ref, xhu_ref, rsu_ref, vb_ref, xh1_ref, rs1_ref, x1b_ref, x1p1_ref,
             x1p2_ref, ext_ref, u1_ref, x1_ref, sh_ref):
        i = pl.program_id(0)
        _fill_glu_ext(ext_ref, a_ref, g_ref, ah_ref, gh_ref, i > 0)
        _fill_shifts(sh_ref, ext_ref)
        starts = [HALO - (CONV_WIDTH - 1) + k for k in range(CONV_WIDTH)]
        for cc in range(D // 128):
            lanes = slice(cc * 128, (cc + 1) * 128)
            for r0 in range(0, tm, CONV_ROWS):
                acc = jnp.broadcast_to(bdw_ref[:, lanes], (CONV_ROWS, 128))
                for k, win in _tap_windows(ext_ref, sh_ref, starts, r0, CONV_ROWS, lanes):
                    acc = acc + wdw_ref[k:k + 1, lanes] * win
                u1_ref[r0:r0 + CONV_ROWS, lanes] = acc
        xhu, rsu = _ln_fwd(u1_ref[...])
        xhu_ref[...] = xhu
        rsu_ref[...] = rsu
        u2 = xhu * lng_ref[...] + lnb_ref[...]
        z = z_ref[...]
        v = (u2 * _sigmoid(u2)) * (z * _sigmoid(z))
        vb = v.astype(BF16)
        vb_ref[...] = vb
        s1 = ALPHA * x_ref[...] + _dot(vb, wout_ref[...]) + bout_ref[...]
        xh1, rs1 = _ln_fwd(s1)
        xh1_ref[...] = xh1
        rs1_ref[...] = rs1
        x1 = xh1 * pg_ref[...] + pb_ref[...]
        x1b_ref[...] = x1.astype(BF16)
        _to_chunks(x1_ref, x1)
        _deinterleave(x1_ref, x1p1_ref, d1, BF16)
        _deinterleave(x1_ref, x1p2_ref, d2, BF16)

    tile = lambda c: pl.BlockSpec((tm, D), lambda i, c=c: (i, c))
    halo = lambda c: pl.BlockSpec((HALO, D), lambda i, c=c: (jnp.maximum(i * hb - 1, 0), c))
    row = pl.BlockSpec((1, D), lambda i: (0, 0))
    stat = pl.BlockSpec((tm, 1), lambda i: (i, 0))
    return pl.pallas_call(
        body, name="a_conv_out", grid=(S // tm,),
        in_specs=[tile(0), tile(1), tile(2), halo(0), halo(1), tile(0),
                  pl.BlockSpec((HALO, D), lambda i: (0, 0)), row, row, row,
                  pl.BlockSpec((D, D), lambda i: (0, 0)), row, row, row],
        out_specs=[tile(0), stat, tile(0), tile(0), stat, tile(0),
                   pl.BlockSpec((d1, tm // d1, D), lambda i: (0, i, 0)),
                   pl.BlockSpec((d2, tm // d2, D), lambda i: (0, i, 0))],
        out_shape=[SDS((S, D), F32), SDS((S, 1), F32), SDS((S, D), BF16), SDS((S, D), F32), SDS((S, 1), F32),
                   SDS((S, D), BF16), SDS((d1, S // d1, D), BF16), SDS((d2, S // d2, D), BF16)],
        scratch_shapes=[pltpu.VMEM((HALO + tm, D), F32), pltpu.VMEM((tm, D), F32),
                        pltpu.VMEM((D // 128, tm, 128), F32), pltpu.VMEM((7, HALO + tm - 8, D), F32)],
        compiler_params=_params(("parallel",)),
    )(h, h, h, h, h, x, wdw, bdw, lng, lnb, wout, bout, pg, pb)


def _band(n, dilation):
    qi = lax.broadcasted_iota(jnp.int32, (BLOCK, 2 * BLOCK), 0)
    kj = lax.broadcasted_iota(jnp.int32, (BLOCK, 2 * BLOCK), 1)
    dist = qi + BLOCK - kj
    valid = (dist >= 0) & (dist <= BLOCK) & ((n > 0) | (kj >= BLOCK))
    return jnp.where(valid, dist.astype(F32) * float(-dilation), -jnp.inf)


def _attn_fwd(g, qkv, D):
    S = qkv.shape[0]
    d = DILATIONS[g]
    nb = S // (d * BLOCK)
    H = D // HEAD_DIM
    slopes = _slopes(H)

    def body(q_ref, kp_ref, kc_ref, vp_ref, vc_ref, o_ref, lse_ref):
        neg_dist = _band(pl.program_id(1), d)
        lane = lax.broadcasted_iota(jnp.int32, (BLOCK, LSE_LANES), 1)
        low = lane < HEAD_DIM
        lse = jnp.zeros((BLOCK, LSE_LANES), F32)
        for hp in range(H // 2):
            sl = slice(hp * 128, (hp + 1) * 128)
            q = q_ref[:, sl]
            k = jnp.concatenate([kp_ref[:, sl], kc_ref[:, sl]], axis=0)
            v = jnp.concatenate([vp_ref[:, sl], vc_ref[:, sl]], axis=0)
            o = []
            for a in range(2):
                h = 2 * hp + a
                s = _dot_t(jnp.where(low if a == 0 else ~low, q, jnp.zeros_like(q)), k)
                s = s + slopes[h] * neg_dist
                m = jnp.max(s, axis=1, keepdims=True)
                p = jnp.exp(s - m)
                l = jnp.sum(p, axis=1, keepdims=True)
                o.append(_dot(p.astype(BF16), v) * (1.0 / l))
                lse = jnp.where(lane == h, m + jnp.log(l), lse)
            o_ref[:, sl] = jnp.where(low, o[0], o[1])
        lse_ref[...] = lse

    cur = lambda c: pl.BlockSpec((BLOCK, D), lambda r, n, c=c: (r * nb + n, c))
    prev = lambda c: pl.BlockSpec((BLOCK, D), lambda r, n, c=c: (r * nb + jnp.maximum(n - 1, 0), c))
    return pl.pallas_call(
        body, name="attn_fwd_g%d" % g, grid=(d, nb),
        in_specs=[cur(0), prev(1), cur(1), prev(2), cur(2)],
        out_specs=[cur(0), pl.BlockSpec((BLOCK, LSE_LANES), lambda r, n: (r * nb + n, 0))],
        out_shape=[SDS((S, D), F32), SDS((S, LSE_LANES), F32)],
        compiler_params=_params(("parallel", "parallel")),
    )(qkv, qkv, qkv, qkv, qkv)


def _attn_bwd(g, qkv, do, lse, delta, D):
    S = qkv.shape[0]
    d = DILATIONS[g]
    nb = S // (d * BLOCK)
    H = D // HEAD_DIM
    slopes = _slopes(H)

    def body(q_ref, kp_ref, kc_ref, vp_ref, vc_ref, do_ref, lse_ref, dl_ref, dq_ref, dkv_ref, ck_ref, cv_ref):
        n = pl.program_id(1)

        @pl.when(n == 0)
        def _():
            ck_ref[...] = jnp.zeros_like(ck_ref)
            cv_ref[...] = jnp.zeros_like(cv_ref)

        @pl.when(n < nb)
        def _():
            neg_dist = _band(n, d)
            low = lax.broadcasted_iota(jnp.int32, (BLOCK, 128), 1) < HEAD_DIM
            low2 = lax.broadcasted_iota(jnp.int32, (2 * BLOCK, 128), 1) < HEAD_DIM
            for hp in range(H // 2):
                sl = slice(hp * 128, (hp + 1) * 128)
                q = q_ref[:, sl]
                do2 = do_ref[:, sl]
                k = jnp.concatenate([kp_ref[:, sl], kc_ref[:, sl]], axis=0)
                v = jnp.concatenate([vp_ref[:, sl], vc_ref[:, sl]], axis=0)
                zero = jnp.zeros_like(q)
                q_do = jnp.concatenate([jnp.concatenate([q, zero], axis=1),
                                        jnp.concatenate([zero, do2], axis=1)], axis=0)
                dq, dkv = [], []
                for a in range(2):
                    h = 2 * hp + a
                    keep = low if a == 0 else ~low
                    s = _dot_t(jnp.where(keep, q, zero), k)
                    s = s + slopes[h] * neg_dist
                    p = jnp.exp(s - lse_ref[:, h:h + 1])
                    dp = _dot_t(jnp.where(keep, do2, zero), v)
                    dsb = (p * (dp - dl_ref[:, h:h + 1])).astype(BF16)
                    dq.append(_dot(dsb, k))
                    dkv.append(_tdot(jnp.concatenate([dsb, p.astype(BF16)], axis=0), q_do))
                dq_ref[:, sl] = (jnp.where(low, dq[0], dq[1]) * (HEAD_DIM ** -0.5)).astype(BF16)
                dk2 = jnp.where(low2, dkv[0][:, :128], dkv[1][:, :128])
                dv2 = jnp.where(low2, dkv[0][:, 128:], dkv[1][:, 128:])
                dkv_ref[:, sl] = (ck_ref[:, sl] + dk2[:BLOCK]).astype(BF16)
                dkv_ref[:, D + hp * 128:D + (hp + 1) * 128] = (cv_ref[:, sl] + dv2[:BLOCK]).astype(BF16)
                ck_ref[:, sl] = dk2[BLOCK:]
                cv_ref[:, sl] = dv2[BLOCK:]

        @pl.when(n == nb)
        def _():
            dkv_ref[:, :D] = ck_ref[...].astype(BF16)
            dkv_ref[:, D:] = cv_ref[...].astype(BF16)

    nq = lambda n: jnp.minimum(n, nb - 1)
    cur = lambda c: pl.BlockSpec((BLOCK, D), lambda r, n, c=c: (r * nb + nq(n), c))
    prev = lambda c: pl.BlockSpec((BLOCK, D), lambda r, n, c=c: (r * nb + jnp.maximum(nq(n) - 1, 0), c))
    stat = pl.BlockSpec((BLOCK, LSE_LANES), lambda r, n: (r * nb + nq(n), 0))
    return pl.pallas_call(
        body, name="attn_bwd_g%d" % g, grid=(d, nb + 1),
        in_specs=[cur(0), prev(1), cur(1), prev(2), cur(2), cur(0), stat, stat],
        out_specs=[cur(0), pl.BlockSpec((BLOCK, 2 * D), lambda r, n: (r * nb + jnp.maximum(n - 1, 0), 0))],
        out_shape=[SDS((S, D), BF16), SDS((S, 2 * D), BF16)],
        scratch_shapes=[pltpu.VMEM((BLOCK, D), F32), pltpu.VMEM((BLOCK, D), F32)],
        compiler_params=_params(("parallel", "arbitrary")),
    )(qkv, qkv, qkv, qkv, qkv, do, lse, delta)


def _b_merge_out_loss(o0, o1, o2, l0, l1, l2, z2, xh1, target, wbo, bbo, pg0, pb0, pg1, pb1, tm):
    S, D = o0.shape
    H = D // HEAD_DIM
    d1, d2 = DILATIONS[1], DILATIONS[2]
    inv_d = 1.0 / D

    def body(o0_ref, o1_ref, o2_ref, l0_ref, l1_ref, l2_ref, z_ref, xh1_ref, t_ref, wbo_ref, bbo_ref,
             pg0_ref, pb0_ref, pg1_ref, pb1_ref,
             v2b_ref, ds2_ref, ds2b_ref, dz2b_ref, da0_ref, da1_ref, da2_ref, ls0_ref, ls1_ref, ls2_ref,
             dl0_ref, dl1_ref, dl2_ref, loss_ref, sums_ref,
             o1n_ref, o2n_ref, l1n_ref, l2n_ref, att_ref):
        i = pl.program_id(0)
        _interleave(o1_ref, o1n_ref, d1)
        _interleave(o2_ref, o2n_ref, d2)
        for r in range(d1):
            l1n_ref[pl.ds(r, tm // d1, stride=d1), :] = l1_ref[r]
        for r in range(d2):
            l2n_ref[pl.ds(r, tm // d2, stride=d2), :] = l2_ref[r]
        la, lb, lc = l0_ref[...], l1n_ref[...], l2n_ref[...]
        m = jnp.maximum(jnp.maximum(la, lb), lc)
        ea, eb, ec = jnp.exp(la - m), jnp.exp(lb - m), jnp.exp(lc - m)
        den = ea + eb + ec
        wa, wb, wc = ea / den, eb / den, ec / den
        ls0_ref[...] = m + jnp.log(den)
        for h in range(H):
            sl = slice(h * HEAD_DIM, (h + 1) * HEAD_DIM)
            cc, hl = divmod(h * HEAD_DIM, 128)
            att_ref[:, sl] = (wa[:, h:h + 1] * o0_ref[:, sl] + wb[:, h:h + 1] * o1n_ref[cc, :, hl:hl + HEAD_DIM]
                              + wc[:, h:h + 1] * o2n_ref[cc, :, hl:hl + HEAD_DIM])
        att = att_ref[...]
        z = z_ref[...]
        sz = _sigmoid(z)
        gate = z * sz
        v2b = (att * gate).astype(BF16)
        v2b_ref[...] = v2b
        x1 = xh1_ref[...] * pg0_ref[...] + pb0_ref[...]
        s2 = ALPHA * x1 + _dot(v2b, wbo_ref[...]) + bbo_ref[...]
        xh2, rs2 = _ln_fwd(s2)
        err = xh2 * pg1_ref[...] + pb1_ref[...] - t_ref[...]
        dy = err * inv_d
        ds2 = _ln_bwd(dy * pg1_ref[...], xh2, rs2)
        ds2b = ds2.astype(BF16)
        ds2_ref[...] = ds2
        ds2b_ref[...] = ds2b

        @pl.when(i == 0)
        def _():
            loss_ref[...] = jnp.zeros_like(loss_ref)
            sums_ref[...] = jnp.zeros_like(sums_ref)
        loss_ref[...] += 0.5 * inv_d * jnp.sum(err * err)
        sums_ref[0:1, :] += _colsum(dy * xh2)
        sums_ref[1:2, :] += _colsum(dy)
        sums_ref[2:3, :] += _colsum(ds2)

        dv2 = _dot_t(ds2b, wbo_ref[...])
        datt = dv2 * gate
        dz2b_ref[...] = (dv2 * att * _silu_grad(z, sz)).astype(BF16)
        prod = datt * att
        lane = lax.broadcasted_iota(jnp.int32, (tm, LSE_LANES), 1)
        dl = jnp.zeros((tm, LSE_LANES), F32)
        for h in range(H):
            sl = slice(h * HEAD_DIM, (h + 1) * HEAD_DIM)
            dl = jnp.where(lane == h, jnp.sum(prod[:, sl], axis=1, keepdims=True), dl)
        da0_ref[...] = datt.astype(BF16)
        dl0_ref[...] = dl
        _to_chunks(o1n_ref, datt)
        _deinterleave(o1n_ref, da1_ref, d1, BF16)
        _deinterleave(o1n_ref, da2_ref, d2, BF16)
        for r in range(d1):
            ls1_ref[r] = ls0_ref[pl.ds(r, tm // d1, stride=d1), :]
            dl1_ref[r] = dl0_ref[pl.ds(r, tm // d1, stride=d1), :]
        for r in range(d2):
            ls2_ref[r] = ls0_ref[pl.ds(r, tm // d2, stride=d2), :]
            dl2_ref[r] = dl0_ref[pl.ds(r, tm // d2, stride=d2), :]

    tile = pl.BlockSpec((tm, D), lambda i: (i, 0))
    stat = pl.BlockSpec((tm, LSE_LANES), lambda i: (i, 0))
    perm = lambda d, w: pl.BlockSpec((d, tm // d, w), lambda i: (0, i, 0))
    row = pl.BlockSpec((1, D), lambda i: (0, 0))
    acc = lambda w: pl.BlockSpec((8, w), lambda i: (0, 0))
    pshape = lambda d, w, dt: SDS((d, S // d, w), dt)
    return pl.pallas_call(
        body, name="b_merge_out_loss", grid=(S // tm,),
        in_specs=[tile, perm(d1, D), perm(d2, D), stat, perm(d1, LSE_LANES), perm(d2, LSE_LANES),
                  tile, tile, tile, pl.BlockSpec((D, D), lambda i: (0, 0)), row, row, row, row, row],
        out_specs=[tile, tile, tile, tile, tile, perm(d1, D), perm(d2, D),
                   stat, perm(d1, LSE_LANES), perm(d2, LSE_LANES),
                   stat, perm(d1, LSE_LANES), perm(d2, LSE_LANES), acc(LSE_LANES), acc(D)],
        out_shape=[SDS((S, D), BF16), SDS((S, D), F32), SDS((S, D), BF16), SDS((S, D), BF16),
                   SDS((S, D), BF16), pshape(d1, D, BF16), pshape(d2, D, BF16),
                   SDS((S, LSE_LANES), F32), pshape(d1, LSE_LANES, F32), pshape(d2, LSE_LANES, F32),
                   SDS((S, LSE_LANES), F32), pshape(d1, LSE_LANES, F32), pshape(d2, LSE_LANES, F32),
                   SDS((8, LSE_LANES), F32), SDS((8, D), F32)],
        scratch_shapes=[pltpu.VMEM((D // 128, tm, 128), F32), pltpu.VMEM((D // 128, tm, 128), F32),
                        pltpu.VMEM((tm, LSE_LANES), F32), pltpu.VMEM((tm, LSE_LANES), F32),
                        pltpu.VMEM((tm, D), F32)],
        compiler_params=_params(("arbitrary",)),
    )(o0, o1, o2, l0, l1, l2, z2, xh1, target, wbo, bbo, pg0, pb0, pg1, pb1)


def _b_dx1_ln1_bwd(ds2, dz2b, dq, dkv, xh1, rs1, wz, wg, pg0, tm):
    S, D = ds2.shape
    d1, d2 = DILATIONS[1], DILATIONS[2]

    def group_part(dq_blk, dkv_blk, w_ref):
        return (_dot_t(dq_blk, w_ref[:, 0:D]) + _dot_t(dkv_blk[:, 0:D], w_ref[:, D:2 * D])
                + _dot_t(dkv_blk[:, D:2 * D], w_ref[:, 2 * D:3 * D]))

    def body(ds2_ref, dz_ref, dq0_ref, dkv0_ref, dq1_ref, dkv1_ref, dq2_ref, dkv2_ref, xh1_ref, rs1_ref,
             wz_ref, w0_ref, w1_ref, w2_ref, pg0_ref, ds1_ref, ds1b_ref, sums_ref, acc_ref):
        i = pl.program_id(0)
        _to_chunks(acc_ref, ALPHA * ds2_ref[...] + _dot_t(dz_ref[...], wz_ref[...])
                   + group_part(dq0_ref[...], dkv0_ref[...], w0_ref))
        for d, dq_ref, dkv_ref, w_ref in ((d1, dq1_ref, dkv1_ref, w1_ref), (d2, dq2_ref, dkv2_ref, w2_ref)):
            rows = tm // d
            part = group_part(dq_ref[...].reshape(tm, D), dkv_ref[...].reshape(tm, 2 * D), w_ref)
            for r in range(d):
                idx = pl.ds(r, rows, stride=d)
                for cc in range(D // 128):
                    acc_ref[cc, idx, :] = acc_ref[cc, idx, :] + part[r * rows:(r + 1) * rows, cc * 128:(cc + 1) * 128]
        dx1 = jnp.concatenate([acc_ref[cc] for cc in range(D // 128)], axis=1)
        xh1 = xh1_ref[...]
        ds1 = _ln_bwd(dx1 * pg0_ref[...], xh1, rs1_ref[...])
        ds1_ref[...] = ds1
        ds1b_ref[...] = ds1.astype(BF16)

        @pl.when(i == 0)
        def _():
            sums_ref[...] = jnp.zeros_like(sums_ref)
        sums_ref[0:1, :] += _colsum(dx1 * xh1)
        sums_ref[1:2, :] += _colsum(dx1)
        sums_ref[2:3, :] += _colsum(ds1)

    tile = lambda w: pl.BlockSpec((tm, w), lambda i: (i, 0))
    perm = lambda d, w: pl.BlockSpec((d, tm // d, w), lambda i: (0, i, 0))
    whole = pl.BlockSpec(memory_space=pltpu.VMEM)
    return pl.pallas_call(
        body, name="b_dx1_ln1_bwd", grid=(S // tm,),
        in_specs=[tile(D), tile(D), tile(D), tile(2 * D), perm(d1, D), perm(d1, 2 * D), perm(d2, D),
                  perm(d2, 2 * D), tile(D), tile(1), whole, whole, whole, whole,
                  pl.BlockSpec((1, D), lambda i: (0, 0))],
        out_specs=[tile(D), tile(D), pl.BlockSpec((8, D), lambda i: (0, 0))],
        out_shape=[SDS((S, D), F32), SDS((S, D), BF16), SDS((8, D), F32)],
        scratch_shapes=[pltpu.VMEM((D // 128, tm, 128), F32)],
        compiler_params=_params(("arbitrary",)),
    )(ds2, dz2b, dq[0], dkv[0], dq[1].reshape(d1, S // d1, D), dkv[1].reshape(d1, S // d1, 2 * D),
      dq[2].reshape(d2, S // d2, D), dkv[2].reshape(d2, S // d2, 2 * D), xh1, rs1, wz, wg[0], wg[1], wg[2], pg0)


def _a_gate_bwd(ds1b, h, xhu, rsu, wout, lng, lnb, grads, tm):
    S, D = xhu.shape
    n = len(grads)
    steps = S // tm

    def body(ds_ref, z_ref, xhu_ref, rsu_ref, w_ref, lng_ref, lnb_ref, *refs):
        grad_refs, (du1_ref, dzb_ref, sums_ref), recvd_refs = refs[:n], refs[n:n + 3], refs[n + 3:2 * n + 3]
        exchange = (grad_refs, recvd_refs, *refs[2 * n + 3:])
        i = pl.program_id(0)
        pl.when(i == 0)(functools.partial(_exchange_start, _PairExchange, *exchange))
        dv = _dot_t(ds_ref[...], w_ref[...])
        xhu = xhu_ref[...]
        u2 = xhu * lng_ref[...] + lnb_ref[...]
        su = _sigmoid(u2)
        z = z_ref[...]
        sz = _sigmoid(z)
        dz = dv * (u2 * su) * _silu_grad(z, sz)
        du2 = dv * (z * sz) * _silu_grad(u2, su)
        du1 = _ln_bwd(du2 * lng_ref[...], xhu, rsu_ref[...])
        du1_ref[...] = du1
        dzb_ref[...] = dz.astype(BF16)

        @pl.when(i == 0)
        def _():
            sums_ref[...] = jnp.zeros_like(sums_ref)
        sums_ref[0:1, :] += _colsum(du2 * xhu)
        sums_ref[1:2, :] += _colsum(du2)
        sums_ref[2:3, :] += _colsum(du1)
        sums_ref[3:4, :] += _colsum(dz)
        pl.when(i == steps - 1)(functools.partial(_exchange_finish, _PairExchange, *exchange))

    tile = pl.BlockSpec((tm, D), lambda i: (i, 0))
    row = pl.BlockSpec((1, D), lambda i: (0, 0))
    outs = pl.pallas_call(
        body, name="a_gate_bwd", grid=(steps,),
        in_specs=[tile, pl.BlockSpec((tm, D), lambda i: (i, 2)), tile, pl.BlockSpec((tm, 1), lambda i: (i, 0)),
                  pl.BlockSpec((D, D), lambda i: (0, 0)), row, row] + _hbm_specs(n),
        out_specs=[tile, tile, pl.BlockSpec((8, D), lambda i: (0, 0))] + _hbm_specs(n),
        out_shape=[SDS((S, D), F32), SDS((S, D), BF16), SDS((8, D), F32)]
        + [_PairExchange.out_shape(g) for g in grads],
        scratch_shapes=_exchange_scratch(_PairExchange, n),
        compiler_params=_params(("arbitrary",)),
    )(ds1b, h, xhu, rsu, wout, lng, lnb, *grads)
    return outs[0], outs[1], outs[2], outs[3:]


def _a_conv_bwd(du1, h, wdw, parts, tm):
    S, D = du1.shape
    hb = tm // HALO
    last_halo = S // HALO - 1
    n_tiles = S // tm
    n = len(parts)

    def body(du_ref, dun_ref, a_ref, g_ref, ah_ref, gh_ref, wdw_ref, *refs):
        part_refs, (dag_ref, sums_ref, wsum_ref), landed_refs = refs[:n], refs[n:n + 3], refs[n + 3:2 * n + 3]
        dext_ref, ext_ref, dsh_ref, sh_ref, wacc_ref, send_sems, recv_sems = refs[2 * n + 3:]
        scatter = (part_refs, landed_refs, send_sems, recv_sems)
        i = pl.program_id(0)

        @pl.when(i == 0)
        def _():
            _exchange_start(_ChipScatter, *scatter)
            sums_ref[...] = jnp.zeros_like(sums_ref)
            wsum_ref[...] = jnp.zeros_like(wsum_ref)
            wacc_ref[...] = jnp.zeros_like(wacc_ref)
        dext_ref[0:tm, :] = du_ref[...]
        dext_ref[tm:, :] = jnp.where(i < n_tiles - 1, dun_ref[...], 0.0)
        _fill_shifts(dsh_ref, dext_ref)
        _fill_glu_ext(ext_ref, a_ref, g_ref, ah_ref, gh_ref, i > 0)
        _fill_shifts(sh_ref, ext_ref)
        back = [CONV_WIDTH - 1 - k for k in range(CONV_WIDTH)]
        fwd = [HALO - (CONV_WIDTH - 1) + k for k in range(CONV_WIDTH)]
        for cc in range(D // 128):
            lanes = slice(cc * 128, (cc + 1) * 128)
            hi_lanes = slice(D + cc * 128, D + (cc + 1) * 128)
            sa = jnp.zeros((1, 128), F32)
            sg = jnp.zeros((1, 128), F32)
            for r0 in range(0, tm, CONV_ROWS):
                acc = jnp.zeros((CONV_ROWS, 128), F32)
                for k, win in _tap_windows(dext_ref, dsh_ref, back, r0, CONV_ROWS, lanes):
                    acc = acc + wdw_ref[k:k + 1, lanes] * win
                a = a_ref[r0:r0 + CONV_ROWS, lanes]
                s = _sigmoid(g_ref[r0:r0 + CONV_ROWS, lanes])
                da = acc * s
                dg = acc * a * s * (1.0 - s)
                dag_ref[r0:r0 + CONV_ROWS, lanes] = da.astype(BF16)
                dag_ref[r0:r0 + CONV_ROWS, hi_lanes] = dg.astype(BF16)
                sa = sa + _colsum(da)
                sg = sg + _colsum(dg)
                du = du_ref[r0:r0 + CONV_ROWS, lanes]
                for k, win in _tap_windows(ext_ref, sh_ref, fwd, r0, CONV_ROWS, lanes):
                    p = du * win
                    fold = p[0:8]
                    for q in range(8, CONV_ROWS, 8):
                        fold = fold + p[q:q + 8]
                    wacc_ref[k, :, lanes] += fold
            sums_ref[0:1, lanes] += sa
            sums_ref[1:2, lanes] += sg

        @pl.when(i == n_tiles - 1)
        def _():
            for k in range(CONV_WIDTH):
                wsum_ref[k:k + 1, :] = _colsum(wacc_ref[k])
            _exchange_finish(_ChipScatter, *scatter)

    tile = lambda c: pl.BlockSpec((tm, D), lambda i, c=c: (i, c))
    halo = lambda c: pl.BlockSpec((HALO, D), lambda i, c=c: (jnp.maximum(i * hb - 1, 0), c))
    outs = pl.pallas_call(
        body, name="a_conv_bwd", grid=(n_tiles,),
        in_specs=[tile(0), pl.BlockSpec((HALO, D), lambda i: (jnp.minimum((i + 1) * hb, last_halo), 0)),
                  tile(0), tile(1), halo(0), halo(1), pl.BlockSpec((HALO, D), lambda i: (0, 0))] + _hbm_specs(n),
        out_specs=[pl.BlockSpec((tm, 2 * D), lambda i: (i, 0)), pl.BlockSpec((8, D), lambda i: (0, 0)),
                   pl.BlockSpec((HALO, D), lambda i: (0, 0))] + _hbm_specs(n),
        out_shape=[SDS((S, 2 * D), BF16), SDS((8, D), F32), SDS((HALO, D), F32)]
        + [SDS((3,) + p.shape[1:], p.dtype) for p in parts],
        scratch_shapes=[pltpu.VMEM((tm + HALO, D), F32), pltpu.VMEM((HALO + tm, D), F32),
                        pltpu.VMEM((7, HALO + tm - 8, D), F32), pltpu.VMEM((7, HALO + tm - 8, D), F32),
                        pltpu.VMEM((HALO, 8, D), F32)] + _exchange_scratch(_ChipScatter, n),
        compiler_params=_params(("arbitrary",)),
    )(du1, du1, h, h, h, h, wdw, *parts)
    return outs[0], outs[1], outs[2], outs[3:]


def _a_dx(ds1, dag, dzb, w_in, parts, tm):
    S, D = ds1.shape
    n = len(parts)
    steps = S // tm

    def body(ds_ref, dag_ref, dz_ref, w_ref, *refs):
        part_refs, o_ref, landed_refs = refs[:n], refs[n], refs[n + 1:2 * n + 1]
        scatter = (part_refs, landed_refs, *refs[2 * n + 1:])
        i = pl.program_id(0)
        pl.when(i == 0)(functools.partial(_exchange_start, _ChipScatter, *scatter))
        o_ref[...] = (ALPHA * ds_ref[...] + _dot_t(dag_ref[...], w_ref[:, 0:2 * D])
                      + _dot_t(dz_ref[...], w_ref[:, 2 * D:3 * D]))
        pl.when(i == steps - 1)(functools.partial(_exchange_finish, _ChipScatter, *scatter))

    tile = lambda w: pl.BlockSpec((tm, w), lambda i: (i, 0))
    outs = pl.pallas_call(
        body, name="a_dx", grid=(steps,),
        in_specs=[tile(D), tile(2 * D), tile(D), pl.BlockSpec(memory_space=pltpu.VMEM)] + _hbm_specs(n),
        out_specs=[tile(D)] + _hbm_specs(n),
        out_shape=[SDS((S, D), F32)] + [_ChipScatter.out_shape(p) for p in parts],
        scratch_shapes=_exchange_scratch(_ChipScatter, n),
        compiler_params=_params(("arbitrary",)),
    )(ds1, dag, dzb, w_in, *parts)
    return outs[0], outs[1:]


def _halves(w):
    return w.reshape(2, w.shape[0] // 2, w.shape[1])


def _unstack_cols(w4):
    return jnp.transpose(w4, (1, 0, 2)).reshape(w4.shape[1], N_CHIPS * w4.shape[2])


def _stack_cols(w):
    D, n = w.shape
    return jnp.transpose(w.reshape(D, N_CHIPS, n // N_CHIPS), (1, 0, 2))


def _pack_rows(rows, width, total=SMALL_ROWS):
    slab = jnp.concatenate([r.reshape(-1, width) for r in rows], axis=0)
    return jnp.pad(slab, ((0, total - slab.shape[0]), (0, 0)))


def _by_row(gathered, rows, dq4):
    return jnp.transpose(gathered.reshape(N_CHIPS, rows, dq4), (1, 0, 2))


def kernel(x, a_w_in, a_b_in, a_w_dw, a_b_dw, a_ln_g, a_ln_b, a_w_out, a_b_out, kv_w, b_w_in, b_w_out, b_b_out, post_ln_g, post_ln_b, loss_target, m_a_w_in, m_a_b_in, m_a_w_dw, m_a_b_dw, m_a_ln_g, m_a_ln_b, m_a_w_out, m_a_b_out, m_kv_w, m_b_w_in, m_b_w_out, m_b_b_out, m_post_ln_g, m_post_ln_b, v_a_w_in, v_a_b_in, v_a_w_dw, v_a_b_dw, v_a_ln_g, v_a_ln_b, v_a_w_out, v_a_b_out, v_kv_w, v_b_w_in, v_b_w_out, v_b_b_out, v_post_ln_g, v_post_ln_b):
    S, D = x.shape[1], x.shape[2]
    dq4 = D // N_CHIPS
    tm = 256
    tm_mm = min(S, 2048)
    x2 = x.reshape(S, D)
    target = loss_target.reshape(S, D)
    jchip = 2 * lax.axis_index("x") + lax.axis_index("y")

    big_local = [a_w_in[0], kv_w, b_w_in[0], a_w_out[0], b_w_out[0]]
    wire = [_halves(w.astype(BF16)) for w in big_local]
    whole = lambda g: g.reshape((N_CHIPS, 2 * g.shape[2], g.shape[3]))
    first_rows = 16
    small_first = _pack_rows([a_b_in.reshape(3, dq4)], dq4, first_rows)
    small_rest = _pack_rows([jnp.pad(a_w_dw[0], ((0, 1), (0, 0))), a_b_dw, a_ln_g, a_ln_b, a_b_out], dq4)
    gathered = _all_gather_chips([wire[0], _halves(small_first)])
    w_in4 = whole(gathered[0])
    w_in_a = _unstack_cols(w_in4)
    b_in_full = jnp.transpose(_by_row(gathered[1], first_rows, dq4)[0:3], (1, 0, 2)).reshape(1, 3 * D)
    pg0, pg1 = post_ln_g[0:1], post_ln_g[1:2]
    pb0, pb1 = post_ln_b[0:1], post_ln_b[1:2]

    h, xb, later = _a_in_proj(x2, w_in4, b_in_full, [wire[1], wire[2], wire[4], wire[3], _halves(small_rest)], tm_mm)
    kv_full = _unstack_cols(whole(later[0]))
    b_in4 = whole(later[1])
    w_out_b = later[2].reshape(D, D)
    w_out_a = later[3].reshape(D, D)
    small = _by_row(later[4], SMALL_ROWS, dq4)
    wdw_full = small[0:HALO].reshape(HALO, D)
    bdw_full, lng_full, lnb_full, bout_a_full = [small[HALO + q].reshape(1, D) for q in range(4)]
    w_z = b_in4[3]
    w_g = [jnp.concatenate([b_in4[g], kv_full[:, g * D:(g + 1) * D], kv_full[:, (3 + g) * D:(4 + g) * D]], axis=1)
           for g in range(3)]
    xhu, rsu, vb, xh1, rs1, x1b, x1p1, x1p2 = _a_conv_out(
        h, x2, wdw_full, bdw_full, lng_full, lnb_full, w_out_a, bout_a_full, pg0, pb0, tm)
    x1g = [x1b, x1p1.reshape(S, D), x1p2.reshape(S, D)]
    qkv = [_mm_nn("b_qkv_g%d" % g, x1g[g], w_g[g], BF16, tm_mm, scale_first_tile=HEAD_DIM ** -0.5)
           for g in range(3)]
    z2 = _mm_nn("b_gate_proj", x1b, w_z, F32, tm_mm)
    og, lg = zip(*[_attn_fwd(g, qkv[g], D) for g in range(3)])
    d1, d2 = DILATIONS[1], DILATIONS[2]
    (v2b, ds2, ds2b, dz2b, da0, da1, da2, ls0, ls1, ls2, dl0, dl1, dl2, loss_acc, sums_b) = _b_merge_out_loss(
        og[0], og[1].reshape(d1, S // d1, D), og[2].reshape(d2, S // d2, D),
        lg[0], lg[1].reshape(d1, S // d1, LSE_LANES), lg[2].reshape(d2, S // d2, LSE_LANES),
        z2, xh1, target, w_out_b, b_b_out, pg0, pb0, pg1, pb1, tm)

    das = [da0, da1.reshape(S, D), da2.reshape(S, D)]
    lss = [ls0, ls1.reshape(S, LSE_LANES), ls2.reshape(S, LSE_LANES)]
    dls = [dl0, dl1.reshape(S, LSE_LANES), dl2.reshape(S, LSE_LANES)]
    dq, dkv = zip(*[_attn_bwd(g, qkv[g], das[g], lss[g], dls[g], D) for g in range(3)])
    ds1, ds1b, sums_1 = _b_dx1_ln1_bwd(ds2, dz2b, dq, dkv, xh1, rs1, w_z, w_g, pg0, tm)

    def by_chip_cols(gw):
        s4 = _stack_cols(gw)
        return s4.reshape(N_CHIPS, 2, D // 2, s4.shape[2])

    def by_chip_rows(gw):
        return gw.reshape(N_CHIPS, 2, D // 8, D)

    core = lax.axis_index("c").astype(jnp.int32).reshape(1)
    chip = jchip.astype(jnp.int32).reshape(1)

    g_w_out_b = _mm_tn("dw_b_out", v2b, ds2b, tm_mm)
    g_q = [_mm_tn("dw_b_q_g%d" % g, x1g[g], dq[g], tm_mm) for g in range(3)]
    g_z = _mm_tn("dw_b_z", x1b, dz2b, tm_mm)
    g_kvg = [_mm_tn("dw_kv_g%d" % g, x1g[g], dkv[g], tm_mm) for g in range(3)]
    g_kv = jnp.concatenate([t[:, :D] for t in g_kvg] + [t[:, D:] for t in g_kvg], axis=1)
    grads_b = [by_chip_cols(g_kv), jnp.stack(g_q + [g_z]).reshape(N_CHIPS, 2, D // 2, D), by_chip_rows(g_w_out_b)]
    du1, dzab, sums_a, recvd_b = _a_gate_bwd(ds1b, h, xhu, rsu, w_out_a, lng_full, lnb_full, grads_b, tm)
    parts_b, wire_b = _pair_sum(grads_b, recvd_b, core, "b")
    dag, sums_c, wsum, landed_b = _a_conv_bwd(du1, h, wdw_full, wire_b, tm)
    own_b = _chip_sum(parts_b, landed_b, chip, "b")
    other_b = _pair_share(own_b, "b")

    small_grads = _pack_rows([sums_c[0:1], sums_c[1:2], sums_a[3:4], wsum, sums_a[2:3], sums_a[0:1], sums_a[1:2],
                              sums_1[2:3], sums_b[2:3], sums_1[0:1], sums_b[0:1], sums_1[1:2], sums_b[1:2]], D)
    g_w_in_ag, small_all = _mm_tn("dw_a_in_ag", xb, dag, tm_mm, small_grads)
    g_w_in = jnp.concatenate([g_w_in_ag, _mm_tn("dw_a_in_z", xb, dzab, tm_mm)], axis=1)
    g_w_out_a = _mm_tn("dw_a_out", vb, ds1b, tm_mm)
    grads_a = [by_chip_cols(g_w_in), by_chip_rows(g_w_out_a)]
    recvd_a = _exchange(_PairExchange, "grad_pair_exchange_a", grads_a)
    parts_a, wire_a = _pair_sum(grads_a, recvd_a, core, "a")
    grad_x, landed_a = _a_dx(ds1, dag, dzab, w_in_a, wire_a, 2 * tm)
    own_a = _chip_sum(parts_a, landed_a, chip, "a")
    other_a = _pair_share(own_a, "a")
    own_half = [own_a[0], own_b[0], own_b[1], own_a[1], own_b[2]]
    other_half = [other_a[0], other_b[0], other_b[1], other_a[1], other_b[2]]

    small_sum = _sum_devices(small_all)
    loss = lax.psum(loss_acc[0, 0], ("x", "y", "c"))

    big_m = [m_a_w_in[0], m_kv_w, m_b_w_in[0], m_a_w_out[0], m_b_w_out[0]]
    big_v = [v_a_w_in[0], v_kv_w, v_b_w_in[0], v_a_w_out[0], v_b_w_out[0]]
    shards, big_delta, big_new_m, big_new_v = [
        [a.reshape(2 * a.shape[1], a.shape[2]) for a in group] for group in _adamw_halves(
            [_halves(w) for w in big_local], own_half, other_half, [_halves(m) for m in big_m],
            [_halves(v) for v in big_v], core)]

    def chip_cols(rows):
        return lax.dynamic_slice_in_dim(rows, jchip * dq4, dq4, axis=1)

    g_b_in = lax.dynamic_slice_in_dim(small_sum[0:3].reshape(1, 3 * D), jchip * 3 * dq4, 3 * dq4, axis=1)
    small_g = [g_b_in, chip_cols(small_sum[3:3 + CONV_WIDTH]), chip_cols(small_sum[35:36]), chip_cols(small_sum[36:37]),
               chip_cols(small_sum[37:38]), chip_cols(small_sum[38:39]), small_sum[39:40], small_sum[40:42],
               small_sum[42:44]]
    small_w = [a_b_in, a_w_dw[0], a_b_dw, a_ln_g, a_ln_b, a_b_out, b_b_out, post_ln_g, post_ln_b]
    small_m = [m_a_b_in, m_a_w_dw[0], m_a_b_dw, m_a_ln_g, m_a_ln_b, m_a_b_out, m_b_b_out, m_post_ln_g, m_post_ln_b]
    small_v = [v_a_b_in, v_a_w_dw[0], v_a_b_dw, v_a_ln_g, v_a_ln_b, v_a_b_out, v_b_b_out, v_post_ln_g, v_post_ln_b]
    small_delta, small_new_m, small_new_v = _adamw("adamw_small", small_w, small_g, small_m, small_v, 1)

    def ordered(big, sm):
        return (big[0][None], sm[0], sm[1][None], sm[2], sm[3], sm[4], big[3][None], sm[5], big[1], big[2][None],
                big[4][None], sm[6], sm[7], sm[8])

    return (loss, grad_x.reshape(1, S, D), *ordered(shards, small_g), *ordered(big_delta, small_delta),
            *ordered(big_new_m, small_new_m), *ordered(big_new_v, small_new_v))
```
